```python
import jax, jax.numpy as jnp
from jax import lax
import numpy as np

D_MODEL = 2048
BATCH = 2
SEQ = 16384
DEPTH = 4

GRID_W = 64
CTX_LEN = 256
N_MIXERS = 3
N_A_LAYERS = (DEPTH + 2) // 3
N_B_LAYERS = (DEPTH + 1) // 3
N_C_LAYERS = DEPTH // 3
NA_HEADS = 16
NA_HEAD_DIM = D_MODEL // NA_HEADS
NA_WIN_ROWS = 8
NA_WIN_COLS = 16
MLA_HEADS = 16
MLA_Q_RANK = 512
MLA_KV_RANK = 512
MLA_NOPE_DIM = 128
MLA_ROPE_DIM = 64
MLA_V_DIM = 128
GQA_HEADS = 16
GQA_KV_HEADS = 4
GQA_HEAD_DIM = 128
FFN_DIM = 5632
CONV_WIDTH = 3
ROPE_THETA = 10000.0
Q_BLOCK = 128
NORM_EPS = 1e-6
ADA_INIT = 0.5

kernel_name = 'hybrid_na_mla_gqa_convglu_dit'


def rms_norm(x, g):
    xf = x.astype(jnp.float32)
    y = xf * lax.rsqrt(jnp.mean(xf * xf, axis=-1, keepdims=True) + NORM_EPS)
    return (y * g.astype(jnp.float32)).astype(x.dtype)


def axial_rope(n_tok, rot_dim, dtype):
    quarter = rot_dim // 4
    t = jnp.arange(n_tok)
    row = (t // GRID_W).astype(jnp.float32)
    col = (t % GRID_W).astype(jnp.float32)
    inv = ROPE_THETA ** (-jnp.arange(quarter, dtype=jnp.float32) / quarter)
    ar = row[:, None] * inv
    ac = col[:, None] * inv
    ang = jnp.concatenate([ar, ar, ac, ac], axis=-1)
    return jnp.cos(ang).astype(dtype), jnp.sin(ang).astype(dtype)


def rotate_half_axial(x):
    x0, x1, x2, x3 = jnp.split(x, 4, axis=-1)
    return jnp.concatenate([-x1, x0, -x3, x2], axis=-1)


def apply_rope(x, cos, sin):
    return x * cos[None, :, None, :] + rotate_half_axial(x) * sin[None, :, None, :]


def attend_blocks(q, k, v, kc, vc, scale):
    B, S, G, R, Dk = q.shape
    nb = S // Q_BLOCK
    qb = q.reshape(B, nb, Q_BLOCK, G, R, Dk).transpose(1, 0, 2, 3, 4, 5)

    def one_block(qblk):
        s_lat = jnp.einsum('bqgrd,bkgd->bgrqk', qblk, k)
        s_ctx = jnp.einsum('bqgrd,bcgd->bgrqc', qblk, kc)
        s = jnp.concatenate([s_lat, s_ctx], axis=-1).astype(jnp.float32) * scale
        p = jax.nn.softmax(s, axis=-1).astype(v.dtype)
        return (jnp.einsum('bgrqk,bkgd->bqgrd', p[..., :S], v)
                + jnp.einsum('bgrqc,bcgd->bqgrd', p[..., S:], vc))

    o = lax.map(one_block, qb)
    return o.transpose(1, 0, 2, 3, 4, 5).reshape(B, S, -1)


def dense_attend(q, k, v, scale):
    B, L = q.shape[:2]
    s = jnp.einsum('bqgrd,bkgd->bgrqk', q, k).astype(jnp.float32) * scale
    p = jax.nn.softmax(s, axis=-1).astype(v.dtype)
    return jnp.einsum('bgrqk,bkgd->bqgrd', p, v).reshape(B, L, -1)


def neighborhood_attend(q, k, v, kc, vc, rpb, rows):
    B, S, H, Dh = q.shape
    kr = min(NA_WIN_ROWS, rows)
    n_loc = kr * NA_WIN_COLS
    scale = Dh ** -0.5
    qg = q.reshape(B, rows, GRID_W, H, Dh)
    kg = k.reshape(B, rows, GRID_W, H, Dh)
    vg = v.reshape(B, rows, GRID_W, H, Dh)
    col = jnp.arange(GRID_W)
    cstart = jnp.clip(col - NA_WIN_COLS // 2, 0, GRID_W - NA_WIN_COLS)
    cidx = cstart[:, None] + jnp.arange(NA_WIN_COLS)[None, :]
    dc = cidx - col[:, None] + (NA_WIN_COLS - 1)

    def one_row(r):
        rstart = jnp.clip(r - kr // 2, 0, rows - kr)
        q_r = lax.dynamic_index_in_dim(qg, r, axis=1, keepdims=False)
        k_band = lax.dynamic_slice_in_dim(kg, rstart, kr, axis=1)
        v_band = lax.dynamic_slice_in_dim(vg, rstart, kr, axis=1)
        k_win = k_band[:, :, cidx]
        v_win = v_band[:, :, cidx]
        dr = rstart + jnp.arange(kr) - r + (NA_WIN_ROWS - 1)
        bias = rpb[:, dr[:, None, None], dc[None, :, :]].transpose(0, 2, 1, 3)
        s_loc = jnp.einsum('bqhd,bkqjhd->bhqkj', q_r, k_win).astype(jnp.float32) * scale
        s_loc = (s_loc + bias[None].astype(jnp.float32)).reshape(B, H, GRID_W, n_loc)
        s_ctx = jnp.einsum('bqhd,bchd->bhqc', q_r, kc).astype(jnp.float32) * scale
        p = jax.nn.softmax(jnp.concatenate([s_loc, s_ctx], axis=-1), axis=-1).astype(v.dtype)
        p_loc = p[..., :n_loc].reshape(B, H, GRID_W, kr, NA_WIN_COLS)
        return (jnp.einsum('bhqkj,bkqjhd->bqhd', p_loc, v_win)
                + jnp.einsum('bhqc,bchd->bqhd', p[..., n_loc:], vc))

    o = lax.map(one_row, jnp.arange(rows))
    return o.transpose(1, 0, 2, 3, 4).reshape(B, S, H, Dh)


def na_mixer(h, hc, wqkv, rpb, wo, rows, need_ctx):
    B, N, D = h.shape
    L = hc.shape[1]
    qkv = (h @ wqkv).reshape(B, N, 3, NA_HEADS, NA_HEAD_DIM)
    qkv_c = (hc @ wqkv).reshape(B, L, 3, NA_HEADS, NA_HEAD_DIM)
    kc, vc = qkv_c[:, :, 1], qkv_c[:, :, 2]
    o = neighborhood_attend(qkv[:, :, 0], qkv[:, :, 1], qkv[:, :, 2], kc, vc, rpb, rows)
    y = o.reshape(B, N, D) @ wo
    yc = None
    if need_ctx:
        oc = dense_attend(qkv_c[:, :, 0][:, :, :, None], kc, vc, NA_HEAD_DIM ** -0.5)
        yc = oc @ wo
    return y, yc


def mla_mixer(h, hc, wq_a, q_norm, wq_b, wkv_a, kv_norm, wkv_b, wo, cos, sin, need_ctx):
    def project(t, rot):
        B, N, _ = t.shape
        q = (rms_norm(t @ wq_a, q_norm) @ wq_b).reshape(B, N, MLA_HEADS, MLA_NOPE_DIM + MLA_ROPE_DIM)
        q_nope, q_pe = q[..., :MLA_NOPE_DIM], q[..., MLA_NOPE_DIM:]
        kv_a = t @ wkv_a
        c_kv, k_pe = kv_a[..., :MLA_KV_RANK], kv_a[..., MLA_KV_RANK:]
        kv = (rms_norm(c_kv, kv_norm) @ wkv_b).reshape(B, N, MLA_HEADS, MLA_NOPE_DIM + MLA_V_DIM)
        k_nope, v = kv[..., :MLA_NOPE_DIM], kv[..., MLA_NOPE_DIM:]
        k_pe = k_pe[:, :, None, :]
        if rot is not None:
            q_pe = apply_rope(q_pe, *rot)
            k_pe = apply_rope(k_pe, *rot)
        k_pe = jnp.broadcast_to(k_pe, (B, N, MLA_HEADS, MLA_ROPE_DIM))
        q = jnp.concatenate([q_nope, q_pe], axis=-1)[:, :, :, None]
        k = jnp.concatenate([k_nope, k_pe], axis=-1)
        return q, k, v

    scale = (MLA_NOPE_DIM + MLA_ROPE_DIM) ** -0.5
    q, k, v = project(h, (cos, sin))
    qc, kc, vc = project(hc, None)
    y = attend_blocks(q, k, v, kc, vc, scale) @ wo
    yc = dense_attend(qc, kc, vc, scale) @ wo if need_ctx else None
    return y, yc


def gqa_mixer(h, hc, wqkv, q_norm, k_norm, wo, cos, sin, need_ctx):
    rep = GQA_HEADS // GQA_KV_HEADS
    split = [GQA_HEADS * GQA_HEAD_DIM, (GQA_HEADS + GQA_KV_HEADS) * GQA_HEAD_DIM]

    def project(t, rot):
        B, N, _ = t.shape
        q, k, v = jnp.split(t @ wqkv, split, axis=-1)
        q = rms_norm(q.reshape(B, N, GQA_HEADS, GQA_HEAD_DIM), q_norm)
        k = rms_norm(k.reshape(B, N, GQA_KV_HEADS, GQA_HEAD_DIM), k_norm)
        v = v.reshape(B, N, GQA_KV_HEADS, GQA_HEAD_DIM)
        if rot is not None:
            q = apply_rope(q, *rot)
            k = apply_rope(k, *rot)
        return q.reshape(B, N, GQA_KV_HEADS, rep, GQA_HEAD_DIM), k, v

    scale = GQA_HEAD_DIM ** -0.5
    q, k, v = project(h, (cos, sin))
    qc, kc, vc = project(hc, None)
    y = attend_blocks(q, k, v, kc, vc, scale) @ wo
    yc = dense_attend(qc, kc, vc, scale) @ wo if need_ctx else None
    return y, yc


def dwconv_seq(u, w, b):
    n = u.shape[1]
    pad = CONV_WIDTH // 2
    up = jnp.pad(u, ((0, 0), (pad, pad), (0, 0)))
    y = b
    for tap in range(CONV_WIDTH):
        y = y + up[:, tap:tap + n] * w[tap]
    return y


def conv_glu(h, w_up, conv_w, conv_b, w_down):
    gate, val = jnp.split(h @ w_up, 2, axis=-1)
    return (jax.nn.silu(dwconv_seq(gate, conv_w, conv_b)) * val) @ w_down


def setup_inputs(seed: int = 0) -> dict:
    key = jax.random.key(seed)
    ks = jax.random.split(key, 32)
    D = D_MODEL

    def nrm(i, shape, scale):
        return jax.random.normal(ks[i], shape, jnp.float32) * scale

    def gain(i, shape):
        return 1.0 + nrm(i, shape, 0.02)

    return {
        'x': nrm(0, (BATCH, SEQ, D), 1.0),
        'c': nrm(1, (BATCH, D), 1.0),
        'ctx': nrm(2, (BATCH, CTX_LEN, D), 1.0),
        'c_ctx': nrm(3, (D,), 1.0),
        'ada_w': nrm(4, (DEPTH, D, 6 * D), ADA_INIT * D ** -0.5),
        'ada_b': nrm(5, (DEPTH, 6 * D), 0.01),
        'mix_norm': gain(6, (DEPTH, D)),
        'ffn_norm': gain(7, (DEPTH, D)),
        'ffn_up': nrm(8, (DEPTH, D, 2 * FFN_DIM), D ** -0.5),
        'ffn_conv_w': nrm(9, (DEPTH, CONV_WIDTH, FFN_DIM), CONV_WIDTH ** -0.5),
        'ffn_conv_b': nrm(10, (DEPTH, FFN_DIM), 0.01),
        'ffn_down': nrm(11, (DEPTH, FFN_DIM, D), FFN_DIM ** -0.5),
        'na_wqkv': nrm(12, (N_A_LAYERS, D, 3 * D), D ** -0.5),
        'na_rpb': nrm(13, (N_A_LAYERS, NA_HEADS, 2 * NA_WIN_ROWS - 1, 2 * NA_WIN_COLS - 1), 0.1),
        'na_wo': nrm(14, (N_A_LAYERS, D, D), D ** -0.5),
        'mla_wq_a': nrm(15, (N_B_LAYERS, D, MLA_Q_RANK), D ** -0.5),
        'mla_q_norm': gain(16, (N_B_LAYERS, MLA_Q_RANK)),
        'mla_wq_b': nrm(17, (N_B_LAYERS, MLA_Q_RANK, MLA_HEADS * (MLA_NOPE_DIM + MLA_ROPE_DIM)), MLA_Q_RANK ** -0.5),
        'mla_wkv_a': nrm(18, (N_B_LAYERS, D, MLA_KV_RANK + MLA_ROPE_DIM), D ** -0.5),
        'mla_kv_norm': gain(19, (N_B_LAYERS, MLA_KV_RANK)),
        'mla_wkv_b': nrm(20, (N_B_LAYERS, MLA_KV_RANK, MLA_HEADS * (MLA_NOPE_DIM + MLA_V_DIM)), MLA_KV_RANK ** -0.5),
        'mla_wo': nrm(21, (N_B_LAYERS, MLA_HEADS * MLA_V_DIM, D), (MLA_HEADS * MLA_V_DIM) ** -0.5),
        'gqa_wqkv': nrm(22, (N_C_LAYERS, D, (GQA_HEADS + 2 * GQA_KV_HEADS) * GQA_HEAD_DIM), D ** -0.5),
        'gqa_q_norm': gain(23, (N_C_LAYERS, GQA_HEAD_DIM)),
        'gqa_k_norm': gain(24, (N_C_LAYERS, GQA_HEAD_DIM)),
        'gqa_wo': nrm(25, (N_C_LAYERS, GQA_HEADS * GQA_HEAD_DIM, D), (GQA_HEADS * GQA_HEAD_DIM) ** -0.5),
        'final_norm': gain(26, (D,)),
    }


def reference(x, c, ctx, c_ctx, ada_w, ada_b, mix_norm, ffn_norm, ffn_up, ffn_conv_w, ffn_conv_b, ffn_down,
              na_wqkv, na_rpb, na_wo, mla_wq_a, mla_q_norm, mla_wq_b, mla_wkv_a, mla_kv_norm, mla_wkv_b, mla_wo,
              gqa_wqkv, gqa_q_norm, gqa_k_norm, gqa_wo, final_norm):
    n_tok = x.shape[1]
    rows = n_tok // GRID_W
    cos_b, sin_b = axial_rope(n_tok, MLA_ROPE_DIM, x.dtype)
    cos_c, sin_c = axial_rope(n_tok, GQA_HEAD_DIM, x.dtype)
    xc = ctx
    for i in range(DEPTH):
        last = i == DEPTH - 1
        mod = (jax.nn.silu(c) @ ada_w[i] + ada_b[i])[:, None, :]
        mod_c = jax.nn.silu(c_ctx) @ ada_w[i] + ada_b[i]
        sh1, sc1, g1, sh2, sc2, g2 = jnp.split(mod, 6, axis=-1)
        csh1, csc1, cg1, csh2, csc2, cg2 = jnp.split(mod_c, 6, axis=-1)
        h = rms_norm(x, mix_norm[i]) * (1 + sc1) + sh1
        hc = rms_norm(xc, mix_norm[i]) * (1 + csc1) + csh1
        kind, j = i % N_MIXERS, i // N_MIXERS
        if kind == 0:
            y, yc = na_mixer(h, hc, na_wqkv[j], na_rpb[j], na_wo[j], rows, not last)
        elif kind == 1:
            y, yc = mla_mixer(h, hc, mla_wq_a[j], mla_q_norm[j], mla_wq_b[j], mla_wkv_a[j], mla_kv_norm[j],
                              mla_wkv_b[j], mla_wo[j], cos_b, sin_b, not last)
        else:
            y, yc = gqa_mixer(h, hc, gqa_wqkv[j], gqa_q_norm[j], gqa_k_norm[j], gqa_wo[j],
                              cos_c, sin_c, not last)
        x = x + g1 * y
        hf = rms_norm(x, ffn_norm[i]) * (1 + sc2) + sh2
        x = x + g2 * conv_glu(hf, ffn_up[i], ffn_conv_w[i], ffn_conv_b[i], ffn_down[i])
        if not last:
            xc = xc + cg1 * yc
            hfc = rms_norm(xc, ffn_norm[i]) * (1 + csc2) + csh2
            xc = xc + cg2 * conv_glu(hfc, ffn_up[i], ffn_conv_w[i], ffn_conv_b[i], ffn_down[i])
    return rms_norm(x, final_norm)
```

```python
import functools

import jax
import jax.numpy as jnp
from jax import lax
from jax.experimental import pallas as pl
from jax.experimental.pallas import tpu as pltpu

F32 = jnp.float32
BF16 = jnp.bfloat16

GRID_W = 64
NA_HEADS = 16
NA_WIN_ROWS = 8
NA_WIN_COLS = 16
MLA_HEADS = 16
MLA_NOPE_DIM = 128
MLA_ROPE_DIM = 64
MLA_V_DIM = 128
GQA_HEADS = 16
GQA_KV_HEADS = 4
GQA_HEAD_DIM = 128
CONV_WIDTH = 3
ROPE_THETA = 10000.0
NORM_EPS = 1e-6
N_MIXERS = 3

LANES = 128
ROW_TILE = 512
HALO = 8
NA_TILE_ROWS = 4
MASK_VALUE = -1e30
VMEM_LIMIT = 56 * 1024 * 1024


def _params(*sem):
    return pltpu.CompilerParams(dimension_semantics=sem, vmem_limit_bytes=VMEM_LIMIT)


def _rms(x, g):
    y = x * lax.rsqrt(jnp.mean(x * x, axis=-1, keepdims=True) + NORM_EPS)
    return y * g


def _rope(y, cos, sin_a, sin_b, quarter):
    return (y * cos + pltpu.roll(y, LANES - quarter, 1) * sin_a
            + pltpu.roll(y, quarter, 1) * sin_b)


def _mod_kernel(c_ref, w_ref, b_ref, o_ref):
    s = jax.nn.silu(c_ref[...])
    o_ref[0] = jnp.dot(s.astype(BF16), w_ref[0].astype(BF16), preferred_element_type=F32) + b_ref[0]


def _modulation(c_rows, ada_w, ada_b):
    depth, d, n = ada_w.shape
    bn = n // 8
    return pl.pallas_call(
        _mod_kernel,
        grid=(depth, n // bn),
        in_specs=[pl.BlockSpec((8, d), lambda l, j: (0, 0)),
                  pl.BlockSpec((1, d, bn), lambda l, j: (l, 0, j)),
                  pl.BlockSpec((1, 1, bn), lambda l, j: (l, 0, j))],
        out_specs=pl.BlockSpec((1, 8, bn), lambda l, j: (l, 0, j)),
        out_shape=jax.ShapeDtypeStruct((depth, 8, n), F32),
        compiler_params=_params("arbitrary", "arbitrary"),
        name="adaln_mod",
    )(c_rows, ada_w, ada_b.reshape(depth, 1, n))


def _norm_matmul_kernel(*refs, has_mod, n_extra, epilogue):
    if has_mod:
        x_ref, g_ref, sc_ref, sh_ref, w_ref = refs[:5]
        rest = refs[5:]
    else:
        x_ref, g_ref, w_ref = refs[:3]
        rest = refs[3:]
    extras, outs, h_sc = rest[:n_extra], rest[n_extra:-1], rest[-1]
    n = pl.program_id(1)

    @pl.when(n == 0)
    def _():
        h = _rms(x_ref[...], g_ref[...])
        if has_mod:
            h = h * (1.0 + sc_ref[0]) + sh_ref[0]
        h_sc[...] = h.astype(BF16)

    acc = jnp.dot(h_sc[...], w_ref[...], preferred_element_type=F32)
    epilogue(acc, n, extras, outs)


def _norm_matmul(x, xcol, kdim, g, mod, mod_chunks, w, bn, extras, extra_specs, out_shapes, out_specs,
                 epilogue, n_row_tiles, mod_row, name):
    bm = ROW_TILE
    n_col = w.shape[1] // bn
    has_mod = mod is not None
    in_specs = [pl.BlockSpec((bm, kdim), lambda m, n: (m, xcol)),
                pl.BlockSpec((1, kdim), lambda m, n: (0, 0))]
    args = [x, g.reshape(1, kdim)]
    if has_mod:
        sc_chunk, sh_chunk = mod_chunks
        in_specs += [pl.BlockSpec((1, 1, kdim), lambda m, n: (mod_row(m), 0, sc_chunk)),
                     pl.BlockSpec((1, 1, kdim), lambda m, n: (mod_row(m), 0, sh_chunk))]
        args += [mod, mod]
    in_specs.append(pl.BlockSpec((kdim, bn), lambda m, n: (0, n)))
    args.append(w)
    in_specs += extra_specs
    args += extras
    kern = functools.partial(_norm_matmul_kernel, has_mod=has_mod, n_extra=len(extras), epilogue=epilogue)
    return pl.pallas_call(
        kern,
        grid=(n_row_tiles, n_col),
        in_specs=in_specs,
        out_specs=out_specs,
        out_shape=out_shapes,
        scratch_shapes=[pltpu.VMEM((bm, kdim), BF16)],
        compiler_params=_params("arbitrary", "arbitrary"),
        name=name,
    )(*args)


def _epi_colscale(acc, n, extras, outs):
    (cs_ref,), (o_ref,) = extras, outs
    o_ref[...] = (acc * cs_ref[...]).astype(o_ref.dtype)


def _epi_plain(acc, n, extras, outs):
    (o_ref,) = outs
    o_ref[...] = acc.astype(o_ref.dtype)


def _epi_mla_a(acc, n, extras, outs):
    cos_ref, sa_ref, sb_ref = extras
    qc_ref, kpe_ref = outs
    wide = qc_ref.shape[1]
    qc_ref[...] = acc[:, :wide]
    kpe = _rope(acc[:, wide:], cos_ref[...], sa_ref[...], sb_ref[...], MLA_ROPE_DIM // 4)
    kpe_ref[...] = kpe.astype(kpe_ref.dtype)


def _epi_mla_q(acc, n, extras, outs, *, scale):
    cos_ref, sa_ref, sb_ref = extras
    (o_ref,) = outs
    for c in range(acc.shape[1] // LANES):
        y = acc[:, c * LANES:(c + 1) * LANES]
        if c % 2 == 1:
            y = _rope(y, cos_ref[...], sa_ref[...], sb_ref[...], MLA_ROPE_DIM // 4)
        o_ref[:, c * LANES:(c + 1) * LANES] = (y * scale).astype(o_ref.dtype)


def _epi_mla_k(acc, n, extras, outs):
    (kpe_ref,), (o_ref,) = extras, outs
    kpe = kpe_ref[...]
    for c in range(acc.shape[1] // LANES):
        o_ref[:, (2 * c) * LANES:(2 * c + 1) * LANES] = acc[:, c * LANES:(c + 1) * LANES].astype(o_ref.dtype)
        o_ref[:, (2 * c + 1) * LANES:(2 * c + 2) * LANES] = kpe


def _epi_gqa(acc, n, extras, outs, *, n_qk_tiles):
    gain_ref, cos_ref, sa_ref, sb_ref = extras
    (o_ref,) = outs

    @pl.when(n < n_qk_tiles)
    def _():
        for c in range(acc.shape[1] // LANES):
            y = _rms(acc[:, c * LANES:(c + 1) * LANES], gain_ref[:, c * LANES:(c + 1) * LANES])
            y = _rope(y, cos_ref[...], sa_ref[...], sb_ref[...], GQA_HEAD_DIM // 4)
            o_ref[:, c * LANES:(c + 1) * LANES] = y.astype(o_ref.dtype)

    @pl.when(n >= n_qk_tiles)
    def _():
        o_ref[...] = acc.astype(o_ref.dtype)


def _mm_res_kernel(a_ref, w_ref, x_ref, g_ref, o_ref):
    acc = jnp.dot(a_ref[...], w_ref[...], preferred_element_type=F32)
    o_ref[...] = x_ref[...] + g_ref[0] * acc


def _matmul_residual(a, w, x, mod, gate_chunk, bn, n_row_tiles, mod_row, name):
    bm = ROW_TILE
    kdim, n_out = w.shape
    assert n_out == x.shape[1] and n_out % bn == 0
    per_chunk = n_out // bn
    return pl.pallas_call(
        _mm_res_kernel,
        grid=(n_row_tiles, per_chunk),
        in_specs=[pl.BlockSpec((bm, kdim), lambda m, n: (m, 0)),
                  pl.BlockSpec((kdim, bn), lambda m, n: (0, n)),
                  pl.BlockSpec((bm, bn), lambda m, n: (m, n)),
                  pl.BlockSpec((1, 1, bn), lambda m, n: (mod_row(m), 0, gate_chunk * per_chunk + n))],
        out_specs=pl.BlockSpec((bm, bn), lambda m, n: (m, n)),
        out_shape=jax.ShapeDtypeStruct(x.shape, x.dtype),
        input_output_aliases={2: 0},
        compiler_params=_params("arbitrary", "arbitrary"),
        name=name,
    )(a, w, x, mod)


def _ffn_up_kernel(x_ref, xp_ref, xn_ref, g_ref, sc_ref, sh_ref, wg_ref, wv_ref, cw_ref, cb_ref, o_ref, h_sc,
                   *, lat_rows, lat_seq, ctx_seq):
    bm = x_ref.shape[0]
    m = pl.program_id(0)

    @pl.when(pl.program_id(1) == 0)
    def _():
        def nm(x):
            return (_rms(x, g_ref[...]) * (1.0 + sc_ref[0]) + sh_ref[0]).astype(BF16)
        h_sc[0:bm, :] = nm(x_ref[...])
        h_sc[bm:bm + 2 * HALO, :] = nm(jnp.concatenate([xn_ref[...], xp_ref[...]], axis=0))

    gate = jnp.dot(h_sc[...], wg_ref[...], preferred_element_type=F32)
    val = jnp.dot(h_sc[0:bm, :], wv_ref[...], preferred_element_type=F32)
    ext = bm + 2 * HALO
    g_prev = pltpu.roll(gate, 1, 0)[0:bm]
    g_next = pltpu.roll(gate, ext - 1, 0)[0:bm]
    g_cur = gate[0:bm]
    row = m * bm + lax.broadcasted_iota(jnp.int32, (bm, 1), 0)
    in_ctx = row >= lat_rows
    pos = jnp.where(in_ctx, (row - lat_rows) & (ctx_seq - 1), row & (lat_seq - 1))
    seq = jnp.where(in_ctx, ctx_seq, lat_seq)
    g_prev = jnp.where(pos == 0, 0.0, g_prev)
    g_next = jnp.where(pos == seq - 1, 0.0, g_next)
    cw = cw_ref[...]
    z = cb_ref[...] + g_prev * cw[0:1] + g_cur * cw[1:2] + g_next * cw[2:3]
    o_ref[...] = (jax.nn.silu(z) * val).astype(o_ref.dtype)


def _ffn_up(x, g, mod, w_up, conv_w, conv_b, bn, n_row_tiles, mod_row, lat_rows, lat_seq, ctx_seq, name):
    bm = ROW_TILE
    rows, d = x.shape
    f = w_up.shape[1] // 2
    n_col = f // bn
    last_halo = rows // HALO - 1
    per = bm // HALO
    kern = functools.partial(_ffn_up_kernel, lat_rows=lat_rows, lat_seq=lat_seq, ctx_seq=ctx_seq)
    return pl.pallas_call(
        kern,
        grid=(n_row_tiles, n_col),
        in_specs=[pl.BlockSpec((bm, d), lambda m, n: (m, 0)),
                  pl.BlockSpec((HALO, d), lambda m, n: (jnp.maximum(m * per - 1, 0), 0)),
                  pl.BlockSpec((HALO, d), lambda m, n: (jnp.minimum((m + 1) * per, last_halo), 0)),
                  pl.BlockSpec((1, d), lambda m, n: (0, 0)),
                  pl.BlockSpec((1, 1, d), lambda m, n: (mod_row(m), 0, 4)),
                  pl.BlockSpec((1, 1, d), lambda m, n: (mod_row(m), 0, 3)),
                  pl.BlockSpec((d, bn), lambda m, n: (0, n)),
                  pl.BlockSpec((d, bn), lambda m, n: (0, n_col + n)),
                  pl.BlockSpec((CONV_WIDTH, bn), lambda m, n: (0, n)),
                  pl.BlockSpec((1, bn), lambda m, n: (0, n))],
        out_specs=pl.BlockSpec((bm, bn), lambda m, n: (m, n)),
        out_shape=jax.ShapeDtypeStruct((rows, f), BF16),
        scratch_shapes=[pltpu.VMEM((bm + 2 * HALO, d), BF16)],
        compiler_params=_params("arbitrary", "arbitrary"),
        name=name,
    )(x, x, x, g.reshape(1, d), mod, mod, w_up, w_up, conv_w, conv_b.reshape(1, f))


def _flash_kernel(*refs, rep, dk, dv, bk, n_kv, has_ctx):
    if has_ctx:
        q_ref, k_ref, v_ref, kc_ref, vc_ref, o_ref, m_sc, l_sc, acc_sc = refs
    else:
        q_ref, k_ref, v_ref, o_ref, m_sc, l_sc, acc_sc = refs
    bq = q_ref.shape[0]
    if rep > 1:
        q = jnp.concatenate([q_ref[:, r * dk:(r + 1) * dk] for r in range(rep)], axis=0)
    else:
        q = q_ref[...]

    def scores(k):
        return lax.dot_general(q, k, (((1,), (1,)), ((), ())), preferred_element_type=F32)

    def first(k, v):
        s = scores(k)
        m = jnp.max(s, axis=-1, keepdims=True)
        p = jnp.exp(s - m)
        m_sc[...] = m
        l_sc[...] = jnp.sum(p, axis=-1, keepdims=True)
        acc_sc[...] = jnp.dot(p.astype(v.dtype), v, preferred_element_type=F32)

    def update(k, v):
        s = scores(k)
        m_prev = m_sc[...]
        m_new = jnp.maximum(m_prev, jnp.max(s, axis=-1, keepdims=True))
        alpha = jnp.exp(m_prev - m_new)
        p = jnp.exp(s - m_new)
        l_sc[...] = alpha * l_sc[...] + jnp.sum(p, axis=-1, keepdims=True)
        acc_sc[...] = alpha * acc_sc[...] + jnp.dot(p.astype(v.dtype), v, preferred_element_type=F32)
        m_sc[...] = m_new

    if has_ctx:
        first(kc_ref[...], vc_ref[...])
        start = 0
    else:
        first(k_ref[0:bk, :], v_ref[0:bk, :])
        start = 1

    def body(j, carry):
        off = pl.multiple_of(j * bk, bk)
        update(k_ref[pl.ds(off, bk), :], v_ref[pl.ds(off, bk), :])
        return carry

    lax.fori_loop(start, n_kv, body, 0)
    o = acc_sc[...] / l_sc[...]
    for r in range(rep):
        o_ref[:, r * dv:(r + 1) * dv] = o[r * bq:(r + 1) * bq].astype(o_ref.dtype)


def _attention(q_arr, k_arr, v_arr, o_prev, *, batch, groups, rep, dk, dv, qcol0, kcol0, vcol0,
               q_blk0, n_q, bq, kv_blk0, kv_len, bk, ctx_blk0, ctx_len, out_cols, name):
    has_ctx = ctx_blk0 is not None
    rows = q_arr.shape[0]
    in_specs = [pl.BlockSpec((bq, rep * dk), lambda b, g, i: (q_blk0 + b * n_q + i, qcol0 + g)),
                pl.BlockSpec((kv_len, dk), lambda b, g, i: (kv_blk0 + b, kcol0 + g)),
                pl.BlockSpec((kv_len, dv), lambda b, g, i: (kv_blk0 + b, vcol0 + g))]
    args = [q_arr, k_arr, v_arr]
    if has_ctx:
        in_specs += [pl.BlockSpec((ctx_len, dk), lambda b, g, i: (ctx_blk0 + b, kcol0 + g)),
                     pl.BlockSpec((ctx_len, dv), lambda b, g, i: (ctx_blk0 + b, vcol0 + g))]
        args += [k_arr, v_arr]
    aliases = {}
    if o_prev is not None:
        in_specs.append(pl.BlockSpec(memory_space=pl.ANY))
        args.append(o_prev)
        aliases = {len(args) - 1: 0}
    kern = functools.partial(_flash_kernel, rep=rep, dk=dk, dv=dv, bk=bk, n_kv=kv_len // bk, has_ctx=has_ctx)
    if o_prev is not None:
        inner = kern

        def kern(*refs):
            n_in = 5 if has_ctx else 3
            return inner(*refs[:n_in], *refs[n_in + 1:])
    return pl.pallas_call(
        kern,
        grid=(batch, groups, n_q),
        in_specs=in_specs,
        out_specs=pl.BlockSpec((bq, rep * dv), lambda b, g, i: (q_blk0 + b * n_q + i, g)),
        out_shape=jax.ShapeDtypeStruct((rows, out_cols), BF16),
        scratch_shapes=[pltpu.VMEM((rep * bq, 1), F32), pltpu.VMEM((rep * bq, 1), F32),
                        pltpu.VMEM((rep * bq, dv), F32)],
        input_output_aliases=aliases,
        compiler_params=_params("arbitrary", "arbitrary", "arbitrary"),
        name=name,
    )(*args)


def _rpb_table_kernel(rpb_ref, o_ref, *, n_tiles, grid_rows):
    var = pl.program_id(0)
    h = pl.program_id(1)
    t_rep = jnp.where(var == 0, 0, jnp.where(var == 1, 1, n_tiles - 1))
    qc = lax.broadcasted_iota(jnp.int32, (GRID_W, GRID_W), 0)
    kc = lax.broadcasted_iota(jnp.int32, (GRID_W, GRID_W), 1)
    dc = kc - qc
    cstart = jnp.clip(qc - NA_WIN_COLS // 2, 0, GRID_W - NA_WIN_COLS)
    valid_c = (kc >= cstart) & (kc < cstart + NA_WIN_COLS)
    n_dc = 2 * NA_WIN_COLS - 1
    masked = jnp.full((GRID_W, GRID_W), MASK_VALUE, F32)
    toeplitz = {}
    for dr in range(-(NA_WIN_ROWS - 1), NA_WIN_ROWS):
        acc = jnp.zeros((GRID_W, GRID_W), F32)
        for b in range(n_dc):
            acc = jnp.where(dc == b - (NA_WIN_COLS - 1), rpb_ref[h, (dr + NA_WIN_ROWS - 1) * n_dc + b], acc)
        toeplitz[dr] = jnp.where(valid_c, acc, MASK_VALUE)
    for ql in range(NA_TILE_ROWS):
        qr = NA_TILE_ROWS * t_rep + ql
        rstart = jnp.clip(qr - NA_WIN_ROWS // 2, 0, grid_rows - NA_WIN_ROWS)
        for kl in range(3 * NA_TILE_ROWS):
            kr = NA_TILE_ROWS * (t_rep - 1) + kl
            dr = kl - NA_TILE_ROWS - ql
            if abs(dr) > NA_WIN_ROWS - 1:
                blk = masked
            else:
                ok = ((kr >= rstart) & (kr < rstart + NA_WIN_ROWS)).astype(F32)
                blk = toeplitz[dr] * ok + MASK_VALUE * (1.0 - ok)
            o_ref[0, 0, ql * GRID_W:(ql + 1) * GRID_W, kl * GRID_W:(kl + 1) * GRID_W] = blk


def _rpb_table(rpb, n_tiles, grid_rows):
    heads = rpb.shape[0]
    tq = NA_TILE_ROWS * GRID_W
    kern = functools.partial(_rpb_table_kernel, n_tiles=n_tiles, grid_rows=grid_rows)
    return pl.pallas_call(
        kern,
        grid=(3, heads),
        in_specs=[pl.BlockSpec(memory_space=pltpu.SMEM)],
        out_specs=pl.BlockSpec((1, 1, tq, 3 * tq), lambda v, h: (v, h, 0, 0)),
        out_shape=jax.ShapeDtypeStruct((3, heads, tq, 3 * tq), F32),
        compiler_params=_params("arbitrary", "arbitrary"),
        name="na_rpb_table",
    )(rpb.reshape(heads, -1))


def _na_kernel(q_ref, kp_ref, kc_ref, kn_ref, kx_ref, vp_ref, vc_ref, vn_ref, vx_ref, tab_ref, o_ref, *, heads, dh):
    tq = q_ref.shape[0]
    for h in range(heads):
        sl = slice(h * dh, (h + 1) * dh)
        q = q_ref[:, sl]

        def sc(k_ref):
            return lax.dot_general(q, k_ref[:, sl], (((1,), (1,)), ((), ())), preferred_element_type=F32)

        s = [sc(kp_ref) + tab_ref[0, h, :, 0:tq],
             sc(kc_ref) + tab_ref[0, h, :, tq:2 * tq],
             sc(kn_ref) + tab_ref[0, h, :, 2 * tq:3 * tq],
             sc(kx_ref)]
        m = jnp.max(s[0], axis=-1, keepdims=True)
        for sj in s[1:]:
            m = jnp.maximum(m, jnp.max(sj, axis=-1, keepdims=True))
        p = [jnp.exp(sj - m) for sj in s]
        l = p[0].sum(axis=-1, keepdims=True)
        for pj in p[1:]:
            l = l + pj.sum(axis=-1, keepdims=True)
        o = jnp.dot(p[0].astype(BF16), vp_ref[:, sl], preferred_element_type=F32)
        for pj, v_ref in zip(p[1:], (vc_ref, vn_ref, vx_ref)):
            o = o + jnp.dot(pj.astype(BF16), v_ref[:, sl], preferred_element_type=F32)
        o_ref[:, sl] = (o / l).astype(o_ref.dtype)


def _neighborhood_attention(qkv, table, *, batch, seq, ctx_len, heads, dh, hb):
    rows = qkv.shape[0]
    tq = NA_TILE_ROWS * GRID_W
    assert ctx_len == tq and seq % tq == 0
    n_tiles = seq // tq
    n_hg = heads // hb
    ctx_blk0 = batch * seq // ctx_len
    wb = hb * dh

    def var(t):
        return jnp.where(t == 0, 0, jnp.where(t == n_tiles - 1, 2, 1))

    def spec(col0, shift):
        if shift is None:
            return pl.BlockSpec((tq, wb), lambda b, g, t: (ctx_blk0 + b, col0 + g))
        return pl.BlockSpec((tq, wb), lambda b, g, t: (b * n_tiles + jnp.clip(t + shift, 0, n_tiles - 1), col0 + g))

    kern = functools.partial(_na_kernel, heads=hb, dh=dh)
    return pl.pallas_call(
        kern,
        grid=(batch, n_hg, n_tiles),
        in_specs=[spec(0, 0),
                  spec(n_hg, -1), spec(n_hg, 0), spec(n_hg, 1), spec(n_hg, None),
                  spec(2 * n_hg, -1), spec(2 * n_hg, 0), spec(2 * n_hg, 1), spec(2 * n_hg, None),
                  pl.BlockSpec((1, hb, tq, 3 * tq), lambda b, g, t: (var(t), g, 0, 0))],
        out_specs=pl.BlockSpec((tq, wb), lambda b, g, t: (b * n_tiles + t, g)),
        out_shape=jax.ShapeDtypeStruct((rows, heads * dh), BF16),
        compiler_params=_params("arbitrary", "arbitrary", "arbitrary"),
        name="na_attention",
    )(qkv, qkv, qkv, qkv, qkv, qkv, qkv, qkv, qkv, table)


def _final_norm_kernel(x_ref, g_ref, o_ref):
    o_ref[...] = _rms(x_ref[...], g_ref[...])


def _final_norm(x, g, n_row_tiles):
    bm = ROW_TILE
    d = x.shape[1]
    return pl.pallas_call(
        _final_norm_kernel,
        grid=(n_row_tiles,),
        in_specs=[pl.BlockSpec((bm, d), lambda m: (m, 0)), pl.BlockSpec((1, d), lambda m: (0, 0))],
        out_specs=pl.BlockSpec((bm, d), lambda m: (m, 0)),
        out_shape=jax.ShapeDtypeStruct((n_row_tiles * bm, d), F32),
        compiler_params=_params("arbitrary"),
        name="final_norm",
    )(x, g.reshape(1, d))


def _rope_tables(n_tok, rot_dim, batch, n_ctx_rows):
    quarter = rot_dim // 4
    t = jnp.arange(n_tok)
    row = (t // GRID_W).astype(F32)
    col = (t % GRID_W).astype(F32)
    inv = ROPE_THETA ** (-jnp.arange(quarter, dtype=F32) / quarter)
    ar = row[:, None] * inv
    ac = col[:, None] * inv
    ang = jnp.concatenate([ar, ar, ac, ac], axis=-1)
    cos, sin = jnp.cos(ang), jnp.sin(ang)
    first = (jnp.arange(rot_dim) % (2 * quarter)) < quarter
    sin_a = jnp.where(first, -sin, 0.0)
    sin_b = jnp.where(first, 0.0, sin)
    pad = LANES - rot_dim

    def full(tab, fill):
        tab = jnp.pad(tab, ((0, 0), (0, pad)), constant_values=fill)
        tab = jnp.tile(tab, (batch, 1))
        return jnp.concatenate([tab, jnp.full((n_ctx_rows, LANES), fill, F32)], axis=0)

    return full(cos, 1.0), full(sin_a, 0.0), full(sin_b, 0.0)


def kernel(x, c, ctx, c_ctx, ada_w, ada_b, mix_norm, ffn_norm, ffn_up, ffn_conv_w, ffn_conv_b, ffn_down, na_wqkv, na_rpb, na_wo, mla_wq_a, mla_q_norm, mla_wq_b, mla_wkv_a, mla_kv_norm, mla_wkv_b, mla_wo, gqa_wqkv, gqa_q_norm, gqa_k_norm, gqa_wo, final_norm):
    batch, seq, d = x.shape
    ctx_len = ctx.shape[1]
    depth = ada_w.shape[0]
    ffn_dim = ffn_down.shape[1]
    bm = ROW_TILE
    lat_rows = batch * seq
    ctx_rows = batch * ctx_len
    rows = lat_rows + ctx_rows
    assert seq % bm == 0 and ctx_rows % bm == 0 and batch < 8
    assert seq & (seq - 1) == 0 and ctx_len & (ctx_len - 1) == 0
    tiles_per_batch = seq // bm
    lat_tiles = lat_rows // bm
    all_tiles = rows // bm
    grid_rows = seq // GRID_W

    def mod_row(m):
        return jnp.minimum(m // tiles_per_batch, batch)

    xs = jnp.concatenate([x.reshape(lat_rows, d), ctx.reshape(ctx_rows, d)], axis=0)
    c_rows = jnp.zeros((8, d), F32).at[:batch].set(c).at[batch].set(c_ctx)
    mod_all = _modulation(c_rows, ada_w, ada_b)

    cos_b, sa_b, sb_b = _rope_tables(seq, MLA_ROPE_DIM, batch, ctx_rows)
    cos_c, sa_c, sb_c = _rope_tables(seq, GQA_HEAD_DIM, batch, ctx_rows)

    def row_spec(width):
        return pl.BlockSpec((bm, width), lambda m, n: (m, 0))

    rope_specs = [row_spec(LANES)] * 3

    for i in range(depth):
        last = i == depth - 1
        kind, j = i % N_MIXERS, i // N_MIXERS
        mod = mod_all[i].reshape(8, 1, 6 * d)
        n_tiles = lat_tiles if last else all_tiles

        if kind == 0:
            scale = (d // NA_HEADS) ** -0.5
            colscale = jnp.concatenate([jnp.full((1, d), scale, F32), jnp.ones((1, 2 * d), F32)], axis=1)
            bn = d
            qkv = _norm_matmul(
                xs, 0, d, mix_norm[i], mod, (1, 0), na_wqkv[j].astype(BF16), bn,
                [colscale], [pl.BlockSpec((1, bn), lambda m, n: (0, n))],
                jax.ShapeDtypeStruct((rows, 3 * d), BF16), pl.BlockSpec((bm, bn), lambda m, n: (m, n)),
                _epi_colscale, all_tiles, mod_row, "na_qkv")
            table = _rpb_table(na_rpb[j], seq // (NA_TILE_ROWS * GRID_W), grid_rows)
            dh = d // NA_HEADS
            o = _neighborhood_attention(qkv, table, batch=batch, seq=seq, ctx_len=ctx_len,
                                        heads=NA_HEADS, dh=dh, hb=8)
            if not last:
                o = _attention(qkv, qkv, qkv, o, batch=batch, groups=NA_HEADS, rep=1, dk=dh, dv=dh,
                               qcol0=0, kcol0=NA_HEADS, vcol0=2 * NA_HEADS,
                               q_blk0=lat_rows // ctx_len, n_q=1, bq=ctx_len,
                               kv_blk0=lat_rows // ctx_len, kv_len=ctx_len, bk=ctx_len,
                               ctx_blk0=None, ctx_len=None, out_cols=d, name="na_ctx_attention")
            wo = na_wo[j]
        elif kind == 1:
            scale = (MLA_NOPE_DIM + MLA_ROPE_DIM) ** -0.5
            q_rank = mla_wq_a.shape[2]
            kv_rank = mla_kv_norm.shape[1]
            hq = 2 * LANES
            w1 = jnp.concatenate([mla_wq_a[j], mla_wkv_a[j], jnp.zeros((d, LANES - MLA_ROPE_DIM), F32)], axis=1)
            n1 = w1.shape[1]
            qc, kpe = _norm_matmul(
                xs, 0, d, mix_norm[i], mod, (1, 0), w1.astype(BF16), n1,
                [cos_b, sa_b, sb_b], rope_specs,
                (jax.ShapeDtypeStruct((rows, q_rank + kv_rank), F32), jax.ShapeDtypeStruct((rows, LANES), BF16)),
                (pl.BlockSpec((bm, q_rank + kv_rank), lambda m, n: (m, 0)), pl.BlockSpec((bm, LANES), lambda m, n: (m, 0))),
                _epi_mla_a, all_tiles, mod_row, "mla_a")
            wqb = mla_wq_b[j].reshape(q_rank, MLA_HEADS, MLA_NOPE_DIM + MLA_ROPE_DIM)
            wqb = jnp.pad(wqb, ((0, 0), (0, 0), (0, hq - MLA_NOPE_DIM - MLA_ROPE_DIM))).reshape(q_rank, MLA_HEADS * hq)
            bn = 4 * hq
            q = _norm_matmul(
                qc, 0, q_rank, mla_q_norm[j], None, None, wqb.astype(BF16), bn,
                [cos_b, sa_b, sb_b], rope_specs,
                jax.ShapeDtypeStruct((rows, MLA_HEADS * hq), BF16), pl.BlockSpec((bm, bn), lambda m, n: (m, n)),
                functools.partial(_epi_mla_q, scale=scale), all_tiles, mod_row, "mla_q")
            wkvb = mla_wkv_b[j].reshape(kv_rank, MLA_HEADS, MLA_NOPE_DIM + MLA_V_DIM)
            wk = wkvb[:, :, :MLA_NOPE_DIM].reshape(kv_rank, MLA_HEADS * MLA_NOPE_DIM)
            wv = wkvb[:, :, MLA_NOPE_DIM:].reshape(kv_rank, MLA_HEADS * MLA_V_DIM)
            bnk = 8 * MLA_NOPE_DIM
            k = _norm_matmul(
                qc, 1, kv_rank, mla_kv_norm[j], None, None, wk.astype(BF16), bnk,
                [kpe], [row_spec(LANES)],
                jax.ShapeDtypeStruct((rows, MLA_HEADS * hq), BF16), pl.BlockSpec((bm, 2 * bnk), lambda m, n: (m, n)),
                _epi_mla_k, all_tiles, mod_row, "mla_k")
            v = _norm_matmul(
                qc, 1, kv_rank, mla_kv_norm[j], None, None, wv.astype(BF16), bnk,
                [], [],
                jax.ShapeDtypeStruct((rows, MLA_HEADS * MLA_V_DIM), BF16), pl.BlockSpec((bm, bnk), lambda m, n: (m, n)),
                _epi_plain, all_tiles, mod_row, "mla_v")
            out_cols = MLA_HEADS * MLA_V_DIM
            bq = min(1024, seq)
            o = _attention(q, k, v, None, batch=batch, groups=MLA_HEADS, rep=1, dk=hq, dv=MLA_V_DIM,
                           qcol0=0, kcol0=0, vcol0=0, q_blk0=0, n_q=seq // bq, bq=bq,
                           kv_blk0=0, kv_len=seq, bk=512, ctx_blk0=lat_rows // ctx_len, ctx_len=ctx_len,
                           out_cols=out_cols, name="mla_attention")
            if not last:
                o = _attention(q, k, v, o, batch=batch, groups=MLA_HEADS, rep=1, dk=hq, dv=MLA_V_DIM,
                               qcol0=0, kcol0=0, vcol0=0, q_blk0=lat_rows // ctx_len, n_q=1, bq=ctx_len,
                               kv_blk0=lat_rows // ctx_len, kv_len=ctx_len, bk=ctx_len,
                               ctx_blk0=None, ctx_len=None, out_cols=out_cols, name="mla_ctx_attention")
            wo = mla_wo[j]
        else:
            scale = GQA_HEAD_DIM ** -0.5
            rep = GQA_HEADS // GQA_KV_HEADS
            nq_cols = GQA_HEADS * GQA_HEAD_DIM
            nk_cols = GQA_KV_HEADS * GQA_HEAD_DIM
            bn = nk_cols
            gain = jnp.concatenate([jnp.tile(gqa_q_norm[j], GQA_HEADS) * scale, jnp.tile(gqa_k_norm[j], GQA_KV_HEADS),
                                    jnp.ones((nk_cols,), F32)]).reshape(1, -1)
            qkv = _norm_matmul(
                xs, 0, d, mix_norm[i], mod, (1, 0), gqa_wqkv[j].astype(BF16), bn,
                [gain, cos_c, sa_c, sb_c], [pl.BlockSpec((1, bn), lambda m, n: (0, n))] + rope_specs,
                jax.ShapeDtypeStruct((rows, nq_cols + 2 * nk_cols), BF16), pl.BlockSpec((bm, bn), lambda m, n: (m, n)),
                functools.partial(_epi_gqa, n_qk_tiles=(nq_cols + nk_cols) // bn), all_tiles, mod_row, "gqa_qkv")
            dh = GQA_HEAD_DIM
            bq = min(256, seq)
            o = _attention(qkv, qkv, qkv, None, batch=batch, groups=GQA_KV_HEADS, rep=rep, dk=dh, dv=dh,
                           qcol0=0, kcol0=GQA_HEADS, vcol0=GQA_HEADS + GQA_KV_HEADS, q_blk0=0, n_q=seq // bq, bq=bq,
                           kv_blk0=0, kv_len=seq, bk=512, ctx_blk0=lat_rows // ctx_len, ctx_len=ctx_len,
                           out_cols=nq_cols, name="gqa_attention")
            if not last:
                o = _attention(qkv, qkv, qkv, o, batch=batch, groups=GQA_KV_HEADS, rep=rep, dk=dh, dv=dh,
                               qcol0=0, kcol0=GQA_HEADS, vcol0=GQA_HEADS + GQA_KV_HEADS,
                               q_blk0=lat_rows // ctx_len, n_q=1, bq=ctx_len,
                               kv_blk0=lat_rows // ctx_len, kv_len=ctx_len, bk=ctx_len,
                               ctx_blk0=None, ctx_len=None, out_cols=nq_cols, name="gqa_ctx_attention")
            wo = gqa_wo[j]

        xs = _matmul_residual(o, wo.astype(BF16), xs, mod, 2, d // 2, n_tiles, mod_row, "attn_out")
        a = _ffn_up(xs, ffn_norm[i], mod, ffn_up[i].astype(BF16), ffn_conv_w[i], ffn_conv_b[i], 512,
                    n_tiles, mod_row, lat_rows, seq, ctx_len, "ffn_up")
        xs = _matmul_residual(a, ffn_down[i].astype(BF16), xs, mod, 5, d // 4, n_tiles, mod_row, "ffn_down")

    out = _final_norm(xs, final_norm, lat_tiles)
    return out.reshape(batch, seq, d)
```

```python
import functools

import jax
import jax.numpy as jnp
from jax import lax
from jax.experimental import pallas as pl
from jax.experimental.pallas import tpu as pltpu

F32 = jnp.float32
BF16 = jnp.bfloat16

GRID_W = 64
NA_HEADS = 16
NA_WIN_ROWS = 8
NA_WIN_COLS = 16
MLA_HEADS = 16
MLA_NOPE_DIM = 128
MLA_ROPE_DIM = 64
MLA_V_DIM = 128
GQA_HEADS = 16
GQA_KV_HEADS = 4
GQA_HEAD_DIM = 128
CONV_WIDTH = 3
ROPE_THETA = 10000.0
NORM_EPS = 1e-6
N_MIXERS = 3
LOG2_E = 1.4426950408889634

LANES = 128
ROW_TILE = 512
HALO = 8
NA_TILE_ROWS = 4
MASK_VALUE = -1e30
VMEM_LIMIT = 56 * 1024 * 1024


def _params(*sem):
    return pltpu.CompilerParams(dimension_semantics=sem, vmem_limit_bytes=VMEM_LIMIT)


def _rms(x, g):
    y = x * lax.rsqrt(jnp.mean(x * x, axis=-1, keepdims=True) + NORM_EPS)
    return y * g


def _rope(y, cos, sin_a, sin_b, quarter):
    return (y * cos + pltpu.roll(y, LANES - quarter, 1) * sin_a
            + pltpu.roll(y, quarter, 1) * sin_b)


def _mod_kernel(c_ref, w_ref, b_ref, o_ref):
    s = jax.nn.silu(c_ref[...])
    o_ref[0] = jnp.dot(s.astype(BF16), w_ref[0].astype(BF16), preferred_element_type=F32) + b_ref[0]


def _modulation(c_rows, ada_w, ada_b):
    depth, d, n = ada_w.shape
    bn = n // 8
    return pl.pallas_call(
        _mod_kernel,
        grid=(depth, n // bn),
        in_specs=[pl.BlockSpec((8, d), lambda l, j: (0, 0)),
                  pl.BlockSpec((1, d, bn), lambda l, j: (l, 0, j)),
                  pl.BlockSpec((1, 1, bn), lambda l, j: (l, 0, j))],
        out_specs=pl.BlockSpec((1, 8, bn), lambda l, j: (l, 0, j)),
        out_shape=jax.ShapeDtypeStruct((depth, 8, n), F32),
        compiler_params=_params("arbitrary", "arbitrary"),
        name="adaln_mod",
    )(c_rows, ada_w, ada_b.reshape(depth, 1, n))


def _norm_matmul_kernel(*refs, has_mod, n_extra, epilogue):
    if has_mod:
        x_ref, g_ref, sc_ref, sh_ref, w_ref = refs[:5]
        rest = refs[5:]
    else:
        x_ref, g_ref, w_ref = refs[:3]
        rest = refs[3:]
    extras, outs, h_sc = rest[:n_extra], rest[n_extra:-1], rest[-1]
    n = pl.program_id(1)

    @pl.when(n == 0)
    def _():
        h = _rms(x_ref[...], g_ref[...])
        if has_mod:
            h = h * (1.0 + sc_ref[0]) + sh_ref[0]
        h_sc[...] = h.astype(BF16)

    acc = jnp.dot(h_sc[...], w_ref[...], preferred_element_type=F32)
    epilogue(acc, n, extras, outs)


def _norm_matmul(x, xcol, kdim, g, mod, mod_chunks, w, bn, extras, extra_specs, out_shapes, out_specs,
                 epilogue, n_row_tiles, mod_row, name):
    bm = ROW_TILE
    n_col = w.shape[1] // bn
    has_mod = mod is not None
    in_specs = [pl.BlockSpec((bm, kdim), lambda m, n: (m, xcol)),
                pl.BlockSpec((1, kdim), lambda m, n: (0, 0))]
    args = [x, g.reshape(1, kdim)]
    if has_mod:
        sc_chunk, sh_chunk = mod_chunks
        in_specs += [pl.BlockSpec((1, 1, kdim), lambda m, n: (mod_row(m), 0, sc_chunk)),
                     pl.BlockSpec((1, 1, kdim), lambda m, n: (mod_row(m), 0, sh_chunk))]
        args += [mod, mod]
    in_specs.append(pl.BlockSpec((kdim, bn), lambda m, n: (0, n)))
    args.append(w)
    in_specs += extra_specs
    args += extras
    kern = functools.partial(_norm_matmul_kernel, has_mod=has_mod, n_extra=len(extras), epilogue=epilogue)
    return pl.pallas_call(
        kern,
        grid=(n_row_tiles, n_col),
        in_specs=in_specs,
        out_specs=out_specs,
        out_shape=out_shapes,
        scratch_shapes=[pltpu.VMEM((bm, kdim), BF16)],
        compiler_params=_params("arbitrary", "arbitrary"),
        name=name,
    )(*args)


def _epi_colscale(acc, n, extras, outs):
    (cs_ref,), (o_ref,) = extras, outs
    o_ref[...] = (acc * cs_ref[...]).astype(o_ref.dtype)


def _epi_plain(acc, n, extras, outs):
    (o_ref,) = outs
    o_ref[...] = acc.astype(o_ref.dtype)


def _epi_mla_a(acc, n, extras, outs):
    cos_ref, sa_ref, sb_ref = extras
    qc_ref, kpe_ref = outs
    wide = qc_ref.shape[1]
    qc_ref[...] = acc[:, :wide]
    kpe = _rope(acc[:, wide:], cos_ref[...], sa_ref[...], sb_ref[...], MLA_ROPE_DIM // 4)
    kpe_ref[...] = kpe.astype(kpe_ref.dtype)


def _epi_mla_q(acc, n, extras, outs, *, scale):
    cos_ref, sa_ref, sb_ref = extras
    (o_ref,) = outs
    for c in range(acc.shape[1] // LANES):
        y = acc[:, c * LANES:(c + 1) * LANES]
        if c % 2 == 1:
            y = _rope(y, cos_ref[...], sa_ref[...], sb_ref[...], MLA_ROPE_DIM // 4)
        o_ref[:, c * LANES:(c + 1) * LANES] = (y * scale).astype(o_ref.dtype)


def _epi_mla_k(acc, n, extras, outs):
    (kpe_ref,), (o_ref,) = extras, outs
    kpe = kpe_ref[...]
    for c in range(acc.shape[1] // LANES):
        o_ref[:, (2 * c) * LANES:(2 * c + 1) * LANES] = acc[:, c * LANES:(c + 1) * LANES].astype(o_ref.dtype)
        o_ref[:, (2 * c + 1) * LANES:(2 * c + 2) * LANES] = kpe


def _epi_gqa(acc, n, extras, outs, *, n_qk_tiles):
    gain_ref, cos_ref, sa_ref, sb_ref = extras
    (o_ref,) = outs

    @pl.when(n < n_qk_tiles)
    def _():
        for c in range(acc.shape[1] // LANES):
            y = _rms(acc[:, c * LANES:(c + 1) * LANES], gain_ref[:, c * LANES:(c + 1) * LANES])
            y = _rope(y, cos_ref[...], sa_ref[...], sb_ref[...], GQA_HEAD_DIM // 4)
            o_ref[:, c * LANES:(c + 1) * LANES] = y.astype(o_ref.dtype)

    @pl.when(n >= n_qk_tiles)
    def _():
        o_ref[...] = acc.astype(o_ref.dtype)


def _mm_res_kernel(a_ref, w_ref, x_ref, g_ref, o_ref):
    acc = jnp.dot(a_ref[...], w_ref[...], preferred_element_type=F32)
    o_ref[...] = x_ref[...] + g_ref[0] * acc


def _matmul_residual(a, w, x, mod, gate_chunk, bn, n_row_tiles, mod_row, name):
    bm = ROW_TILE
    kdim, n_out = w.shape
    assert n_out == x.shape[1] and n_out % bn == 0
    per_chunk = n_out // bn
    return pl.pallas_call(
        _mm_res_kernel,
        grid=(n_row_tiles, per_chunk),
        in_specs=[pl.BlockSpec((bm, kdim), lambda m, n: (m, 0)),
                  pl.BlockSpec((kdim, bn), lambda m, n: (0, n)),
                  pl.BlockSpec((bm, bn), lambda m, n: (m, n)),
                  pl.BlockSpec((1, 1, bn), lambda m, n: (mod_row(m), 0, gate_chunk * per_chunk + n))],
        out_specs=pl.BlockSpec((bm, bn), lambda m, n: (m, n)),
        out_shape=jax.ShapeDtypeStruct(x.shape, x.dtype),
        input_output_aliases={2: 0},
        compiler_params=_params("arbitrary", "arbitrary"),
        name=name,
    )(a, w, x, mod)


def _ffn_up_kernel(x_ref, xp_ref, xn_ref, g_ref, sc_ref, sh_ref, wg_ref, wv_ref, cw_ref, cb_ref, o_ref, h_sc,
                   *, lat_rows, lat_seq, ctx_seq):
    bm = x_ref.shape[0]
    m = pl.program_id(0)

    @pl.when(pl.program_id(1) == 0)
    def _():
        def nm(x):
            return (_rms(x, g_ref[...]) * (1.0 + sc_ref[0]) + sh_ref[0]).astype(BF16)
        h_sc[0:bm, :] = nm(x_ref[...])
        h_sc[bm:bm + 2 * HALO, :] = nm(jnp.concatenate([xn_ref[...], xp_ref[...]], axis=0))

    gate = jnp.dot(h_sc[...], wg_ref[...], preferred_element_type=F32)
    val = jnp.dot(h_sc[0:bm, :], wv_ref[...], preferred_element_type=F32)
    ext = bm + 2 * HALO
    g_prev = pltpu.roll(gate, 1, 0)[0:bm]
    g_next = pltpu.roll(gate, ext - 1, 0)[0:bm]
    g_cur = gate[0:bm]
    row = m * bm + lax.broadcasted_iota(jnp.int32, (bm, 1), 0)
    in_ctx = row >= lat_rows
    pos = jnp.where(in_ctx, (row - lat_rows) & (ctx_seq - 1), row & (lat_seq - 1))
    seq = jnp.where(in_ctx, ctx_seq, lat_seq)
    g_prev = jnp.where(pos == 0, 0.0, g_prev)
    g_next = jnp.where(pos == seq - 1, 0.0, g_next)
    cw = cw_ref[...]
    z = cb_ref[...] + g_prev * cw[0:1] + g_cur * cw[1:2] + g_next * cw[2:3]
    o_ref[...] = (jax.nn.silu(z) * val).astype(o_ref.dtype)


def _ffn_up(x, g, mod, w_up, conv_w, conv_b, bn, n_row_tiles, mod_row, lat_rows, lat_seq, ctx_seq, name):
    bm = ROW_TILE
    rows, d = x.shape
    f = w_up.shape[1] // 2
    n_col = f // bn
    last_halo = rows // HALO - 1
    per = bm // HALO
    kern = functools.partial(_ffn_up_kernel, lat_rows=lat_rows, lat_seq=lat_seq, ctx_seq=ctx_seq)
    return pl.pallas_call(
        kern,
        grid=(n_row_tiles, n_col),
        in_specs=[pl.BlockSpec((bm, d), lambda m, n: (m, 0)),
                  pl.BlockSpec((HALO, d), lambda m, n: (jnp.maximum(m * per - 1, 0), 0)),
                  pl.BlockSpec((HALO, d), lambda m, n: (jnp.minimum((m + 1) * per, last_halo), 0)),
                  pl.BlockSpec((1, d), lambda m, n: (0, 0)),
                  pl.BlockSpec((1, 1, d), lambda m, n: (mod_row(m), 0, 4)),
                  pl.BlockSpec((1, 1, d), lambda m, n: (mod_row(m), 0, 3)),
                  pl.BlockSpec((d, bn), lambda m, n: (0, n)),
                  pl.BlockSpec((d, bn), lambda m, n: (0, n_col + n)),
                  pl.BlockSpec((CONV_WIDTH, bn), lambda m, n: (0, n)),
                  pl.BlockSpec((1, bn), lambda m, n: (0, n))],
        out_specs=pl.BlockSpec((bm, bn), lambda m, n: (m, n)),
        out_shape=jax.ShapeDtypeStruct((rows, f), BF16),
        scratch_shapes=[pltpu.VMEM((bm + 2 * HALO, d), BF16)],
        compiler_params=_params("arbitrary", "arbitrary"),
        name=name,
    )(x, x, x, g.reshape(1, d), mod, mod, w_up, w_up, conv_w, conv_b.reshape(1, f))


def _flash_t_kernel(qt_ref, k_ref, vt_ref, kc_ref, vct_ref, o_ref, m_sc, l_sc, acc_sc, s_sc, *, rep, bk, n_kv, chunk):
    bq = qt_ref.shape[3]
    dv = vt_ref.shape[3]
    n_chunks = rep * bq // chunk

    def scores(k, c):
        r, off = divmod(c * chunk, bq)
        return jnp.dot(k, qt_ref[0, r, :, off:off + chunk], preferred_element_type=F32)

    def softmax_pv(s, vt, c, is_first):
        cs = slice(c * chunk, (c + 1) * chunk)
        mx = jnp.max(s, axis=0, keepdims=True)
        if is_first:
            m_new = mx
            p = jnp.exp2(s - m_new)
            l_sc[:, cs] = jnp.sum(p, axis=0, keepdims=True)
            acc_sc[:, cs] = jnp.dot(vt, p.astype(vt.dtype), preferred_element_type=F32)
        else:
            m_prev = m_sc[:, cs]
            m_new = jnp.maximum(m_prev, mx)
            alpha = jnp.exp2(m_prev - m_new)
            p = jnp.exp2(s - m_new)
            l_sc[:, cs] = alpha * l_sc[:, cs] + jnp.sum(p, axis=0, keepdims=True)
            acc_sc[:, cs] = alpha * acc_sc[:, cs] + jnp.dot(vt, p.astype(vt.dtype), preferred_element_type=F32)
        m_sc[:, cs] = m_new

    kc = kc_ref[...]
    k0 = k_ref[0:bk, :]
    for c in range(n_chunks):
        s_ctx = scores(kc, c)
        s_sc[0, c] = scores(k0, c)
        softmax_pv(s_ctx, vct_ref[0, 0], c, True)

    def block(j, cur):
        off = pl.multiple_of(jnp.minimum(j + 1, n_kv - 1) * bk, bk)
        k_next = k_ref[pl.ds(off, bk), :]
        vt = vt_ref[0, 0, j]
        for c in range(n_chunks):
            s_sc[1 - cur, c] = scores(k_next, c)
            softmax_pv(s_sc[cur, c], vt, c, False)

    def body(i, carry):
        block(2 * i, 0)
        block(2 * i + 1, 1)
        return carry

    assert n_kv % 2 == 0
    lax.fori_loop(0, n_kv // 2, body, 0)
    for r in range(rep):
        ot = acc_sc[:, r * bq:(r + 1) * bq] / l_sc[:, r * bq:(r + 1) * bq]
        o_ref[:, r * dv:(r + 1) * dv] = ot.T.astype(o_ref.dtype)


def _attention_t(q_arr, k_arr, v_arr, *, batch, seq, ctx_len, heads, groups, dk, dv, qcol0, kcol0, vcol0,
                 bq, bk, chunk, out_cols, name):
    rows = q_arr.shape[0]
    lat_rows = batch * seq
    rep = heads // groups
    n_q, n_kv = seq // bq, seq // bk
    ctx_blk0 = lat_rows // ctx_len
    qt = q_arr[:lat_rows, qcol0 * dk:(qcol0 + heads) * dk].reshape(batch, seq, heads, dk).transpose(0, 2, 3, 1)
    v_lat = v_arr[:lat_rows, vcol0 * dv:(vcol0 + groups) * dv]
    vt = v_lat.reshape(batch, n_kv, bk, groups, dv).transpose(0, 3, 1, 4, 2)
    v_ctx = v_arr[lat_rows:, vcol0 * dv:(vcol0 + groups) * dv]
    vct = v_ctx.reshape(batch, ctx_len, groups, dv).transpose(0, 2, 3, 1)
    kern = functools.partial(_flash_t_kernel, rep=rep, bk=bk, n_kv=n_kv, chunk=chunk)
    return pl.pallas_call(
        kern,
        grid=(batch, groups, n_q),
        in_specs=[pl.BlockSpec((1, rep, dk, bq), lambda b, g, i: (b, g, 0, i)),
                  pl.BlockSpec((seq, dk), lambda b, g, i: (b, kcol0 + g)),
                  pl.BlockSpec((1, 1, n_kv, dv, bk), lambda b, g, i: (b, g, 0, 0, 0)),
                  pl.BlockSpec((ctx_len, dk), lambda b, g, i: (ctx_blk0 + b, kcol0 + g)),
                  pl.BlockSpec((1, 1, dv, ctx_len), lambda b, g, i: (b, g, 0, 0))],
        out_specs=pl.BlockSpec((bq, rep * dv), lambda b, g, i: (b * n_q + i, g)),
        out_shape=jax.ShapeDtypeStruct((rows, out_cols), BF16),
        scratch_shapes=[pltpu.VMEM((1, rep * bq), F32), pltpu.VMEM((1, rep * bq), F32),
                        pltpu.VMEM((dv, rep * bq), F32), pltpu.VMEM((2, rep * bq // chunk, bk, chunk), F32)],
        compiler_params=_params("arbitrary", "arbitrary", "arbitrary"),
        name=name,
    )(qt, k_arr, vt, k_arr, vct)


def _flash_kernel(*refs, rep, dk, dv, bk, n_kv, has_ctx, log2_domain):
    if has_ctx:
        q_ref, k_ref, v_ref, kc_ref, vc_ref, o_ref, m_sc, l_sc, acc_sc = refs
    else:
        q_ref, k_ref, v_ref, o_ref, m_sc, l_sc, acc_sc = refs
    ex = jnp.exp2 if log2_domain else jnp.exp
    bq = q_ref.shape[0]
    if rep > 1:
        q = jnp.concatenate([q_ref[:, r * dk:(r + 1) * dk] for r in range(rep)], axis=0)
    else:
        q = q_ref[...]

    def scores(k):
        return lax.dot_general(q, k, (((1,), (1,)), ((), ())), preferred_element_type=F32)

    def first(k, v):
        s = scores(k)
        m = jnp.max(s, axis=-1, keepdims=True)
        p = ex(s - m)
        m_sc[...] = m
        l_sc[...] = jnp.sum(p, axis=-1, keepdims=True)
        acc_sc[...] = jnp.dot(p.astype(v.dtype), v, preferred_element_type=F32)

    def update(k, v):
        s = scores(k)
        m_prev = m_sc[...]
        m_new = jnp.maximum(m_prev, jnp.max(s, axis=-1, keepdims=True))
        alpha = ex(m_prev - m_new)
        p = ex(s - m_new)
        l_sc[...] = alpha * l_sc[...] + jnp.sum(p, axis=-1, keepdims=True)
        acc_sc[...] = alpha * acc_sc[...] + jnp.dot(p.astype(v.dtype), v, preferred_element_type=F32)
        m_sc[...] = m_new

    if has_ctx:
        first(kc_ref[...], vc_ref[...])
        start = 0
    else:
        first(k_ref[0:bk, :], v_ref[0:bk, :])
        start = 1

    def body(j, carry):
        off = pl.multiple_of(j * bk, bk)
        update(k_ref[pl.ds(off, bk), :], v_ref[pl.ds(off, bk), :])
        return carry

    lax.fori_loop(start, n_kv, body, 0)
    o = acc_sc[...] / l_sc[...]
    for r in range(rep):
        o_ref[:, r * dv:(r + 1) * dv] = o[r * bq:(r + 1) * bq].astype(o_ref.dtype)


def _attention(q_arr, k_arr, v_arr, o_prev, *, batch, groups, rep, dk, dv, qcol0, kcol0, vcol0,
               q_blk0, n_q, bq, kv_blk0, kv_len, bk, ctx_blk0, ctx_len, out_cols, name, log2_domain):
    has_ctx = ctx_blk0 is not None
    rows = q_arr.shape[0]
    in_specs = [pl.BlockSpec((bq, rep * dk), lambda b, g, i: (q_blk0 + b * n_q + i, qcol0 + g)),
                pl.BlockSpec((kv_len, dk), lambda b, g, i: (kv_blk0 + b, kcol0 + g)),
                pl.BlockSpec((kv_len, dv), lambda b, g, i: (kv_blk0 + b, vcol0 + g))]
    args = [q_arr, k_arr, v_arr]
    if has_ctx:
        in_specs += [pl.BlockSpec((ctx_len, dk), lambda b, g, i: (ctx_blk0 + b, kcol0 + g)),
                     pl.BlockSpec((ctx_len, dv), lambda b, g, i: (ctx_blk0 + b, vcol0 + g))]
        args += [k_arr, v_arr]
    aliases = {}
    if o_prev is not None:
        in_specs.append(pl.BlockSpec(memory_space=pl.ANY))
        args.append(o_prev)
        aliases = {len(args) - 1: 0}
    kern = functools.partial(_flash_kernel, rep=rep, dk=dk, dv=dv, bk=bk, n_kv=kv_len // bk, has_ctx=has_ctx,
                             log2_domain=log2_domain)
    if o_prev is not None:
        inner = kern

        def kern(*refs):
            n_in = 5 if has_ctx else 3
            return inner(*refs[:n_in], *refs[n_in + 1:])
    return pl.pallas_call(
        kern,
        grid=(batch, groups, n_q),
        in_specs=in_specs,
        out_specs=pl.BlockSpec((bq, rep * dv), lambda b, g, i: (q_blk0 + b * n_q + i, g)),
        out_shape=jax.ShapeDtypeStruct((rows, out_cols), BF16),
        scratch_shapes=[pltpu.VMEM((rep * bq, 1), F32), pltpu.VMEM((rep * bq, 1), F32),
                        pltpu.VMEM((rep * bq, dv), F32)],
        input_output_aliases=aliases,
        compiler_params=_params("arbitrary", "arbitrary", "arbitrary"),
        name=name,
    )(*args)


def _rpb_table_kernel(rpb_ref, o_ref, *, n_tiles, grid_rows):
    var = pl.program_id(0)
    h = pl.program_id(1)
    t_rep = jnp.where(var == 0, 0, jnp.where(var == 1, 1, n_tiles - 1))
    qc = lax.broadcasted_iota(jnp.int32, (GRID_W, GRID_W), 0)
    kc = lax.broadcasted_iota(jnp.int32, (GRID_W, GRID_W), 1)
    dc = kc - qc
    cstart = jnp.clip(qc - NA_WIN_COLS // 2, 0, GRID_W - NA_WIN_COLS)
    valid_c = (kc >= cstart) & (kc < cstart + NA_WIN_COLS)
    n_dc = 2 * NA_WIN_COLS - 1
    masked = jnp.full((GRID_W, GRID_W), MASK_VALUE, F32)
    toeplitz = {}
    for dr in range(-(NA_WIN_ROWS - 1), NA_WIN_ROWS):
        acc = jnp.zeros((GRID_W, GRID_W), F32)
        for b in range(n_dc):
            acc = jnp.where(dc == b - (NA_WIN_COLS - 1), rpb_ref[h, (dr + NA_WIN_ROWS - 1) * n_dc + b], acc)
        toeplitz[dr] = jnp.where(valid_c, acc, MASK_VALUE)
    for ql in range(NA_TILE_ROWS):
        qr = NA_TILE_ROWS * t_rep + ql
        rstart = jnp.clip(qr - NA_WIN_ROWS // 2, 0, grid_rows - NA_WIN_ROWS)
        for kl in range(3 * NA_TILE_ROWS):
            kr = NA_TILE_ROWS * (t_rep - 1) + kl
            dr = kl - NA_TILE_ROWS - ql
            if abs(dr) > NA_WIN_ROWS - 1:
                blk = masked
            else:
                ok = ((kr >= rstart) & (kr < rstart + NA_WIN_ROWS)).astype(F32)
                blk = toeplitz[dr] * ok + MASK_VALUE * (1.0 - ok)
            o_ref[0, 0, ql * GRID_W:(ql + 1) * GRID_W, kl * GRID_W:(kl + 1) * GRID_W] = blk


def _rpb_table(rpb, n_tiles, grid_rows):
    heads = rpb.shape[0]
    tq = NA_TILE_ROWS * GRID_W
    kern = functools.partial(_rpb_table_kernel, n_tiles=n_tiles, grid_rows=grid_rows)
    return pl.pallas_call(
        kern,
        grid=(3, heads),
        in_specs=[pl.BlockSpec(memory_space=pltpu.SMEM)],
        out_specs=pl.BlockSpec((1, 1, tq, 3 * tq), lambda v, h: (v, h, 0, 0)),
        out_shape=jax.ShapeDtypeStruct((3, heads, tq, 3 * tq), F32),
        compiler_params=_params("arbitrary", "arbitrary"),
        name="na_rpb_table",
    )(rpb.reshape(heads, -1))


def _na_kernel(q_ref, kp_ref, kc_ref, kn_ref, kx_ref, vp_ref, vc_ref, vn_ref, vx_ref, tab_ref, o_ref, *, heads, dh):
    tq = q_ref.shape[0]
    for h in range(heads):
        sl = slice(h * dh, (h + 1) * dh)
        q = q_ref[:, sl]

        def sc(k_ref):
            return lax.dot_general(q, k_ref[:, sl], (((1,), (1,)), ((), ())), preferred_element_type=F32)

        s = [sc(kp_ref) + tab_ref[0, h, :, 0:tq],
             sc(kc_ref) + tab_ref[0, h, :, tq:2 * tq],
             sc(kn_ref) + tab_ref[0, h, :, 2 * tq:3 * tq],
             sc(kx_ref)]
        m = jnp.max(s[0], axis=-1, keepdims=True)
        for sj in s[1:]:
            m = jnp.maximum(m, jnp.max(sj, axis=-1, keepdims=True))
        p = [jnp.exp(sj - m) for sj in s]
        l = p[0].sum(axis=-1, keepdims=True)
        for pj in p[1:]:
            l = l + pj.sum(axis=-1, keepdims=True)
        o = jnp.dot(p[0].astype(BF16), vp_ref[:, sl], preferred_element_type=F32)
        for pj, v_ref in zip(p[1:], (vc_ref, vn_ref, vx_ref)):
            o = o + jnp.dot(pj.astype(BF16), v_ref[:, sl], preferred_element_type=F32)
        o_ref[:, sl] = (o / l).astype(o_ref.dtype)


def _neighborhood_attention(qkv, table, *, batch, seq, ctx_len, heads, dh, hb):
    rows = qkv.shape[0]
    tq = NA_TILE_ROWS * GRID_W
    assert ctx_len == tq and seq % tq == 0
    n_tiles = seq // tq
    n_hg = heads // hb
    ctx_blk0 = batch * seq // ctx_len
    wb = hb * dh

    def var(t):
        return jnp.where(t == 0, 0, jnp.where(t == n_tiles - 1, 2, 1))

    def spec(col0, shift):
        if shift is None:
            return pl.BlockSpec((tq, wb), lambda b, g, t: (ctx_blk0 + b, col0 + g))
        return pl.BlockSpec((tq, wb), lambda b, g, t: (b * n_tiles + jnp.clip(t + shift, 0, n_tiles - 1), col0 + g))

    kern = functools.partial(_na_kernel, heads=hb, dh=dh)
    return pl.pallas_call(
        kern,
        grid=(batch, n_hg, n_tiles),
        in_specs=[spec(0, 0),
                  spec(n_hg, -1), spec(n_hg, 0), spec(n_hg, 1), spec(n_hg, None),
                  spec(2 * n_hg, -1), spec(2 * n_hg, 0), spec(2 * n_hg, 1), spec(2 * n_hg, None),
                  pl.BlockSpec((1, hb, tq, 3 * tq), lambda b, g, t: (var(t), g, 0, 0))],
        out_specs=pl.BlockSpec((tq, wb), lambda b, g, t: (b * n_tiles + t, g)),
        out_shape=jax.ShapeDtypeStruct((rows, heads * dh), BF16),
        compiler_params=_params("arbitrary", "arbitrary", "arbitrary"),
        name="na_attention",
    )(qkv, qkv, qkv, qkv, qkv, qkv, qkv, qkv, qkv, table)


def _final_norm_kernel(x_ref, g_ref, o_ref):
    o_ref[...] = _rms(x_ref[...], g_ref[...])


def _final_norm(x, g, n_row_tiles):
    bm = ROW_TILE
    d = x.shape[1]
    return pl.pallas_call(
        _final_norm_kernel,
        grid=(n_row_tiles,),
        in_specs=[pl.BlockSpec((bm, d), lambda m: (m, 0)), pl.BlockSpec((1, d), lambda m: (0, 0))],
        out_specs=pl.BlockSpec((bm, d), lambda m: (m, 0)),
        out_shape=jax.ShapeDtypeStruct((n_row_tiles * bm, d), F32),
        compiler_params=_params("arbitrary"),
        name="final_norm",
    )(x, g.reshape(1, d))


def _rope_tables(n_tok, rot_dim, batch, n_ctx_rows):
    quarter = rot_dim // 4
    t = jnp.arange(n_tok)
    row = (t // GRID_W).astype(F32)
    col = (t % GRID_W).astype(F32)
    inv = ROPE_THETA ** (-jnp.arange(quarter, dtype=F32) / quarter)
    ar = row[:, None] * inv
    ac = col[:, None] * inv
    ang = jnp.concatenate([ar, ar, ac, ac], axis=-1)
    cos, sin = jnp.cos(ang), jnp.sin(ang)
    first = (jnp.arange(rot_dim) % (2 * quarter)) < quarter
    sin_a = jnp.where(first, -sin, 0.0)
    sin_b = jnp.where(first, 0.0, sin)
    pad = LANES - rot_dim

    def full(tab, fill):
        tab = jnp.pad(tab, ((0, 0), (0, pad)), constant_values=fill)
        tab = jnp.tile(tab, (batch, 1))
        return jnp.concatenate([tab, jnp.full((n_ctx_rows, LANES), fill, F32)], axis=0)

    return full(cos, 1.0), full(sin_a, 0.0), full(sin_b, 0.0)


def kernel(x, c, ctx, c_ctx, ada_w, ada_b, mix_norm, ffn_norm, ffn_up, ffn_conv_w, ffn_conv_b, ffn_down, na_wqkv, na_rpb, na_wo, mla_wq_a, mla_q_norm, mla_wq_b, mla_wkv_a, mla_kv_norm, mla_wkv_b, mla_wo, gqa_wqkv, gqa_q_norm, gqa_k_norm, gqa_wo, final_norm):
    batch, seq, d = x.shape
    ctx_len = ctx.shape[1]
    depth = ada_w.shape[0]
    ffn_dim = ffn_down.shape[1]
    bm = ROW_TILE
    lat_rows = batch * seq
    ctx_rows = batch * ctx_len
    rows = lat_rows + ctx_rows
    assert seq % bm == 0 and ctx_rows % bm == 0 and batch < 8
    assert seq & (seq - 1) == 0 and ctx_len & (ctx_len - 1) == 0
    tiles_per_batch = seq // bm
    lat_tiles = lat_rows // bm
    all_tiles = rows // bm
    grid_rows = seq // GRID_W

    def mod_row(m):
        return jnp.minimum(m // tiles_per_batch, batch)

    xs = jnp.concatenate([x.reshape(lat_rows, d), ctx.reshape(ctx_rows, d)], axis=0)
    c_rows = jnp.zeros((8, d), F32).at[:batch].set(c).at[batch].set(c_ctx)
    mod_all = _modulation(c_rows, ada_w, ada_b)

    cos_b, sa_b, sb_b = _rope_tables(seq, MLA_ROPE_DIM, batch, ctx_rows)
    cos_c, sa_c, sb_c = _rope_tables(seq, GQA_HEAD_DIM, batch, ctx_rows)

    def row_spec(width):
        return pl.BlockSpec((bm, width), lambda m, n: (m, 0))

    rope_specs = [row_spec(LANES)] * 3

    for i in range(depth):
        last = i == depth - 1
        kind, j = i % N_MIXERS, i // N_MIXERS
        mod = mod_all[i].reshape(8, 1, 6 * d)
        n_tiles = lat_tiles if last else all_tiles

        if kind == 0:
            scale = (d // NA_HEADS) ** -0.5
            colscale = jnp.concatenate([jnp.full((1, d), scale, F32), jnp.ones((1, 2 * d), F32)], axis=1)
            bn = d
            qkv = _norm_matmul(
                xs, 0, d, mix_norm[i], mod, (1, 0), na_wqkv[j].astype(BF16), bn,
                [colscale], [pl.BlockSpec((1, bn), lambda m, n: (0, n))],
                jax.ShapeDtypeStruct((rows, 3 * d), BF16), pl.BlockSpec((bm, bn), lambda m, n: (m, n)),
                _epi_colscale, all_tiles, mod_row, "na_qkv")
            table = _rpb_table(na_rpb[j], seq // (NA_TILE_ROWS * GRID_W), grid_rows)
            dh = d // NA_HEADS
            o = _neighborhood_attention(qkv, table, batch=batch, seq=seq, ctx_len=ctx_len,
                                        heads=NA_HEADS, dh=dh, hb=8)
            if not last:
                o = _attention(qkv, qkv, qkv, o, batch=batch, groups=NA_HEADS, rep=1, dk=dh, dv=dh,
                               qcol0=0, kcol0=NA_HEADS, vcol0=2 * NA_HEADS,
                               q_blk0=lat_rows // ctx_len, n_q=1, bq=ctx_len,
                               kv_blk0=lat_rows // ctx_len, kv_len=ctx_len, bk=ctx_len,
                               ctx_blk0=None, ctx_len=None, out_cols=d, name="na_ctx_attention",
                               log2_domain=False)
            wo = na_wo[j]
        elif kind == 1:
            scale = (MLA_NOPE_DIM + MLA_ROPE_DIM) ** -0.5 * LOG2_E
            q_rank = mla_wq_a.shape[2]
            kv_rank = mla_kv_norm.shape[1]
            hq = 2 * LANES
            w1 = jnp.concatenate([mla_wq_a[j], mla_wkv_a[j], jnp.zeros((d, LANES - MLA_ROPE_DIM), F32)], axis=1)
            n1 = w1.shape[1]
            qc, kpe = _norm_matmul(
                xs, 0, d, mix_norm[i], mod, (1, 0), w1.astype(BF16), n1,
                [cos_b, sa_b, sb_b], rope_specs,
                (jax.ShapeDtypeStruct((rows, q_rank + kv_rank), F32), jax.ShapeDtypeStruct((rows, LANES), BF16)),
                (pl.BlockSpec((bm, q_rank + kv_rank), lambda m, n: (m, 0)), pl.BlockSpec((bm, LANES), lambda m, n: (m, 0))),
                _epi_mla_a, all_tiles, mod_row, "mla_a")
            wqb = mla_wq_b[j].reshape(q_rank, MLA_HEADS, MLA_NOPE_DIM + MLA_ROPE_DIM)
            wqb = jnp.pad(wqb, ((0, 0), (0, 0), (0, hq - MLA_NOPE_DIM - MLA_ROPE_DIM))).reshape(q_rank, MLA_HEADS * hq)
            bn = 4 * hq
            q = _norm_matmul(
                qc, 0, q_rank, mla_q_norm[j], None, None, wqb.astype(BF16), bn,
                [cos_b, sa_b, sb_b], rope_specs,
                jax.ShapeDtypeStruct((rows, MLA_HEADS * hq), BF16), pl.BlockSpec((bm, bn), lambda m, n: (m, n)),
                functools.partial(_epi_mla_q, scale=scale), all_tiles, mod_row, "mla_q")
            wkvb = mla_wkv_b[j].reshape(kv_rank, MLA_HEADS, MLA_NOPE_DIM + MLA_V_DIM)
            wk = wkvb[:, :, :MLA_NOPE_DIM].reshape(kv_rank, MLA_HEADS * MLA_NOPE_DIM)
            wv = wkvb[:, :, MLA_NOPE_DIM:].reshape(kv_rank, MLA_HEADS * MLA_V_DIM)
            bnk = 8 * MLA_NOPE_DIM
            k = _norm_matmul(
                qc, 1, kv_rank, mla_kv_norm[j], None, None, wk.astype(BF16), bnk,
                [kpe], [row_spec(LANES)],
                jax.ShapeDtypeStruct((rows, MLA_HEADS * hq), BF16), pl.BlockSpec((bm, 2 * bnk), lambda m, n: (m, n)),
                _epi_mla_k, all_tiles, mod_row, "mla_k")
            v = _norm_matmul(
                qc, 1, kv_rank, mla_kv_norm[j], None, None, wv.astype(BF16), bnk,
                [], [],
                jax.ShapeDtypeStruct((rows, MLA_HEADS * MLA_V_DIM), BF16), pl.BlockSpec((bm, bnk), lambda m, n: (m, n)),
                _epi_plain, all_tiles, mod_row, "mla_v")
            out_cols = MLA_HEADS * MLA_V_DIM
            o = _attention_t(q, k, v, batch=batch, seq=seq, ctx_len=ctx_len, heads=MLA_HEADS, groups=MLA_HEADS,
                             dk=hq, dv=MLA_V_DIM, qcol0=0, kcol0=0, vcol0=0, bq=min(1024, seq), bk=512, chunk=256,
                             out_cols=out_cols, name="mla_attention")
            if not last:
                o = _attention(q, k, v, o, batch=batch, groups=MLA_HEADS, rep=1, dk=hq, dv=MLA_V_DIM,
                               qcol0=0, kcol0=0, vcol0=0, q_blk0=lat_rows // ctx_len, n_q=1, bq=ctx_len,
                               kv_blk0=lat_rows // ctx_len, kv_len=ctx_len, bk=ctx_len,
                               ctx_blk0=None, ctx_len=None, out_cols=out_cols, name="mla_ctx_attention",
                               log2_domain=True)
            wo = mla_wo[j]
        else:
            scale = GQA_HEAD_DIM ** -0.5 * LOG2_E
            rep = GQA_HEADS // GQA_KV_HEADS
            nq_cols = GQA_HEADS * GQA_HEAD_DIM
            nk_cols = GQA_KV_HEADS * GQA_HEAD_DIM
            bn = nk_cols
            gain = jnp.concatenate([jnp.tile(gqa_q_norm[j], GQA_HEADS) * scale, jnp.tile(gqa_k_norm[j], GQA_KV_HEADS),
                                    jnp.ones((nk_cols,), F32)]).reshape(1, -1)
            qkv = _norm_matmul(
                xs, 0, d, mix_norm[i], mod, (1, 0), gqa_wqkv[j].astype(BF16), bn,
                [gain, cos_c, sa_c, sb_c], [pl.BlockSpec((1, bn), lambda m, n: (0, n))] + rope_specs,
                jax.ShapeDtypeStruct((rows, nq_cols + 2 * nk_cols), BF16), pl.BlockSpec((bm, bn), lambda m, n: (m, n)),
                functools.partial(_epi_gqa, n_qk_tiles=(nq_cols + nk_cols) // bn), all_tiles, mod_row, "gqa_qkv")
            dh = GQA_HEAD_DIM
            o = _attention_t(qkv, qkv, qkv, batch=batch, seq=seq, ctx_len=ctx_len, heads=GQA_HEADS,
                             groups=GQA_KV_HEADS, dk=dh, dv=dh, qcol0=0, kcol0=GQA_HEADS,
                             vcol0=GQA_HEADS + GQA_KV_HEADS, bq=min(256, seq), bk=512, chunk=256,
                             out_cols=nq_cols, name="gqa_attention")
            if not last:
                o = _attention(qkv, qkv, qkv, o, batch=batch, groups=GQA_KV_HEADS, rep=rep, dk=dh, dv=dh,
                               qcol0=0, kcol0=GQA_HEADS, vcol0=GQA_HEADS + GQA_KV_HEADS,
                               q_blk0=lat_rows // ctx_len, n_q=1, bq=ctx_len,
                               kv_blk0=lat_rows // ctx_len, kv_len=ctx_len, bk=ctx_len,
                               ctx_blk0=None, ctx_len=None, out_cols=nq_cols, name="gqa_ctx_attention",
                               log2_domain=True)
            wo = gqa_wo[j]

        xs = _matmul_residual(o, wo.astype(BF16), xs, mod, 2, d // 2, n_tiles, mod_row, "attn_out")
        a = _ffn_up(xs, ffn_norm[i], mod, ffn_up[i].astype(BF16), ffn_conv_w[i], ffn_conv_b[i], 512,
                    n_tiles, mod_row, lat_rows, seq, ctx_len, "ffn_up")
        xs = _matmul_residual(a, ffn_down[i].astype(BF16), xs, mod, 5, d // 4, n_tiles, mod_row, "ffn_down")

    out = _final_norm(xs, final_norm, lat_tiles)
    return out.reshape(batch, seq, d)
```

```python
import functools

import jax
import jax.numpy as jnp
from jax import lax
from jax.experimental import pallas as pl
from jax.experimental.pallas import tpu as pltpu

F32 = jnp.float32
BF16 = jnp.bfloat16

GRID_W = 64
NA_HEADS = 16
NA_WIN_ROWS = 8
NA_WIN_COLS = 16
MLA_HEADS = 16
MLA_NOPE_DIM = 128
MLA_ROPE_DIM = 64
MLA_V_DIM = 128
GQA_HEADS = 16
GQA_KV_HEADS = 4
GQA_HEAD_DIM = 128
CONV_WIDTH = 3
ROPE_THETA = 10000.0
NORM_EPS = 1e-6
N_MIXERS = 3
LOG2_E = 1.4426950408889634

LANES = 128
ROW_TILE = 512
HALO = 8
ONES_ROWS = 16
FLASH_UNROLL = 2
FLASH_LEAD = 2
NA_TILE_ROWS = 4
MASK_VALUE = -1e30
VMEM_LIMIT = 56 * 1024 * 1024


def _params(*sem):
    return pltpu.CompilerParams(dimension_semantics=sem, vmem_limit_bytes=VMEM_LIMIT)


def _rms(x, g):
    y = x * lax.rsqrt(jnp.mean(x * x, axis=-1, keepdims=True) + NORM_EPS)
    return y * g


def _rope(y, cos, sin_a, sin_b, quarter):
    return (y * cos + pltpu.roll(y, LANES - quarter, 1) * sin_a
            + pltpu.roll(y, quarter, 1) * sin_b)


def _mod_kernel(c_ref, w_ref, b_ref, o_ref):
    s = jax.nn.silu(c_ref[...])
    o_ref[0] = jnp.dot(s.astype(BF16), w_ref[0].astype(BF16), preferred_element_type=F32) + b_ref[0]


def _modulation(c_rows, ada_w, ada_b):
    depth, d, n = ada_w.shape
    bn = n // 8
    return pl.pallas_call(
        _mod_kernel,
        grid=(depth, n // bn),
        in_specs=[pl.BlockSpec((8, d), lambda l, j: (0, 0)),
                  pl.BlockSpec((1, d, bn), lambda l, j: (l, 0, j)),
                  pl.BlockSpec((1, 1, bn), lambda l, j: (l, 0, j))],
        out_specs=pl.BlockSpec((1, 8, bn), lambda l, j: (l, 0, j)),
        out_shape=jax.ShapeDtypeStruct((depth, 8, n), F32),
        compiler_params=_params("arbitrary", "arbitrary"),
        name="adaln_mod",
    )(c_rows, ada_w, ada_b.reshape(depth, 1, n))


def _norm_matmul_kernel(*refs, has_mod, n_extra, epilogue):
    if has_mod:
        x_ref, g_ref, sc_ref, sh_ref, w_ref = refs[:5]
        rest = refs[5:]
    else:
        x_ref, g_ref, w_ref = refs[:3]
        rest = refs[3:]
    extras, outs, h_sc = rest[:n_extra], rest[n_extra:-1], rest[-1]
    n = pl.program_id(1)

    @pl.when(n == 0)
    def _():
        h = _rms(x_ref[...], g_ref[...])
        if has_mod:
            h = h * (1.0 + sc_ref[0]) + sh_ref[0]
        h_sc[...] = h.astype(BF16)

    acc = jnp.dot(h_sc[...], w_ref[...], preferred_element_type=F32)
    epilogue(acc, n, extras, outs)


def _norm_matmul(x, xcol, kdim, g, mod, mod_chunks, w, bn, extras, extra_specs, out_shapes, out_specs,
                 epilogue, n_row_tiles, mod_row, name):
    bm = ROW_TILE
    n_col = w.shape[1] // bn
    has_mod = mod is not None
    in_specs = [pl.BlockSpec((bm, kdim), lambda m, n: (m, xcol)),
                pl.BlockSpec((1, kdim), lambda m, n: (0, 0))]
    args = [x, g.reshape(1, kdim)]
    if has_mod:
        sc_chunk, sh_chunk = mod_chunks
        in_specs += [pl.BlockSpec((1, 1, kdim), lambda m, n: (mod_row(m), 0, sc_chunk)),
                     pl.BlockSpec((1, 1, kdim), lambda m, n: (mod_row(m), 0, sh_chunk))]
        args += [mod, mod]
    in_specs.append(pl.BlockSpec((kdim, bn), lambda m, n: (0, n)))
    args.append(w)
    in_specs += extra_specs
    args += extras
    kern = functools.partial(_norm_matmul_kernel, has_mod=has_mod, n_extra=len(extras), epilogue=epilogue)
    return pl.pallas_call(
        kern,
        grid=(n_row_tiles, n_col),
        in_specs=in_specs,
        out_specs=out_specs,
        out_shape=out_shapes,
        scratch_shapes=[pltpu.VMEM((bm, kdim), BF16)],
        compiler_params=_params("arbitrary", "arbitrary"),
        name=name,
    )(*args)


def _epi_colscale(acc, n, extras, outs):
    (cs_ref,), (o_ref,) = extras, outs
    o_ref[...] = (acc * cs_ref[...]).astype(o_ref.dtype)


def _epi_plain(acc, n, extras, outs):
    (o_ref,) = outs
    o_ref[...] = acc.astype(o_ref.dtype)


def _epi_mla_a(acc, n, extras, outs):
    cos_ref, sa_ref, sb_ref = extras
    qc_ref, kpe_ref = outs
    wide = qc_ref.shape[1]
    qc_ref[...] = acc[:, :wide]
    kpe = _rope(acc[:, wide:], cos_ref[...], sa_ref[...], sb_ref[...], MLA_ROPE_DIM // 4)
    kpe_ref[...] = kpe.astype(kpe_ref.dtype)


def _epi_mla_q(acc, n, extras, outs, *, scale):
    cos_ref, sa_ref, sb_ref = extras
    (o_ref,) = outs
    for c in range(acc.shape[1] // LANES):
        y = acc[:, c * LANES:(c + 1) * LANES]
        if c % 2 == 1:
            y = _rope(y, cos_ref[...], sa_ref[...], sb_ref[...], MLA_ROPE_DIM // 4)
        o_ref[:, c * LANES:(c + 1) * LANES] = (y * scale).astype(o_ref.dtype)


def _epi_mla_k(acc, n, extras, outs):
    (kpe_ref,), (o_ref,) = extras, outs
    kpe = kpe_ref[...]
    for c in range(acc.shape[1] // LANES):
        o_ref[:, (2 * c) * LANES:(2 * c + 1) * LANES] = acc[:, c * LANES:(c + 1) * LANES].astype(o_ref.dtype)
        o_ref[:, (2 * c + 1) * LANES:(2 * c + 2) * LANES] = kpe


def _epi_gqa(acc, n, extras, outs, *, n_qk_tiles):
    gain_ref, cos_ref, sa_ref, sb_ref = extras
    (o_ref,) = outs

    @pl.when(n < n_qk_tiles)
    def _():
        for c in range(acc.shape[1] // LANES):
            y = _rms(acc[:, c * LANES:(c + 1) * LANES], gain_ref[:, c * LANES:(c + 1) * LANES])
            y = _rope(y, cos_ref[...], sa_ref[...], sb_ref[...], GQA_HEAD_DIM // 4)
            o_ref[:, c * LANES:(c + 1) * LANES] = y.astype(o_ref.dtype)

    @pl.when(n >= n_qk_tiles)
    def _():
        o_ref[...] = acc.astype(o_ref.dtype)


def _mm_res_kernel(a_ref, w_ref, x_ref, g_ref, o_ref):
    acc = jnp.dot(a_ref[...], w_ref[...], preferred_element_type=F32)
    o_ref[...] = x_ref[...] + g_ref[0] * acc


def _matmul_residual(a, w, x, mod, gate_chunk, bn, n_row_tiles, mod_row, name):
    bm = ROW_TILE
    kdim, n_out = w.shape
    assert n_out == x.shape[1] and n_out % bn == 0
    per_chunk = n_out // bn
    return pl.pallas_call(
        _mm_res_kernel,
        grid=(n_row_tiles, per_chunk),
        in_specs=[pl.BlockSpec((bm, kdim), lambda m, n: (m, 0)),
                  pl.BlockSpec((kdim, bn), lambda m, n: (0, n)),
                  pl.BlockSpec((bm, bn), lambda m, n: (m, n)),
                  pl.BlockSpec((1, 1, bn), lambda m, n: (mod_row(m), 0, gate_chunk * per_chunk + n))],
        out_specs=pl.BlockSpec((bm, bn), lambda m, n: (m, n)),
        out_shape=jax.ShapeDtypeStruct(x.shape, x.dtype),
        input_output_aliases={2: 0},
        compiler_params=_params("arbitrary", "arbitrary"),
        name=name,
    )(a, w, x, mod)


def _ffn_up_kernel(x_ref, xp_ref, xn_ref, g_ref, sc_ref, sh_ref, wg_ref, wv_ref, cw_ref, cb_ref, o_ref, h_sc,
                   *, lat_rows, lat_seq, ctx_seq):
    bm = x_ref.shape[0]
    m = pl.program_id(0)

    @pl.when(pl.program_id(1) == 0)
    def _():
        def nm(x):
            return (_rms(x, g_ref[...]) * (1.0 + sc_ref[0]) + sh_ref[0]).astype(BF16)
        h_sc[0:bm, :] = nm(x_ref[...])
        h_sc[bm:bm + 2 * HALO, :] = nm(jnp.concatenate([xn_ref[...], xp_ref[...]], axis=0))

    gate = jnp.dot(h_sc[...], wg_ref[...], preferred_element_type=F32)
    val = jnp.dot(h_sc[0:bm, :], wv_ref[...], preferred_element_type=F32)
    ext = bm + 2 * HALO
    g_prev = pltpu.roll(gate, 1, 0)[0:bm]
    g_next = pltpu.roll(gate, ext - 1, 0)[0:bm]
    g_cur = gate[0:bm]
    row = m * bm + lax.broadcasted_iota(jnp.int32, (bm, 1), 0)
    in_ctx = row >= lat_rows
    pos = jnp.where(in_ctx, (row - lat_rows) & (ctx_seq - 1), row & (lat_seq - 1))
    seq = jnp.where(in_ctx, ctx_seq, lat_seq)
    g_prev = jnp.where(pos == 0, 0.0, g_prev)
    g_next = jnp.where(pos == seq - 1, 0.0, g_next)
    cw = cw_ref[...]
    z = cb_ref[...] + g_prev * cw[0:1] + g_cur * cw[1:2] + g_next * cw[2:3]
    o_ref[...] = (jax.nn.silu(z) * val).astype(o_ref.dtype)


def _ffn_up(x, g, mod, w_up, conv_w, conv_b, bn, n_row_tiles, mod_row, lat_rows, lat_seq, ctx_seq, name):
    bm = ROW_TILE
    rows, d = x.shape
    f = w_up.shape[1] // 2
    n_col = f // bn
    last_halo = rows // HALO - 1
    per = bm // HALO
    kern = functools.partial(_ffn_up_kernel, lat_rows=lat_rows, lat_seq=lat_seq, ctx_seq=ctx_seq)
    return pl.pallas_call(
        kern,
        grid=(n_row_tiles, n_col),
        in_specs=[pl.BlockSpec((bm, d), lambda m, n: (m, 0)),
                  pl.BlockSpec((HALO, d), lambda m, n: (jnp.maximum(m * per - 1, 0), 0)),
                  pl.BlockSpec((HALO, d), lambda m, n: (jnp.minimum((m + 1) * per, last_halo), 0)),
                  pl.BlockSpec((1, d), lambda m, n: (0, 0)),
                  pl.BlockSpec((1, 1, d), lambda m, n: (mod_row(m), 0, 4)),
                  pl.BlockSpec((1, 1, d), lambda m, n: (mod_row(m), 0, 3)),
                  pl.BlockSpec((d, bn), lambda m, n: (0, n)),
                  pl.BlockSpec((d, bn), lambda m, n: (0, n_col + n)),
                  pl.BlockSpec((CONV_WIDTH, bn), lambda m, n: (0, n)),
                  pl.BlockSpec((1, bn), lambda m, n: (0, n))],
        out_specs=pl.BlockSpec((bm, bn), lambda m, n: (m, n)),
        out_shape=jax.ShapeDtypeStruct((rows, f), BF16),
        scratch_shapes=[pltpu.VMEM((bm + 2 * HALO, d), BF16)],
        compiler_params=_params("arbitrary", "arbitrary"),
        name=name,
    )(x, x, x, g.reshape(1, d), mod, mod, w_up, w_up, conv_w, conv_b.reshape(1, f))


def _flash_t_kernel(qt_ref, k_ref, vt_ref, kc_ref, vct_ref, o_ref, m_sc, acc_sc, s_sc, *, rep, bk, n_kv, chunk,
                    unroll, lead):
    bq = qt_ref.shape[3]
    dv = vt_ref.shape[3] - ONES_ROWS
    n_chunks = rep * bq // chunk

    def scores(k, c):
        r, off = divmod(c * chunk, bq)
        return jnp.dot(k, qt_ref[0, r, :, off:off + chunk], preferred_element_type=F32)

    def softmax_pv(s, vt, c, is_first):
        cs = slice(c * chunk, (c + 1) * chunk)
        mx = jnp.max(s, axis=0, keepdims=True)
        if is_first:
            m_new = mx
            p = jnp.exp2(s - m_new)
            acc_sc[:, cs] = jnp.dot(vt, p.astype(vt.dtype), preferred_element_type=F32)
        else:
            m_prev = m_sc[:, cs]
            m_new = jnp.maximum(m_prev, mx)
            alpha = jnp.exp2(m_prev - m_new)
            p = jnp.exp2(s - m_new)
            acc_sc[:, cs] = alpha * acc_sc[:, cs] + jnp.dot(vt, p.astype(vt.dtype), preferred_element_type=F32)
        m_sc[:, cs] = m_new

    kc = kc_ref[...]
    k0 = k_ref[0:bk, :]
    for c in range(n_chunks):
        s_ctx = scores(kc, c)
        s_sc[0, c] = scores(k0, c)
        softmax_pv(s_ctx, vct_ref[0, 0], c, True)

    def body(i, carry):
        def issue_scores(t):
            u, c = divmod(t, n_chunks)
            j_next = jnp.minimum(unroll * i + u + 1, n_kv - 1)
            off = pl.multiple_of(j_next * bk, bk)
            s_sc[(u + 1) % 2, c] = scores(k_ref[pl.ds(off, bk), :], c)

        def consume(t):
            u, c = divmod(t, n_chunks)
            softmax_pv(s_sc[u % 2, c], vt_ref[0, 0, unroll * i + u], c, False)

        n_tasks = unroll * n_chunks
        for t in range(n_tasks + lead):
            if t < n_tasks:
                issue_scores(t)
            if t >= lead:
                consume(t - lead)
        return carry

    assert unroll % 2 == 0 and n_kv % unroll == 0 and lead < n_chunks
    lax.fori_loop(0, n_kv // unroll, body, 0)
    for r in range(rep):
        ot = acc_sc[0:dv, r * bq:(r + 1) * bq] / acc_sc[dv:dv + 1, r * bq:(r + 1) * bq]
        o_ref[:, r * dv:(r + 1) * dv] = ot.T.astype(o_ref.dtype)


def _attention_t(q_arr, k_arr, v_arr, *, batch, seq, ctx_len, heads, groups, dk, dv, qcol0, kcol0, vcol0,
                 bq, bk, chunk, out_cols, name):
    rows = q_arr.shape[0]
    lat_rows = batch * seq
    rep = heads // groups
    n_q, n_kv = seq // bq, seq // bk
    ctx_blk0 = lat_rows // ctx_len
    qt = q_arr[:lat_rows, qcol0 * dk:(qcol0 + heads) * dk].reshape(batch, seq, heads, dk).transpose(0, 2, 3, 1)
    v_lat = v_arr[:lat_rows, vcol0 * dv:(vcol0 + groups) * dv]
    vt = v_lat.reshape(batch, n_kv, bk, groups, dv).transpose(0, 3, 1, 4, 2)
    vt = jnp.concatenate([vt, jnp.ones(vt.shape[:3] + (ONES_ROWS, bk), vt.dtype)], axis=3)
    v_ctx = v_arr[lat_rows:, vcol0 * dv:(vcol0 + groups) * dv]
    vct = v_ctx.reshape(batch, ctx_len, groups, dv).transpose(0, 2, 3, 1)
    vct = jnp.concatenate([vct, jnp.ones(vct.shape[:2] + (ONES_ROWS, ctx_len), vct.dtype)], axis=2)
    dve = dv + ONES_ROWS
    kern = functools.partial(_flash_t_kernel, rep=rep, bk=bk, n_kv=n_kv, chunk=chunk,
                             unroll=min(FLASH_UNROLL, n_kv), lead=FLASH_LEAD)
    return pl.pallas_call(
        kern,
        grid=(batch, groups, n_q),
        in_specs=[pl.BlockSpec((1, rep, dk, bq), lambda b, g, i: (b, g, 0, i)),
                  pl.BlockSpec((seq, dk), lambda b, g, i: (b, kcol0 + g)),
                  pl.BlockSpec((1, 1, n_kv, dve, bk), lambda b, g, i: (b, g, 0, 0, 0)),
                  pl.BlockSpec((ctx_len, dk), lambda b, g, i: (ctx_blk0 + b, kcol0 + g)),
                  pl.BlockSpec((1, 1, dve, ctx_len), lambda b, g, i: (b, g, 0, 0))],
        out_specs=pl.BlockSpec((bq, rep * dv), lambda b, g, i: (b * n_q + i, g)),
        out_shape=jax.ShapeDtypeStruct((rows, out_cols), BF16),
        scratch_shapes=[pltpu.VMEM((1, rep * bq), F32), pltpu.VMEM((dve, rep * bq), F32),
                        pltpu.VMEM((2, rep * bq // chunk, bk, chunk), F32)],
        compiler_params=_params("arbitrary", "arbitrary", "arbitrary"),
        name=name,
    )(qt, k_arr, vt, k_arr, vct)


def _flash_kernel(*refs, rep, dk, dv, bk, n_kv, has_ctx, log2_domain):
    if has_ctx:
        q_ref, k_ref, v_ref, kc_ref, vc_ref, o_ref, m_sc, l_sc, acc_sc = refs
    else:
        q_ref, k_ref, v_ref, o_ref, m_sc, l_sc, acc_sc = refs
    ex = jnp.exp2 if log2_domain else jnp.exp
    bq = q_ref.shape[0]
    if rep > 1:
        q = jnp.concatenate([q_ref[:, r * dk:(r + 1) * dk] for r in range(rep)], axis=0)
    else:
        q = q_ref[...]

    def scores(k):
        return lax.dot_general(q, k, (((1,), (1,)), ((), ())), preferred_element_type=F32)

    def first(k, v):
        s = scores(k)
        m = jnp.max(s, axis=-1, keepdims=True)
        p = ex(s - m)
        m_sc[...] = m
        l_sc[...] = jnp.sum(p, axis=-1, keepdims=True)
        acc_sc[...] = jnp.dot(p.astype(v.dtype), v, preferred_element_type=F32)

    def update(k, v):
        s = scores(k)
        m_prev = m_sc[...]
        m_new = jnp.maximum(m_prev, jnp.max(s, axis=-1, keepdims=True))
        alpha = ex(m_prev - m_new)
        p = ex(s - m_new)
        l_sc[...] = alpha * l_sc[...] + jnp.sum(p, axis=-1, keepdims=True)
        acc_sc[...] = alpha * acc_sc[...] + jnp.dot(p.astype(v.dtype), v, preferred_element_type=F32)
        m_sc[...] = m_new

    if has_ctx:
        first(kc_ref[...], vc_ref[...])
        start = 0
    else:
        first(k_ref[0:bk, :], v_ref[0:bk, :])
        start = 1

    def body(j, carry):
        off = pl.multiple_of(j * bk, bk)
        update(k_ref[pl.ds(off, bk), :], v_ref[pl.ds(off, bk), :])
        return carry

    lax.fori_loop(start, n_kv, body, 0)
    o = acc_sc[...] / l_sc[...]
    for r in range(rep):
        o_ref[:, r * dv:(r + 1) * dv] = o[r * bq:(r + 1) * bq].astype(o_ref.dtype)


def _attention(q_arr, k_arr, v_arr, o_prev, *, batch, groups, rep, dk, dv, qcol0, kcol0, vcol0,
               q_blk0, n_q, bq, kv_blk0, kv_len, bk, ctx_blk0, ctx_len, out_cols, name, log2_domain):
    has_ctx = ctx_blk0 is not None
    rows = q_arr.shape[0]
    in_specs = [pl.BlockSpec((bq, rep * dk), lambda b, g, i: (q_blk0 + b * n_q + i, qcol0 + g)),
                pl.BlockSpec((kv_len, dk), lambda b, g, i: (kv_blk0 + b, kcol0 + g)),
                pl.BlockSpec((kv_len, dv), lambda b, g, i: (kv_blk0 + b, vcol0 + g))]
    args = [q_arr, k_arr, v_arr]
    if has_ctx:
        in_specs += [pl.BlockSpec((ctx_len, dk), lambda b, g, i: (ctx_blk0 + b, kcol0 + g)),
                     pl.BlockSpec((ctx_len, dv), lambda b, g, i: (ctx_blk0 + b, vcol0 + g))]
        args += [k_arr, v_arr]
    aliases = {}
    if o_prev is not None:
        in_specs.append(pl.BlockSpec(memory_space=pl.ANY))
        args.append(o_prev)
        aliases = {len(args) - 1: 0}
    kern = functools.partial(_flash_kernel, rep=rep, dk=dk, dv=dv, bk=bk, n_kv=kv_len // bk, has_ctx=has_ctx,
                             log2_domain=log2_domain)
    if o_prev is not None:
        inner = kern

        def kern(*refs):
            n_in = 5 if has_ctx else 3
            return inner(*refs[:n_in], *refs[n_in + 1:])
    return pl.pallas_call(
        kern,
        grid=(batch, groups, n_q),
        in_specs=in_specs,
        out_specs=pl.BlockSpec((bq, rep * dv), lambda b, g, i: (q_blk0 + b * n_q + i, g)),
        out_shape=jax.ShapeDtypeStruct((rows, out_cols), BF16),
        scratch_shapes=[pltpu.VMEM((rep * bq, 1), F32), pltpu.VMEM((rep * bq, 1), F32),
                        pltpu.VMEM((rep * bq, dv), F32)],
        input_output_aliases=aliases,
        compiler_params=_params("arbitrary", "arbitrary", "arbitrary"),
        name=name,
    )(*args)


def _rpb_table_kernel(rpb_ref, o_ref, *, n_tiles, grid_rows):
    var = pl.program_id(0)
    h = pl.program_id(1)
    t_rep = jnp.where(var == 0, 0, jnp.where(var == 1, 1, n_tiles - 1))
    qc = lax.broadcasted_iota(jnp.int32, (GRID_W, GRID_W), 0)
    kc = lax.broadcasted_iota(jnp.int32, (GRID_W, GRID_W), 1)
    dc = kc - qc
    cstart = jnp.clip(qc - NA_WIN_COLS // 2, 0, GRID_W - NA_WIN_COLS)
    valid_c = (kc >= cstart) & (kc < cstart + NA_WIN_COLS)
    n_dc = 2 * NA_WIN_COLS - 1
    masked = jnp.full((GRID_W, GRID_W), MASK_VALUE, F32)
    toeplitz = {}
    for dr in range(-(NA_WIN_ROWS - 1), NA_WIN_ROWS):
        acc = jnp.zeros((GRID_W, GRID_W), F32)
        for b in range(n_dc):
            acc = jnp.where(dc == b - (NA_WIN_COLS - 1), rpb_ref[h, (dr + NA_WIN_ROWS - 1) * n_dc + b], acc)
        toeplitz[dr] = jnp.where(valid_c, acc, MASK_VALUE)
    for ql in range(NA_TILE_ROWS):
        qr = NA_TILE_ROWS * t_rep + ql
        rstart = jnp.clip(qr - NA_WIN_ROWS // 2, 0, grid_rows - NA_WIN_ROWS)
        for kl in range(3 * NA_TILE_ROWS):
            kr = NA_TILE_ROWS * (t_rep - 1) + kl
            dr = kl - NA_TILE_ROWS - ql
            if abs(dr) > NA_WIN_ROWS - 1:
                blk = masked
            else:
                ok = ((kr >= rstart) & (kr < rstart + NA_WIN_ROWS)).astype(F32)
                blk = toeplitz[dr] * ok + MASK_VALUE * (1.0 - ok)
            o_ref[0, 0, ql * GRID_W:(ql + 1) * GRID_W, kl * GRID_W:(kl + 1) * GRID_W] = blk


def _rpb_table(rpb, n_tiles, grid_rows):
    heads = rpb.shape[0]
    tq = NA_TILE_ROWS * GRID_W
    kern = functools.partial(_rpb_table_kernel, n_tiles=n_tiles, grid_rows=grid_rows)
    return pl.pallas_call(
        kern,
        grid=(3, heads),
        in_specs=[pl.BlockSpec(memory_space=pltpu.SMEM)],
        out_specs=pl.BlockSpec((1, 1, tq, 3 * tq), lambda v, h: (v, h, 0, 0)),
        out_shape=jax.ShapeDtypeStruct((3, heads, tq, 3 * tq), F32),
        compiler_params=_params("arbitrary", "arbitrary"),
        name="na_rpb_table",
    )(rpb.reshape(heads, -1))


def _na_kernel(q_ref, kp_ref, kc_ref, kn_ref, kx_ref, vp_ref, vc_ref, vn_ref, vx_ref, tab_ref, o_ref, *, heads, dh):
    tq = q_ref.shape[0]
    for h in range(heads):
        sl = slice(h * dh, (h + 1) * dh)
        q = q_ref[:, sl]

        def sc(k_ref):
            return lax.dot_general(q, k_ref[:, sl], (((1,), (1,)), ((), ())), preferred_element_type=F32)

        s = [sc(kp_ref) + tab_ref[0, h, :, 0:tq],
             sc(kc_ref) + tab_ref[0, h, :, tq:2 * tq],
             sc(kn_ref) + tab_ref[0, h, :, 2 * tq:3 * tq],
             sc(kx_ref)]
        m = jnp.max(s[0], axis=-1, keepdims=True)
        for sj in s[1:]:
            m = jnp.maximum(m, jnp.max(sj, axis=-1, keepdims=True))
        p = [jnp.exp(sj - m) for sj in s]
        l = p[0].sum(axis=-1, keepdims=True)
        for pj in p[1:]:
            l = l + pj.sum(axis=-1, keepdims=True)
        o = jnp.dot(p[0].astype(BF16), vp_ref[:, sl], preferred_element_type=F32)
        for pj, v_ref in zip(p[1:], (vc_ref, vn_ref, vx_ref)):
            o = o + jnp.dot(pj.astype(BF16), v_ref[:, sl], preferred_element_type=F32)
        o_ref[:, sl] = (o / l).astype(o_ref.dtype)


def _neighborhood_attention(qkv, table, *, batch, seq, ctx_len, heads, dh, hb):
    rows = qkv.shape[0]
    tq = NA_TILE_ROWS * GRID_W
    assert ctx_len == tq and seq % tq == 0
    n_tiles = seq // tq
    n_hg = heads // hb
    ctx_blk0 = batch * seq // ctx_len
    wb = hb * dh

    def var(t):
        return jnp.where(t == 0, 0, jnp.where(t == n_tiles - 1, 2, 1))

    def spec(col0, shift):
        if shift is None:
            return pl.BlockSpec((tq, wb), lambda b, g, t: (ctx_blk0 + b, col0 + g))
        return pl.BlockSpec((tq, wb), lambda b, g, t: (b * n_tiles + jnp.clip(t + shift, 0, n_tiles - 1), col0 + g))

    kern = functools.partial(_na_kernel, heads=hb, dh=dh)
    return pl.pallas_call(
        kern,
        grid=(batch, n_hg, n_tiles),
        in_specs=[spec(0, 0),
                  spec(n_hg, -1), spec(n_hg, 0), spec(n_hg, 1), spec(n_hg, None),
                  spec(2 * n_hg, -1), spec(2 * n_hg, 0), spec(2 * n_hg, 1), spec(2 * n_hg, None),
                  pl.BlockSpec((1, hb, tq, 3 * tq), lambda b, g, t: (var(t), g, 0, 0))],
        out_specs=pl.BlockSpec((tq, wb), lambda b, g, t: (b * n_tiles + t, g)),
        out_shape=jax.ShapeDtypeStruct((rows, heads * dh), BF16),
        compiler_params=_params("arbitrary", "arbitrary", "arbitrary"),
        name="na_attention",
    )(qkv, qkv, qkv, qkv, qkv, qkv, qkv, qkv, qkv, table)


def _final_norm_kernel(x_ref, g_ref, o_ref):
    o_ref[...] = _rms(x_ref[...], g_ref[...])


def _final_norm(x, g, n_row_tiles):
    bm = ROW_TILE
    d = x.shape[1]
    return pl.pallas_call(
        _final_norm_kernel,
        grid=(n_row_tiles,),
        in_specs=[pl.BlockSpec((bm, d), lambda m: (m, 0)), pl.BlockSpec((1, d), lambda m: (0, 0))],
        out_specs=pl.BlockSpec((bm, d), lambda m: (m, 0)),
        out_shape=jax.ShapeDtypeStruct((n_row_tiles * bm, d), F32),
        compiler_params=_params("arbitrary"),
        name="final_norm",
    )(x, g.reshape(1, d))


def _rope_tables(n_tok, rot_dim, batch, n_ctx_rows):
    quarter = rot_dim // 4
    t = jnp.arange(n_tok)
    row = (t // GRID_W).astype(F32)
    col = (t % GRID_W).astype(F32)
    inv = ROPE_THETA ** (-jnp.arange(quarter, dtype=F32) / quarter)
    ar = row[:, None] * inv
    ac = col[:, None] * inv
    ang = jnp.concatenate([ar, ar, ac, ac], axis=-1)
    cos, sin = jnp.cos(ang), jnp.sin(ang)
    first = (jnp.arange(rot_dim) % (2 * quarter)) < quarter
    sin_a = jnp.where(first, -sin, 0.0)
    sin_b = jnp.where(first, 0.0, sin)
    pad = LANES - rot_dim

    def full(tab, fill):
        tab = jnp.pad(tab, ((0, 0), (0, pad)), constant_values=fill)
        tab = jnp.tile(tab, (batch, 1))
        return jnp.concatenate([tab, jnp.full((n_ctx_rows, LANES), fill, F32)], axis=0)

    return full(cos, 1.0), full(sin_a, 0.0), full(sin_b, 0.0)


def kernel(x, c, ctx, c_ctx, ada_w, ada_b, mix_norm, ffn_norm, ffn_up, ffn_conv_w, ffn_conv_b, ffn_down, na_wqkv, na_rpb, na_wo, mla_wq_a, mla_q_norm, mla_wq_b, mla_wkv_a, mla_kv_norm, mla_wkv_b, mla_wo, gqa_wqkv, gqa_q_norm, gqa_k_norm, gqa_wo, final_norm):
    batch, seq, d = x.shape
    ctx_len = ctx.shape[1]
    depth = ada_w.shape[0]
    ffn_dim = ffn_down.shape[1]
    bm = ROW_TILE
    lat_rows = batch * seq
    ctx_rows = batch * ctx_len
    rows = lat_rows + ctx_rows
    assert seq % bm == 0 and ctx_rows % bm == 0 and batch < 8
    assert seq & (seq - 1) == 0 and ctx_len & (ctx_len - 1) == 0
    tiles_per_batch = seq // bm
    lat_tiles = lat_rows // bm
    all_tiles = rows // bm
    grid_rows = seq // GRID_W

    def mod_row(m):
        return jnp.minimum(m // tiles_per_batch, batch)

    xs = jnp.concatenate([x.reshape(lat_rows, d), ctx.reshape(ctx_rows, d)], axis=0)
    c_rows = jnp.zeros((8, d), F32).at[:batch].set(c).at[batch].set(c_ctx)
    mod_all = _modulation(c_rows, ada_w, ada_b)

    cos_b, sa_b, sb_b = _rope_tables(seq, MLA_ROPE_DIM, batch, ctx_rows)
    cos_c, sa_c, sb_c = _rope_tables(seq, GQA_HEAD_DIM, batch, ctx_rows)

    def row_spec(width):
        return pl.BlockSpec((bm, width), lambda m, n: (m, 0))

    rope_specs = [row_spec(LANES)] * 3

    for i in range(depth):
        last = i == depth - 1
        kind, j = i % N_MIXERS, i // N_MIXERS
        mod = mod_all[i].reshape(8, 1, 6 * d)
        n_tiles = lat_tiles if last else all_tiles

        if kind == 0:
            scale = (d // NA_HEADS) ** -0.5
            colscale = jnp.concatenate([jnp.full((1, d), scale, F32), jnp.ones((1, 2 * d), F32)], axis=1)
            bn = d
            qkv = _norm_matmul(
                xs, 0, d, mix_norm[i], mod, (1, 0), na_wqkv[j].astype(BF16), bn,
                [colscale], [pl.BlockSpec((1, bn), lambda m, n: (0, n))],
                jax.ShapeDtypeStruct((rows, 3 * d), BF16), pl.BlockSpec((bm, bn), lambda m, n: (m, n)),
                _epi_colscale, all_tiles, mod_row, "na_qkv")
            table = _rpb_table(na_rpb[j], seq // (NA_TILE_ROWS * GRID_W), grid_rows)
            dh = d // NA_HEADS
            o = _neighborhood_attention(qkv, table, batch=batch, seq=seq, ctx_len=ctx_len,
                                        heads=NA_HEADS, dh=dh, hb=8)
            if not last:
                o = _attention(qkv, qkv, qkv, o, batch=batch, groups=NA_HEADS, rep=1, dk=dh, dv=dh,
                               qcol0=0, kcol0=NA_HEADS, vcol0=2 * NA_HEADS,
                               q_blk0=lat_rows // ctx_len, n_q=1, bq=ctx_len,
                               kv_blk0=lat_rows // ctx_len, kv_len=ctx_len, bk=ctx_len,
                               ctx_blk0=None, ctx_len=None, out_cols=d, name="na_ctx_attention",
                               log2_domain=False)
            wo = na_wo[j]
        elif kind == 1:
            scale = (MLA_NOPE_DIM + MLA_ROPE_DIM) ** -0.5 * LOG2_E
            q_rank = mla_wq_a.shape[2]
            kv_rank = mla_kv_norm.shape[1]
            hq = 2 * LANES
            w1 = jnp.concatenate([mla_wq_a[j], mla_wkv_a[j], jnp.zeros((d, LANES - MLA_ROPE_DIM), F32)], axis=1)
            n1 = w1.shape[1]
            qc, kpe = _norm_matmul(
                xs, 0, d, mix_norm[i], mod, (1, 0), w1.astype(BF16), n1,
                [cos_b, sa_b, sb_b], rope_specs,
                (jax.ShapeDtypeStruct((rows, q_rank + kv_rank), F32), jax.ShapeDtypeStruct((rows, LANES), BF16)),
                (pl.BlockSpec((bm, q_rank + kv_rank), lambda m, n: (m, 0)), pl.BlockSpec((bm, LANES), lambda m, n: (m, 0))),
                _epi_mla_a, all_tiles, mod_row, "mla_a")
            wqb = mla_wq_b[j].reshape(q_rank, MLA_HEADS, MLA_NOPE_DIM + MLA_ROPE_DIM)
            wqb = jnp.pad(wqb, ((0, 0), (0, 0), (0, hq - MLA_NOPE_DIM - MLA_ROPE_DIM))).reshape(q_rank, MLA_HEADS * hq)
            bn = 4 * hq
            q = _norm_matmul(
                qc, 0, q_rank, mla_q_norm[j], None, None, wqb.astype(BF16), bn,
                [cos_b, sa_b, sb_b], rope_specs,
                jax.ShapeDtypeStruct((rows, MLA_HEADS * hq), BF16), pl.BlockSpec((bm, bn), lambda m, n: (m, n)),
                functools.partial(_epi_mla_q, scale=scale), all_tiles, mod_row, "mla_q")
            wkvb = mla_wkv_b[j].reshape(kv_rank, MLA_HEADS, MLA_NOPE_DIM + MLA_V_DIM)
            wk = wkvb[:, :, :MLA_NOPE_DIM].reshape(kv_rank, MLA_HEADS * MLA_NOPE_DIM)
            wv = wkvb[:, :, MLA_NOPE_DIM:].reshape(kv_rank, MLA_HEADS * MLA_V_DIM)
            bnk = 8 * MLA_NOPE_DIM
            k = _norm_matmul(
                qc, 1, kv_rank, mla_kv_norm[j], None, None, wk.astype(BF16), bnk,
                [kpe], [row_spec(LANES)],
                jax.ShapeDtypeStruct((rows, MLA_HEADS * hq), BF16), pl.BlockSpec((bm, 2 * bnk), lambda m, n: (m, n)),
                _epi_mla_k, all_tiles, mod_row, "mla_k")
            v = _norm_matmul(
                qc, 1, kv_rank, mla_kv_norm[j], None, None, wv.astype(BF16), bnk,
                [], [],
                jax.ShapeDtypeStruct((rows, MLA_HEADS * MLA_V_DIM), BF16), pl.BlockSpec((bm, bnk), lambda m, n: (m, n)),
                _epi_plain, all_tiles, mod_row, "mla_v")
            out_cols = MLA_HEADS * MLA_V_DIM
            o = _attention_t(q, k, v, batch=batch, seq=seq, ctx_len=ctx_len, heads=MLA_HEADS, groups=MLA_HEADS,
                             dk=hq, dv=MLA_V_DIM, qcol0=0, kcol0=0, vcol0=0, bq=min(2048, seq), bk=512, chunk=256,
                             out_cols=out_cols, name="mla_attention")
            if not last:
                o = _attention(q, k, v, o, batch=batch, groups=MLA_HEADS, rep=1, dk=hq, dv=MLA_V_DIM,
                               qcol0=0, kcol0=0, vcol0=0, q_blk0=lat_rows // ctx_len, n_q=1, bq=ctx_len,
                               kv_blk0=lat_rows // ctx_len, kv_len=ctx_len, bk=ctx_len,
                               ctx_blk0=None, ctx_len=None, out_cols=out_cols, name="mla_ctx_attention",
                               log2_domain=True)
            wo = mla_wo[j]
        else:
            scale = GQA_HEAD_DIM ** -0.5 * LOG2_E
            rep = GQA_HEADS // GQA_KV_HEADS
            nq_cols = GQA_HEADS * GQA_HEAD_DIM
            nk_cols = GQA_KV_HEADS * GQA_HEAD_DIM
            bn = nk_cols
            gain = jnp.concatenate([jnp.tile(gqa_q_norm[j], GQA_HEADS) * scale, jnp.tile(gqa_k_norm[j], GQA_KV_HEADS),
                                    jnp.ones((nk_cols,), F32)]).reshape(1, -1)
            qkv = _norm_matmul(
                xs, 0, d, mix_norm[i], mod, (1, 0), gqa_wqkv[j].astype(BF16), bn,
                [gain, cos_c, sa_c, sb_c], [pl.BlockSpec((1, bn), lambda m, n: (0, n))] + rope_specs,
                jax.ShapeDtypeStruct((rows, nq_cols + 2 * nk_cols), BF16), pl.BlockSpec((bm, bn), lambda m, n: (m, n)),
                functools.partial(_epi_gqa, n_qk_tiles=(nq_cols + nk_cols) // bn), all_tiles, mod_row, "gqa_qkv")
            dh = GQA_HEAD_DIM
            o = _attention_t(qkv, qkv, qkv, batch=batch, seq=seq, ctx_len=ctx_len, heads=GQA_HEADS,
                             groups=GQA_KV_HEADS, dk=dh, dv=dh, qcol0=0, kcol0=GQA_HEADS,
                             vcol0=GQA_HEADS + GQA_KV_HEADS, bq=min(512, seq), bk=512, chunk=256,
                             out_cols=nq_cols, name="gqa_attention")
            if not last:
                o = _attention(qkv, qkv, qkv, o, batch=batch, groups=GQA_KV_HEADS, rep=rep, dk=dh, dv=dh,
                               qcol0=0, kcol0=GQA_HEADS, vcol0=GQA_HEADS + GQA_KV_HEADS,
                               q_blk0=lat_rows // ctx_len, n_q=1, bq=ctx_len,
                               kv_blk0=lat_rows // ctx_len, kv_len=ctx_len, bk=ctx_len,
                               ctx_blk0=None, ctx_len=None, out_cols=nq_cols, name="gqa_ctx_attention",
                               log2_domain=True)
            wo = gqa_wo[j]

        xs = _matmul_residual(o, wo.astype(BF16), xs, mod, 2, d // 2, n_tiles, mod_row, "attn_out")
        a = _ffn_up(xs, ffn_norm[i], mod, ffn_up[i].astype(BF16), ffn_conv_w[i], ffn_conv_b[i], 512,
                    n_tiles, mod_row, lat_rows, seq, ctx_len, "ffn_up")
        xs = _matmul_residual(a, ffn_down[i].astype(BF16), xs, mod, 5, d // 4, n_tiles, mod_row, "ffn_down")

    out = _final_norm(xs, final_norm, lat_tiles)
    return out.reshape(batch, seq, d)
```

```python
import functools

import jax
import jax.numpy as jnp
from jax import lax
from jax.experimental import pallas as pl
from jax.experimental.pallas import tpu as pltpu

F32 = jnp.float32
BF16 = jnp.bfloat16

GRID_W = 64
NA_HEADS = 16
NA_WIN_ROWS = 8
NA_WIN_COLS = 16
MLA_HEADS = 16
MLA_NOPE_DIM = 128
MLA_ROPE_DIM = 64
MLA_V_DIM = 128
GQA_HEADS = 16
GQA_KV_HEADS = 4
GQA_HEAD_DIM = 128
CONV_WIDTH = 3
ROPE_THETA = 10000.0
NORM_EPS = 1e-6
N_MIXERS = 3
LOG2_E = 1.4426950408889634

LANES = 128
ROW_TILE = 512
HALO = 8
ONES_ROWS = 16
FFN_SUBTILE = 512
FLASH_UNROLL = 2
FLASH_LEAD = 2
NA_TILE_ROWS = 4
MASK_VALUE = -1e30
VMEM_LIMIT = 56 * 1024 * 1024


def _params(*sem):
    return pltpu.CompilerParams(dimension_semantics=sem, vmem_limit_bytes=VMEM_LIMIT)


def _rms(x, g):
    y = x * lax.rsqrt(jnp.mean(x * x, axis=-1, keepdims=True) + NORM_EPS)
    return y * g


def _rope(y, cos, sin_a, sin_b, quarter):
    return (y * cos + pltpu.roll(y, LANES - quarter, 1) * sin_a
            + pltpu.roll(y, quarter, 1) * sin_b)


def _mod_kernel(c_ref, w_ref, b_ref, o_ref):
    s = jax.nn.silu(c_ref[...])
    o_ref[0] = jnp.dot(s.astype(BF16), w_ref[0].astype(BF16), preferred_element_type=F32) + b_ref[0]


def _modulation(c_rows, ada_w, ada_b):
    depth, d, n = ada_w.shape
    bn = n // 8
    return pl.pallas_call(
        _mod_kernel,
        grid=(depth, n // bn),
        in_specs=[pl.BlockSpec((8, d), lambda l, j: (0, 0)),
                  pl.BlockSpec((1, d, bn), lambda l, j: (l, 0, j)),
                  pl.BlockSpec((1, 1, bn), lambda l, j: (l, 0, j))],
        out_specs=pl.BlockSpec((1, 8, bn), lambda l, j: (l, 0, j)),
        out_shape=jax.ShapeDtypeStruct((depth, 8, n), F32),
        compiler_params=_params("arbitrary", "arbitrary"),
        name="adaln_mod",
    )(c_rows, ada_w, ada_b.reshape(depth, 1, n))


def _norm_matmul_kernel(*refs, has_mod, n_extra, epilogue):
    if has_mod:
        x_ref, g_ref, sc_ref, sh_ref, w_ref = refs[:5]
        rest = refs[5:]
    else:
        x_ref, g_ref, w_ref = refs[:3]
        rest = refs[3:]
    extras, outs, h_sc = rest[:n_extra], rest[n_extra:-1], rest[-1]
    n = pl.program_id(1)

    @pl.when(n == 0)
    def _():
        h = _rms(x_ref[...], g_ref[...])
        if has_mod:
            h = h * (1.0 + sc_ref[0]) + sh_ref[0]
        h_sc[...] = h.astype(BF16)

    acc = jnp.dot(h_sc[...], w_ref[...], preferred_element_type=F32)
    epilogue(acc, n, extras, outs)


def _norm_matmul(x, xcol, kdim, g, mod, mod_chunks, w, bn, extras, extra_specs, out_shapes, out_specs,
                 epilogue, n_row_tiles, mod_row, name):
    bm = ROW_TILE
    n_col = w.shape[1] // bn
    has_mod = mod is not None
    in_specs = [pl.BlockSpec((bm, kdim), lambda m, n: (m, xcol)),
                pl.BlockSpec((1, kdim), lambda m, n: (0, 0))]
    args = [x, g.reshape(1, kdim)]
    if has_mod:
        sc_chunk, sh_chunk = mod_chunks
        in_specs += [pl.BlockSpec((1, 1, kdim), lambda m, n: (mod_row(m), 0, sc_chunk)),
                     pl.BlockSpec((1, 1, kdim), lambda m, n: (mod_row(m), 0, sh_chunk))]
        args += [mod, mod]
    in_specs.append(pl.BlockSpec((kdim, bn), lambda m, n: (0, n)))
    args.append(w)
    in_specs += extra_specs
    args += extras
    kern = functools.partial(_norm_matmul_kernel, has_mod=has_mod, n_extra=len(extras), epilogue=epilogue)
    return pl.pallas_call(
        kern,
        grid=(n_row_tiles, n_col),
        in_specs=in_specs,
        out_specs=out_specs,
        out_shape=out_shapes,
        scratch_shapes=[pltpu.VMEM((bm, kdim), BF16)],
        compiler_params=_params("arbitrary", "arbitrary"),
        name=name,
    )(*args)


def _epi_colscale(acc, n, extras, outs):
    (cs_ref,), (o_ref,) = extras, outs
    o_ref[...] = (acc * cs_ref[...]).astype(o_ref.dtype)


def _epi_plain(acc, n, extras, outs):
    (o_ref,) = outs
    o_ref[...] = acc.astype(o_ref.dtype)


def _epi_mla_a(acc, n, extras, outs):
    cos_ref, sa_ref, sb_ref = extras
    qc_ref, kpe_ref = outs
    wide = qc_ref.shape[1]
    qc_ref[...] = acc[:, :wide]
    kpe = _rope(acc[:, wide:], cos_ref[...], sa_ref[...], sb_ref[...], MLA_ROPE_DIM // 4)
    kpe_ref[...] = kpe.astype(kpe_ref.dtype)


def _epi_mla_q(acc, n, extras, outs, *, scale):
    cos_ref, sa_ref, sb_ref = extras
    (o_ref,) = outs
    for c in range(acc.shape[1] // LANES):
        y = acc[:, c * LANES:(c + 1) * LANES]
        if c % 2 == 1:
            y = _rope(y, cos_ref[...], sa_ref[...], sb_ref[...], MLA_ROPE_DIM // 4)
        o_ref[:, c * LANES:(c + 1) * LANES] = (y * scale).astype(o_ref.dtype)


def _epi_mla_k(acc, n, extras, outs):
    (kpe_ref,), (o_ref,) = extras, outs
    kpe = kpe_ref[...]
    for c in range(acc.shape[1] // LANES):
        o_ref[:, (2 * c) * LANES:(2 * c + 1) * LANES] = acc[:, c * LANES:(c + 1) * LANES].astype(o_ref.dtype)
        o_ref[:, (2 * c + 1) * LANES:(2 * c + 2) * LANES] = kpe


def _epi_gqa(acc, n, extras, outs, *, n_qk_tiles):
    gain_ref, cos_ref, sa_ref, sb_ref = extras
    (o_ref,) = outs

    @pl.when(n < n_qk_tiles)
    def _():
        for c in range(acc.shape[1] // LANES):
            y = _rms(acc[:, c * LANES:(c + 1) * LANES], gain_ref[:, c * LANES:(c + 1) * LANES])
            y = _rope(y, cos_ref[...], sa_ref[...], sb_ref[...], GQA_HEAD_DIM // 4)
            o_ref[:, c * LANES:(c + 1) * LANES] = y.astype(o_ref.dtype)

    @pl.when(n >= n_qk_tiles)
    def _():
        o_ref[...] = acc.astype(o_ref.dtype)


def _mm_res_kernel(a_ref, w_ref, x_ref, g_ref, o_ref):
    acc = jnp.dot(a_ref[...], w_ref[...], preferred_element_type=F32)
    o_ref[...] = x_ref[...] + g_ref[0] * acc


def _matmul_residual(a, w, x, mod, gate_chunk, bn, n_row_tiles, mod_row, name):
    bm = ROW_TILE
    kdim, n_out = w.shape
    assert n_out == x.shape[1] and n_out % bn == 0
    per_chunk = n_out // bn
    return pl.pallas_call(
        _mm_res_kernel,
        grid=(n_row_tiles, per_chunk),
        in_specs=[pl.BlockSpec((bm, kdim), lambda m, n: (m, 0)),
                  pl.BlockSpec((kdim, bn), lambda m, n: (0, n)),
                  pl.BlockSpec((bm, bn), lambda m, n: (m, n)),
                  pl.BlockSpec((1, 1, bn), lambda m, n: (mod_row(m), 0, gate_chunk * per_chunk + n))],
        out_specs=pl.BlockSpec((bm, bn), lambda m, n: (m, n)),
        out_shape=jax.ShapeDtypeStruct(x.shape, x.dtype),
        input_output_aliases={2: 0},
        compiler_params=_params("arbitrary", "arbitrary"),
        name=name,
    )(a, w, x, mod)


def _ffn_up_kernel(x_ref, xp_ref, xn_ref, g_ref, sc_ref, sh_ref, wg_ref, wv_ref, cw_ref, cb_ref, o_ref, h_sc,
                   *, lat_rows, lat_seq, ctx_seq):
    bm = x_ref.shape[0]
    m = pl.program_id(0)

    @pl.when(pl.program_id(1) == 0)
    def _():
        def nm(x):
            return (_rms(x, g_ref[...]) * (1.0 + sc_ref[0]) + sh_ref[0]).astype(BF16)
        h_sc[0:bm, :] = nm(x_ref[...])
        h_sc[bm:bm + 2 * HALO, :] = nm(jnp.concatenate([xn_ref[...], xp_ref[...]], axis=0))

    ext = bm + 2 * HALO
    row = m * bm + lax.broadcasted_iota(jnp.int32, (bm, 1), 0)
    in_ctx = row >= lat_rows
    pos = jnp.where(in_ctx, (row - lat_rows) & (ctx_seq - 1), row & (lat_seq - 1))
    seq = jnp.where(in_ctx, ctx_seq, lat_seq)
    is_first = pos == 0
    is_last = pos == seq - 1
    for c0 in range(0, o_ref.shape[1], FFN_SUBTILE):
        cs = slice(c0, c0 + FFN_SUBTILE)
        gate = jnp.dot(h_sc[...], wg_ref[:, cs], preferred_element_type=F32)
        val = jnp.dot(h_sc[0:bm, :], wv_ref[:, cs], preferred_element_type=F32)
        g_prev = pltpu.roll(gate, 1, 0)[0:bm]
        g_next = pltpu.roll(gate, ext - 1, 0)[0:bm]
        g_cur = gate[0:bm]
        g_prev = jnp.where(is_first, 0.0, g_prev)
        g_next = jnp.where(is_last, 0.0, g_next)
        cw = cw_ref[:, cs]
        z = cb_ref[:, cs] + g_prev * cw[0:1] + g_cur * cw[1:2] + g_next * cw[2:3]
        o_ref[:, cs] = (jax.nn.silu(z) * val).astype(o_ref.dtype)


def _ffn_up(x, g, mod, w_up, conv_w, conv_b, bn, n_row_tiles, mod_row, lat_rows, lat_seq, ctx_seq, name):
    bm = ROW_TILE
    rows, d = x.shape
    f = w_up.shape[1] // 2
    n_col = f // bn
    last_halo = rows // HALO - 1
    per = bm // HALO
    kern = functools.partial(_ffn_up_kernel, lat_rows=lat_rows, lat_seq=lat_seq, ctx_seq=ctx_seq)
    return pl.pallas_call(
        kern,
        grid=(n_row_tiles, n_col),
        in_specs=[pl.BlockSpec((bm, d), lambda m, n: (m, 0)),
                  pl.BlockSpec((HALO, d), lambda m, n: (jnp.maximum(m * per - 1, 0), 0)),
                  pl.BlockSpec((HALO, d), lambda m, n: (jnp.minimum((m + 1) * per, last_halo), 0)),
                  pl.BlockSpec((1, d), lambda m, n: (0, 0)),
                  pl.BlockSpec((1, 1, d), lambda m, n: (mod_row(m), 0, 4)),
                  pl.BlockSpec((1, 1, d), lambda m, n: (mod_row(m), 0, 3)),
                  pl.BlockSpec((d, bn), lambda m, n: (0, n)),
                  pl.BlockSpec((d, bn), lambda m, n: (0, n_col + n)),
                  pl.BlockSpec((CONV_WIDTH, bn), lambda m, n: (0, n)),
                  pl.BlockSpec((1, bn), lambda m, n: (0, n))],
        out_specs=pl.BlockSpec((bm, bn), lambda m, n: (m, n)),
        out_shape=jax.ShapeDtypeStruct((rows, f), BF16),
        scratch_shapes=[pltpu.VMEM((bm + 2 * HALO, d), BF16)],
        compiler_params=_params("arbitrary", "arbitrary"),
        name=name,
    )(x, x, x, g.reshape(1, d), mod, mod, w_up, w_up, conv_w, conv_b.reshape(1, f))


def _flash_t_kernel(qt_ref, k_ref, vt_ref, kc_ref, vct_ref, o_ref, m_sc, acc_sc, s_sc, *, rep, bk, n_kv, chunk,
                    unroll, lead):
    bq = qt_ref.shape[3]
    dv = vt_ref.shape[3] - ONES_ROWS
    n_chunks = rep * bq // chunk

    def scores(k, c):
        r, off = divmod(c * chunk, bq)
        return jnp.dot(k, qt_ref[0, r, :, off:off + chunk], preferred_element_type=F32)

    def softmax_pv(s, vt, c, is_first):
        cs = slice(c * chunk, (c + 1) * chunk)
        mx = jnp.max(s, axis=0, keepdims=True)
        if is_first:
            m_new = mx
            p = jnp.exp2(s - m_new)
            acc_sc[:, cs] = jnp.dot(vt, p.astype(vt.dtype), preferred_element_type=F32)
        else:
            m_prev = m_sc[:, cs]
            m_new = jnp.maximum(m_prev, mx)
            alpha = jnp.exp2(m_prev - m_new)
            p = jnp.exp2(s - m_new)
            acc_sc[:, cs] = alpha * acc_sc[:, cs] + jnp.dot(vt, p.astype(vt.dtype), preferred_element_type=F32)
        m_sc[:, cs] = m_new

    kc = kc_ref[...]
    k0 = k_ref[0:bk, :]
    n_ctx = kc_ref.shape[0]
    assert n_ctx <= bk
    for c in range(n_chunks):
        s_sc[1, c, 0:n_ctx, :] = scores(kc, c)
    for c in range(n_chunks):
        s_sc[0, c] = scores(k0, c)
        softmax_pv(s_sc[1, c, 0:n_ctx, :], vct_ref[0, 0], c, True)

    def body(i, carry):
        def issue_scores(t):
            u, c = divmod(t, n_chunks)
            j_next = jnp.minimum(unroll * i + u + 1, n_kv - 1)
            off = pl.multiple_of(j_next * bk, bk)
            s_sc[(u + 1) % 2, c] = scores(k_ref[pl.ds(off, bk), :], c)

        def consume(t):
            u, c = divmod(t, n_chunks)
            softmax_pv(s_sc[u % 2, c], vt_ref[0, 0, unroll * i + u], c, False)

        n_tasks = unroll * n_chunks
        for t in range(n_tasks + lead):
            if t < n_tasks:
                issue_scores(t)
            if t >= lead:
                consume(t - lead)
        return carry

    assert unroll % 2 == 0 and n_kv % unroll == 0 and lead < n_chunks
    lax.fori_loop(0, n_kv // unroll, body, 0)
    for r in range(rep):
        ot = acc_sc[0:dv, r * bq:(r + 1) * bq] / acc_sc[dv:dv + 1, r * bq:(r + 1) * bq]
        o_ref[:, r * dv:(r + 1) * dv] = ot.T.astype(o_ref.dtype)


def _attention_t(q_arr, k_arr, v_arr, *, batch, seq, ctx_len, heads, groups, dk, dv, qcol0, kcol0, vcol0,
                 bq, bk, chunk, out_cols, name):
    rows = q_arr.shape[0]
    lat_rows = batch * seq
    rep = heads // groups
    n_q, n_kv = seq // bq, seq // bk
    ctx_blk0 = lat_rows // ctx_len
    qt = q_arr[:lat_rows, qcol0 * dk:(qcol0 + heads) * dk].reshape(batch, seq, heads, dk).transpose(0, 2, 3, 1)
    v_lat = v_arr[:lat_rows, vcol0 * dv:(vcol0 + groups) * dv]
    vt = v_lat.reshape(batch, n_kv, bk, groups, dv).transpose(0, 3, 1, 4, 2)
    vt = jnp.concatenate([vt, jnp.ones(vt.shape[:3] + (ONES_ROWS, bk), vt.dtype)], axis=3)
    v_ctx = v_arr[lat_rows:, vcol0 * dv:(vcol0 + groups) * dv]
    vct = v_ctx.reshape(batch, ctx_len, groups, dv).transpose(0, 2, 3, 1)
    vct = jnp.concatenate([vct, jnp.ones(vct.shape[:2] + (ONES_ROWS, ctx_len), vct.dtype)], axis=2)
    dve = dv + ONES_ROWS
    kern = functools.partial(_flash_t_kernel, rep=rep, bk=bk, n_kv=n_kv, chunk=chunk,
                             unroll=min(FLASH_UNROLL, n_kv), lead=FLASH_LEAD)
    return pl.pallas_call(
        kern,
        grid=(batch, groups, n_q),
        in_specs=[pl.BlockSpec((1, rep, dk, bq), lambda b, g, i: (b, g, 0, i)),
                  pl.BlockSpec((seq, dk), lambda b, g, i: (b, kcol0 + g)),
                  pl.BlockSpec((1, 1, n_kv, dve, bk), lambda b, g, i: (b, g, 0, 0, 0)),
                  pl.BlockSpec((ctx_len, dk), lambda b, g, i: (ctx_blk0 + b, kcol0 + g)),
                  pl.BlockSpec((1, 1, dve, ctx_len), lambda b, g, i: (b, g, 0, 0))],
        out_specs=pl.BlockSpec((bq, rep * dv), lambda b, g, i: (b * n_q + i, g)),
        out_shape=jax.ShapeDtypeStruct((rows, out_cols), BF16),
        scratch_shapes=[pltpu.VMEM((1, rep * bq), F32), pltpu.VMEM((dve, rep * bq), F32),
                        pltpu.VMEM((2, rep * bq // chunk, bk, chunk), F32)],
        compiler_params=_params("arbitrary", "arbitrary", "arbitrary"),
        name=name,
    )(qt, k_arr, vt, k_arr, vct)


def _flash_kernel(*refs, rep, dk, dv, bk, n_kv, has_ctx, log2_domain):
    if has_ctx:
        q_ref, k_ref, v_ref, kc_ref, vc_ref, o_ref, m_sc, l_sc, acc_sc = refs
    else:
        q_ref, k_ref, v_ref, o_ref, m_sc, l_sc, acc_sc = refs
    ex = jnp.exp2 if log2_domain else jnp.exp
    bq = q_ref.shape[0]
    if rep > 1:
        q = jnp.concatenate([q_ref[:, r * dk:(r + 1) * dk] for r in range(rep)], axis=0)
    else:
        q = q_ref[...]

    def scores(k):
        return lax.dot_general(q, k, (((1,), (1,)), ((), ())), preferred_element_type=F32)

    def first(k, v):
        s = scores(k)
        m = jnp.max(s, axis=-1, keepdims=True)
        p = ex(s - m)
        m_sc[...] = m
        l_sc[...] = jnp.sum(p, axis=-1, keepdims=True)
        acc_sc[...] = jnp.dot(p.astype(v.dtype), v, preferred_element_type=F32)

    def update(k, v):
        s = scores(k)
        m_prev = m_sc[...]
        m_new = jnp.maximum(m_prev, jnp.max(s, axis=-1, keepdims=True))
        alpha = ex(m_prev - m_new)
        p = ex(s - m_new)
        l_sc[...] = alpha * l_sc[...] + jnp.sum(p, axis=-1, keepdims=True)
        acc_sc[...] = alpha * acc_sc[...] + jnp.dot(p.astype(v.dtype), v, preferred_element_type=F32)
        m_sc[...] = m_new

    if has_ctx:
        first(kc_ref[...], vc_ref[...])
        start = 0
    else:
        first(k_ref[0:bk, :], v_ref[0:bk, :])
        start = 1

    def body(j, carry):
        off = pl.multiple_of(j * bk, bk)
        update(k_ref[pl.ds(off, bk), :], v_ref[pl.ds(off, bk), :])
        return carry

    lax.fori_loop(start, n_kv, body, 0)
    o = acc_sc[...] / l_sc[...]
    for r in range(rep):
        o_ref[:, r * dv:(r + 1) * dv] = o[r * bq:(r + 1) * bq].astype(o_ref.dtype)


def _attention(q_arr, k_arr, v_arr, o_prev, *, batch, groups, rep, dk, dv, qcol0, kcol0, vcol0,
               q_blk0, n_q, bq, kv_blk0, kv_len, bk, ctx_blk0, ctx_len, out_cols, name, log2_domain):
    has_ctx = ctx_blk0 is not None
    rows = q_arr.shape[0]
    in_specs = [pl.BlockSpec((bq, rep * dk), lambda b, g, i: (q_blk0 + b * n_q + i, qcol0 + g)),
                pl.BlockSpec((kv_len, dk), lambda b, g, i: (kv_blk0 + b, kcol0 + g)),
                pl.BlockSpec((kv_len, dv), lambda b, g, i: (kv_blk0 + b, vcol0 + g))]
    args = [q_arr, k_arr, v_arr]
    if has_ctx:
        in_specs += [pl.BlockSpec((ctx_len, dk), lambda b, g, i: (ctx_blk0 + b, kcol0 + g)),
                     pl.BlockSpec((ctx_len, dv), lambda b, g, i: (ctx_blk0 + b, vcol0 + g))]
        args += [k_arr, v_arr]
    aliases = {}
    if o_prev is not None:
        in_specs.append(pl.BlockSpec(memory_space=pl.ANY))
        args.append(o_prev)
        aliases = {len(args) - 1: 0}
    kern = functools.partial(_flash_kernel, rep=rep, dk=dk, dv=dv, bk=bk, n_kv=kv_len // bk, has_ctx=has_ctx,
                             log2_domain=log2_domain)
    if o_prev is not None:
        inner = kern

        def kern(*refs):
            n_in = 5 if has_ctx else 3
            return inner(*refs[:n_in], *refs[n_in + 1:])
    return pl.pallas_call(
        kern,
        grid=(batch, groups, n_q),
        in_specs=in_specs,
        out_specs=pl.BlockSpec((bq, rep * dv), lambda b, g, i: (q_blk0 + b * n_q + i, g)),
        out_shape=jax.ShapeDtypeStruct((rows, out_cols), BF16),
        scratch_shapes=[pltpu.VMEM((rep * bq, 1), F32), pltpu.VMEM((rep * bq, 1), F32),
                        pltpu.VMEM((rep * bq, dv), F32)],
        input_output_aliases=aliases,
        compiler_params=_params("arbitrary", "arbitrary", "arbitrary"),
        name=name,
    )(*args)


def _rpb_table_kernel(rpb_ref, o_ref, *, n_tiles, grid_rows):
    var = pl.program_id(0)
    h = pl.program_id(1)
    t_rep = jnp.where(var == 0, 0, jnp.where(var == 1, 1, n_tiles - 1))
    qc = lax.broadcasted_iota(jnp.int32, (GRID_W, GRID_W), 0)
    kc = lax.broadcasted_iota(jnp.int32, (GRID_W, GRID_W), 1)
    dc = kc - qc
    cstart = jnp.clip(qc - NA_WIN_COLS // 2, 0, GRID_W - NA_WIN_COLS)
    valid_c = (kc >= cstart) & (kc < cstart + NA_WIN_COLS)
    n_dc = 2 * NA_WIN_COLS - 1
    masked = jnp.full((GRID_W, GRID_W), MASK_VALUE, F32)
    toeplitz = {}
    for dr in range(-(NA_WIN_ROWS - 1), NA_WIN_ROWS):
        acc = jnp.zeros((GRID_W, GRID_W), F32)
        for b in range(n_dc):
            bias = rpb_ref[h, (dr + NA_WIN_ROWS - 1) * n_dc + b] * LOG2_E
            acc = jnp.where(dc == b - (NA_WIN_COLS - 1), bias, acc)
        toeplitz[dr] = jnp.where(valid_c, acc, MASK_VALUE)
    for ql in range(NA_TILE_ROWS):
        qr = NA_TILE_ROWS * t_rep + ql
        rstart = jnp.clip(qr - NA_WIN_ROWS // 2, 0, grid_rows - NA_WIN_ROWS)
        for kl in range(3 * NA_TILE_ROWS):
            kr = NA_TILE_ROWS * (t_rep - 1) + kl
            dr = kl - NA_TILE_ROWS - ql
            if abs(dr) > NA_WIN_ROWS - 1:
                blk = masked
            else:
                ok = ((kr >= rstart) & (kr < rstart + NA_WIN_ROWS)).astype(F32)
                blk = toeplitz[dr] * ok + MASK_VALUE * (1.0 - ok)
            o_ref[0, 0, ql * GRID_W:(ql + 1) * GRID_W, kl * GRID_W:(kl + 1) * GRID_W] = blk


def _rpb_table(rpb, n_tiles, grid_rows):
    heads = rpb.shape[0]
    tq = NA_TILE_ROWS * GRID_W
    kern = functools.partial(_rpb_table_kernel, n_tiles=n_tiles, grid_rows=grid_rows)
    return pl.pallas_call(
        kern,
        grid=(3, heads),
        in_specs=[pl.BlockSpec(memory_space=pltpu.SMEM)],
        out_specs=pl.BlockSpec((1, 1, tq, 3 * tq), lambda v, h: (v, h, 0, 0)),
        out_shape=jax.ShapeDtypeStruct((3, heads, tq, 3 * tq), F32),
        compiler_params=_params("arbitrary", "arbitrary"),
        name="na_rpb_table",
    )(rpb.reshape(heads, -1))


def _na_kernel(q_ref, kp_ref, kc_ref, kn_ref, kx_ref, vp_ref, vc_ref, vn_ref, vx_ref, tab_ref, o_ref, *, heads, dh):
    tq = q_ref.shape[0]
    k_refs = (kp_ref, kc_ref, kn_ref, kx_ref)
    v_refs = (vp_ref, vc_ref, vn_ref, vx_ref)
    ones = jnp.ones((tq, dh), BF16)

    def scores(h):
        sl = slice(h * dh, (h + 1) * dh)
        q = q_ref[:, sl]
        return [lax.dot_general(q, k_ref[:, sl], (((1,), (1,)), ((), ())), preferred_element_type=F32)
                for k_ref in k_refs]

    s_next = scores(0)
    for h in range(heads):
        sl = slice(h * dh, (h + 1) * dh)
        s = s_next
        if h + 1 < heads:
            s_next = scores(h + 1)
        s = [s[j] + tab_ref[0, h, :, j * tq:(j + 1) * tq] for j in range(3)] + [s[3]]
        m = jnp.max(jnp.maximum(jnp.maximum(s[0], s[1]), jnp.maximum(s[2], s[3])), axis=-1, keepdims=True)
        o = None
        for sj, v_ref in zip(s, v_refs):
            p = jnp.exp2(sj - m).astype(BF16)
            part = jnp.dot(p, jnp.concatenate([v_ref[:, sl], ones], axis=1), preferred_element_type=F32)
            o = part if o is None else o + part
        o_ref[:, sl] = (o[:, 0:dh] / o[:, dh:2 * dh]).astype(o_ref.dtype)


def _neighborhood_attention(qkv, table, *, batch, seq, ctx_len, heads, dh, hb):
    rows = qkv.shape[0]
    tq = NA_TILE_ROWS * GRID_W
    assert ctx_len == tq and seq % tq == 0
    n_tiles = seq // tq
    n_hg = heads // hb
    ctx_blk0 = batch * seq // ctx_len
    wb = hb * dh

    def var(t):
        return jnp.where(t == 0, 0, jnp.where(t == n_tiles - 1, 2, 1))

    def spec(col0, shift):
        if shift is None:
            return pl.BlockSpec((tq, wb), lambda b, g, t: (ctx_blk0 + b, col0 + g))
        return pl.BlockSpec((tq, wb), lambda b, g, t: (b * n_tiles + jnp.clip(t + shift, 0, n_tiles - 1), col0 + g))

    kern = functools.partial(_na_kernel, heads=hb, dh=dh)
    return pl.pallas_call(
        kern,
        grid=(batch, n_hg, n_tiles),
        in_specs=[spec(0, 0),
                  spec(n_hg, -1), spec(n_hg, 0), spec(n_hg, 1), spec(n_hg, None),
                  spec(2 * n_hg, -1), spec(2 * n_hg, 0), spec(2 * n_hg, 1), spec(2 * n_hg, None),
                  pl.BlockSpec((1, hb, tq, 3 * tq), lambda b, g, t: (var(t), g, 0, 0))],
        out_specs=pl.BlockSpec((tq, wb), lambda b, g, t: (b * n_tiles + t, g)),
        out_shape=jax.ShapeDtypeStruct((rows, heads * dh), BF16),
        compiler_params=_params("arbitrary", "arbitrary", "arbitrary"),
        name="na_attention",
    )(qkv, qkv, qkv, qkv, qkv, qkv, qkv, qkv, qkv, table)


def _final_norm_kernel(x_ref, g_ref, o_ref):
    o_ref[...] = _rms(x_ref[...], g_ref[...])


def _final_norm(x, g, n_row_tiles):
    bm = ROW_TILE
    d = x.shape[1]
    return pl.pallas_call(
        _final_norm_kernel,
        grid=(n_row_tiles,),
        in_specs=[pl.BlockSpec((bm, d), lambda m: (m, 0)), pl.BlockSpec((1, d), lambda m: (0, 0))],
        out_specs=pl.BlockSpec((bm, d), lambda m: (m, 0)),
        out_shape=jax.ShapeDtypeStruct((n_row_tiles * bm, d), F32),
        compiler_params=_params("arbitrary"),
        name="final_norm",
    )(x, g.reshape(1, d))


def _rope_tables(n_tok, rot_dim, batch, n_ctx_rows):
    quarter = rot_dim // 4
    t = jnp.arange(n_tok)
    row = (t // GRID_W).astype(F32)
    col = (t % GRID_W).astype(F32)
    inv = ROPE_THETA ** (-jnp.arange(quarter, dtype=F32) / quarter)
    ar = row[:, None] * inv
    ac = col[:, None] * inv
    ang = jnp.concatenate([ar, ar, ac, ac], axis=-1)
    cos, sin = jnp.cos(ang), jnp.sin(ang)
    first = (jnp.arange(rot_dim) % (2 * quarter)) < quarter
    sin_a = jnp.where(first, -sin, 0.0)
    sin_b = jnp.where(first, 0.0, sin)
    pad = LANES - rot_dim

    def full(tab, fill):
        tab = jnp.pad(tab, ((0, 0), (0, pad)), constant_values=fill)
        tab = jnp.tile(tab, (batch, 1))
        return jnp.concatenate([tab, jnp.full((n_ctx_rows, LANES), fill, F32)], axis=0)

    return full(cos, 1.0), full(sin_a, 0.0), full(sin_b, 0.0)


def kernel(x, c, ctx, c_ctx, ada_w, ada_b, mix_norm, ffn_norm, ffn_up, ffn_conv_w, ffn_conv_b, ffn_down, na_wqkv, na_rpb, na_wo, mla_wq_a, mla_q_norm, mla_wq_b, mla_wkv_a, mla_kv_norm, mla_wkv_b, mla_wo, gqa_wqkv, gqa_q_norm, gqa_k_norm, gqa_wo, final_norm):
    batch, seq, d = x.shape
    ctx_len = ctx.shape[1]
    depth = ada_w.shape[0]
    ffn_dim = ffn_down.shape[1]
    bm = ROW_TILE
    lat_rows = batch * seq
    ctx_rows = batch * ctx_len
    rows = lat_rows + ctx_rows
    assert seq % bm == 0 and ctx_rows % bm == 0 and batch < 8
    assert seq & (seq - 1) == 0 and ctx_len & (ctx_len - 1) == 0
    tiles_per_batch = seq // bm
    lat_tiles = lat_rows // bm
    all_tiles = rows // bm
    grid_rows = seq // GRID_W

    def mod_row(m):
        return jnp.minimum(m // tiles_per_batch, batch)

    xs = jnp.concatenate([x.reshape(lat_rows, d), ctx.reshape(ctx_rows, d)], axis=0)
    c_rows = jnp.zeros((8, d), F32).at[:batch].set(c).at[batch].set(c_ctx)
    mod_all = _modulation(c_rows, ada_w, ada_b)

    cos_b, sa_b, sb_b = _rope_tables(seq, MLA_ROPE_DIM, batch, ctx_rows)
    cos_c, sa_c, sb_c = _rope_tables(seq, GQA_HEAD_DIM, batch, ctx_rows)

    def row_spec(width):
        return pl.BlockSpec((bm, width), lambda m, n: (m, 0))

    rope_specs = [row_spec(LANES)] * 3

    for i in range(depth):
        last = i == depth - 1
        kind, j = i % N_MIXERS, i // N_MIXERS
        mod = mod_all[i].reshape(8, 1, 6 * d)
        n_tiles = lat_tiles if last else all_tiles

        if kind == 0:
            scale = (d // NA_HEADS) ** -0.5 * LOG2_E
            colscale = jnp.concatenate([jnp.full((1, d), scale, F32), jnp.ones((1, 2 * d), F32)], axis=1)
            bn = d
            qkv = _norm_matmul(
                xs, 0, d, mix_norm[i], mod, (1, 0), na_wqkv[j].astype(BF16), bn,
                [colscale], [pl.BlockSpec((1, bn), lambda m, n: (0, n))],
                jax.ShapeDtypeStruct((rows, 3 * d), BF16), pl.BlockSpec((bm, bn), lambda m, n: (m, n)),
                _epi_colscale, all_tiles, mod_row, "na_qkv")
            table = _rpb_table(na_rpb[j], seq // (NA_TILE_ROWS * GRID_W), grid_rows)
            dh = d // NA_HEADS
            o = _neighborhood_attention(qkv, table, batch=batch, seq=seq, ctx_len=ctx_len,
                                        heads=NA_HEADS, dh=dh, hb=8)
            if not last:
                o = _attention(qkv, qkv, qkv, o, batch=batch, groups=NA_HEADS, rep=1, dk=dh, dv=dh,
                               qcol0=0, kcol0=NA_HEADS, vcol0=2 * NA_HEADS,
                               q_blk0=lat_rows // ctx_len, n_q=1, bq=ctx_len,
                               kv_blk0=lat_rows // ctx_len, kv_len=ctx_len, bk=ctx_len,
                               ctx_blk0=None, ctx_len=None, out_cols=d, name="na_ctx_attention",
                               log2_domain=True)
            wo = na_wo[j]
        elif kind == 1:
            scale = (MLA_NOPE_DIM + MLA_ROPE_DIM) ** -0.5 * LOG2_E
            q_rank = mla_wq_a.shape[2]
            kv_rank = mla_kv_norm.shape[1]
            hq = 2 * LANES
            w1 = jnp.concatenate([mla_wq_a[j], mla_wkv_a[j], jnp.zeros((d, LANES - MLA_ROPE_DIM), F32)], axis=1)
            n1 = w1.shape[1]
            qc, kpe = _norm_matmul(
                xs, 0, d, mix_norm[i], mod, (1, 0), w1.astype(BF16), n1,
                [cos_b, sa_b, sb_b], rope_specs,
                (jax.ShapeDtypeStruct((rows, q_rank + kv_rank), F32), jax.ShapeDtypeStruct((rows, LANES), BF16)),
                (pl.BlockSpec((bm, q_rank + kv_rank), lambda m, n: (m, 0)), pl.BlockSpec((bm, LANES), lambda m, n: (m, 0))),
                _epi_mla_a, all_tiles, mod_row, "mla_a")
            wqb = mla_wq_b[j].reshape(q_rank, MLA_HEADS, MLA_NOPE_DIM + MLA_ROPE_DIM)
            wqb = jnp.pad(wqb, ((0, 0), (0, 0), (0, hq - MLA_NOPE_DIM - MLA_ROPE_DIM))).reshape(q_rank, MLA_HEADS * hq)
            bn = 4 * hq
            q = _norm_matmul(
                qc, 0, q_rank, mla_q_norm[j], None, None, wqb.astype(BF16), bn,
                [cos_b, sa_b, sb_b], rope_specs,
                jax.ShapeDtypeStruct((rows, MLA_HEADS * hq), BF16), pl.BlockSpec((bm, bn), lambda m, n: (m, n)),
                functools.partial(_epi_mla_q, scale=scale), all_tiles, mod_row, "mla_q")
            wkvb = mla_wkv_b[j].reshape(kv_rank, MLA_HEADS, MLA_NOPE_DIM + MLA_V_DIM)
            wk = wkvb[:, :, :MLA_NOPE_DIM].reshape(kv_rank, MLA_HEADS * MLA_NOPE_DIM)
            wv = wkvb[:, :, MLA_NOPE_DIM:].reshape(kv_rank, MLA_HEADS * MLA_V_DIM)
            bnk = 8 * MLA_NOPE_DIM
            k = _norm_matmul(
                qc, 1, kv_rank, mla_kv_norm[j], None, None, wk.astype(BF16), bnk,
                [kpe], [row_spec(LANES)],
                jax.ShapeDtypeStruct((rows, MLA_HEADS * hq), BF16), pl.BlockSpec((bm, 2 * bnk), lambda m, n: (m, n)),
                _epi_mla_k, all_tiles, mod_row, "mla_k")
            v = _norm_matmul(
                qc, 1, kv_rank, mla_kv_norm[j], None, None, wv.astype(BF16), bnk,
                [], [],
                jax.ShapeDtypeStruct((rows, MLA_HEADS * MLA_V_DIM), BF16), pl.BlockSpec((bm, bnk), lambda m, n: (m, n)),
                _epi_plain, all_tiles, mod_row, "mla_v")
            out_cols = MLA_HEADS * MLA_V_DIM
            o = _attention_t(q, k, v, batch=batch, seq=seq, ctx_len=ctx_len, heads=MLA_HEADS, groups=MLA_HEADS,
                             dk=hq, dv=MLA_V_DIM, qcol0=0, kcol0=0, vcol0=0, bq=min(2048, seq), bk=512, chunk=256,
                             out_cols=out_cols, name="mla_attention")
            if not last:
                o = _attention(q, k, v, o, batch=batch, groups=MLA_HEADS, rep=1, dk=hq, dv=MLA_V_DIM,
                               qcol0=0, kcol0=0, vcol0=0, q_blk0=lat_rows // ctx_len, n_q=1, bq=ctx_len,
                               kv_blk0=lat_rows // ctx_len, kv_len=ctx_len, bk=ctx_len,
                               ctx_blk0=None, ctx_len=None, out_cols=out_cols, name="mla_ctx_attention",
                               log2_domain=True)
            wo = mla_wo[j]
        else:
            scale = GQA_HEAD_DIM ** -0.5 * LOG2_E
            rep = GQA_HEADS // GQA_KV_HEADS
            nq_cols = GQA_HEADS * GQA_HEAD_DIM
            nk_cols = GQA_KV_HEADS * GQA_HEAD_DIM
            bn = nk_cols
            gain = jnp.concatenate([jnp.tile(gqa_q_norm[j], GQA_HEADS) * scale, jnp.tile(gqa_k_norm[j], GQA_KV_HEADS),
                                    jnp.ones((nk_cols,), F32)]).reshape(1, -1)
            qkv = _norm_matmul(
                xs, 0, d, mix_norm[i], mod, (1, 0), gqa_wqkv[j].astype(BF16), bn,
                [gain, cos_c, sa_c, sb_c], [pl.BlockSpec((1, bn), lambda m, n: (0, n))] + rope_specs,
                jax.ShapeDtypeStruct((rows, nq_cols + 2 * nk_cols), BF16), pl.BlockSpec((bm, bn), lambda m, n: (m, n)),
                functools.partial(_epi_gqa, n_qk_tiles=(nq_cols + nk_cols) // bn), all_tiles, mod_row, "gqa_qkv")
            dh = GQA_HEAD_DIM
            o = _attention_t(qkv, qkv, qkv, batch=batch, seq=seq, ctx_len=ctx_len, heads=GQA_HEADS,
                             groups=GQA_KV_HEADS, dk=dh, dv=dh, qcol0=0, kcol0=GQA_HEADS,
                             vcol0=GQA_HEADS + GQA_KV_HEADS, bq=min(512, seq), bk=512, chunk=256,
                             out_cols=nq_cols, name="gqa_attention")
            if not last:
                o = _attention(qkv, qkv, qkv, o, batch=batch, groups=GQA_KV_HEADS, rep=rep, dk=dh, dv=dh,
                               qcol0=0, kcol0=GQA_HEADS, vcol0=GQA_HEADS + GQA_KV_HEADS,
                               q_blk0=lat_rows // ctx_len, n_q=1, bq=ctx_len,
                               kv_blk0=lat_rows // ctx_len, kv_len=ctx_len, bk=ctx_len,
                               ctx_blk0=None, ctx_len=None, out_cols=nq_cols, name="gqa_ctx_attention",
                               log2_domain=True)
            wo = gqa_wo[j]

        xs = _matmul_residual(o, wo.astype(BF16), xs, mod, 2, d // 2, n_tiles, mod_row, "attn_out")
        a = _ffn_up(xs, ffn_norm[i], mod, ffn_up[i].astype(BF16), ffn_conv_w[i], ffn_conv_b[i], 512,
                    n_tiles, mod_row, lat_rows, seq, ctx_len, "ffn_up")
        xs = _matmul_residual(a, ffn_down[i].astype(BF16), xs, mod, 5, d // 4, n_tiles, mod_row, "ffn_down")

    out = _final_norm(xs, final_norm, lat_tiles)
    return out.reshape(batch, seq, d)
```

```python
import functools

import jax
import jax.numpy as jnp
from jax import lax
from jax.experimental import pallas as pl
from jax.experimental.pallas import tpu as pltpu

F32 = jnp.float32
BF16 = jnp.bfloat16

GRID_W = 64
NA_HEADS = 16
NA_WIN_ROWS = 8
NA_WIN_COLS = 16
MLA_HEADS = 16
MLA_NOPE_DIM = 128
MLA_ROPE_DIM = 64
MLA_V_DIM = 128
GQA_HEADS = 16
GQA_KV_HEADS = 4
GQA_HEAD_DIM = 128
CONV_WIDTH = 3
ROPE_THETA = 10000.0
NORM_EPS = 1e-6
N_MIXERS = 3
LOG2_E = 1.4426950408889634

LANES = 128
ROW_TILE = 1024
HALO = 8
ONES_ROWS = 16
FFN_SUBTILE = 512
FLASH_UNROLL = 2
FLASH_LEAD = 2
NA_TILE_ROWS = 4
MASK_VALUE = -1e30
VMEM_LIMIT = 56 * 1024 * 1024


def _params(*sem):
    return pltpu.CompilerParams(dimension_semantics=sem, vmem_limit_bytes=VMEM_LIMIT)


def _rms(x, g):
    y = x * lax.rsqrt(jnp.mean(x * x, axis=-1, keepdims=True) + NORM_EPS)
    return y * g


def _rope(y, cos, sin_a, sin_b, quarter):
    return (y * cos + pltpu.roll(y, LANES - quarter, 1) * sin_a
            + pltpu.roll(y, quarter, 1) * sin_b)


def _mod_kernel(c_ref, w_ref, b_ref, o_ref):
    s = jax.nn.silu(c_ref[...])
    o_ref[0] = jnp.dot(s.astype(BF16), w_ref[0].astype(BF16), preferred_element_type=F32) + b_ref[0]


def _modulation(c_rows, ada_w, ada_b):
    depth, d, n = ada_w.shape
    bn = n // 8
    return pl.pallas_call(
        _mod_kernel,
        grid=(depth, n // bn),
        in_specs=[pl.BlockSpec((8, d), lambda l, j: (0, 0)),
                  pl.BlockSpec((1, d, bn), lambda l, j: (l, 0, j)),
                  pl.BlockSpec((1, 1, bn), lambda l, j: (l, 0, j))],
        out_specs=pl.BlockSpec((1, 8, bn), lambda l, j: (l, 0, j)),
        out_shape=jax.ShapeDtypeStruct((depth, 8, n), F32),
        compiler_params=_params("arbitrary", "arbitrary"),
        name="adaln_mod",
    )(c_rows, ada_w, ada_b.reshape(depth, 1, n))


def _norm_matmul_kernel(*refs, has_mod, n_extra, epilogue):
    if has_mod:
        x_ref, g_ref, sc_ref, sh_ref, w_ref = refs[:5]
        rest = refs[5:]
    else:
        x_ref, g_ref, w_ref = refs[:3]
        rest = refs[3:]
    extras, outs, h_sc = rest[:n_extra], rest[n_extra:-1], rest[-1]
    n = pl.program_id(1)

    @pl.when(n == 0)
    def _():
        h = _rms(x_ref[...], g_ref[...])
        if has_mod:
            h = h * (1.0 + sc_ref[0]) + sh_ref[0]
        h_sc[...] = h.astype(BF16)

    acc = jnp.dot(h_sc[...], w_ref[...], preferred_element_type=F32)
    epilogue(acc, n, extras, outs)


def _norm_matmul(x, xcol, kdim, g, mod, mod_chunks, w, bn, extras, extra_specs, out_shapes, out_specs,
                 epilogue, n_row_tiles, mod_row, name):
    bm = ROW_TILE
    n_col = w.shape[1] // bn
    has_mod = mod is not None
    in_specs = [pl.BlockSpec((bm, kdim), lambda m, n: (m, xcol)),
                pl.BlockSpec((1, kdim), lambda m, n: (0, 0))]
    args = [x, g.reshape(1, kdim)]
    if has_mod:
        sc_chunk, sh_chunk = mod_chunks
        in_specs += [pl.BlockSpec((1, 1, kdim), lambda m, n: (mod_row(m), 0, sc_chunk)),
                     pl.BlockSpec((1, 1, kdim), lambda m, n: (mod_row(m), 0, sh_chunk))]
        args += [mod, mod]
    in_specs.append(pl.BlockSpec((kdim, bn), lambda m, n: (0, n)))
    args.append(w)
    in_specs += extra_specs
    args += extras
    kern = functools.partial(_norm_matmul_kernel, has_mod=has_mod, n_extra=len(extras), epilogue=epilogue)
    return pl.pallas_call(
        kern,
        grid=(n_row_tiles, n_col),
        in_specs=in_specs,
        out_specs=out_specs,
        out_shape=out_shapes,
        scratch_shapes=[pltpu.VMEM((bm, kdim), BF16)],
        compiler_params=_params("arbitrary", "arbitrary"),
        name=name,
    )(*args)


def _epi_colscale(acc, n, extras, outs):
    (cs_ref,), (o_ref,) = extras, outs
    o_ref[...] = (acc * cs_ref[...]).astype(o_ref.dtype)


def _epi_plain(acc, n, extras, outs):
    (o_ref,) = outs
    o_ref[...] = acc.astype(o_ref.dtype)


def _epi_mla_a(acc, n, extras, outs):
    cos_ref, sa_ref, sb_ref = extras
    qc_ref, kpe_ref = outs
    wide = qc_ref.shape[1]
    qc_ref[...] = acc[:, :wide]
    kpe = _rope(acc[:, wide:], cos_ref[...], sa_ref[...], sb_ref[...], MLA_ROPE_DIM // 4)
    kpe_ref[...] = kpe.astype(kpe_ref.dtype)


def _epi_mla_q(acc, n, extras, outs, *, scale):
    cos_ref, sa_ref, sb_ref = extras
    (o_ref,) = outs
    for c in range(acc.shape[1] // LANES):
        y = acc[:, c * LANES:(c + 1) * LANES]
        if c % 2 == 1:
            y = _rope(y, cos_ref[...], sa_ref[...], sb_ref[...], MLA_ROPE_DIM // 4)
        o_ref[:, c * LANES:(c + 1) * LANES] = (y * scale).astype(o_ref.dtype)


def _epi_mla_k(acc, n, extras, outs):
    (kpe_ref,), (o_ref,) = extras, outs
    kpe = kpe_ref[...]
    for c in range(acc.shape[1] // LANES):
        o_ref[:, (2 * c) * LANES:(2 * c + 1) * LANES] = acc[:, c * LANES:(c + 1) * LANES].astype(o_ref.dtype)
        o_ref[:, (2 * c + 1) * LANES:(2 * c + 2) * LANES] = kpe


def _epi_gqa(acc, n, extras, outs, *, n_qk_tiles):
    gain_ref, cos_ref, sa_ref, sb_ref = extras
    (o_ref,) = outs

    @pl.when(n < n_qk_tiles)
    def _():
        for c in range(acc.shape[1] // LANES):
            y = _rms(acc[:, c * LANES:(c + 1) * LANES], gain_ref[:, c * LANES:(c + 1) * LANES])
            y = _rope(y, cos_ref[...], sa_ref[...], sb_ref[...], GQA_HEAD_DIM // 4)
            o_ref[:, c * LANES:(c + 1) * LANES] = y.astype(o_ref.dtype)

    @pl.when(n >= n_qk_tiles)
    def _():
        o_ref[...] = acc.astype(o_ref.dtype)


def _mm_res_kernel(a_ref, w_ref, x_ref, g_ref, o_ref):
    acc = jnp.dot(a_ref[...], w_ref[...], preferred_element_type=F32)
    o_ref[...] = x_ref[...] + g_ref[0] * acc


def _matmul_residual(a, w, x, mod, gate_chunk, bn, n_row_tiles, mod_row, name):
    bm = ROW_TILE
    kdim, n_out = w.shape
    assert n_out == x.shape[1] and n_out % bn == 0
    per_chunk = n_out // bn
    return pl.pallas_call(
        _mm_res_kernel,
        grid=(n_row_tiles, per_chunk),
        in_specs=[pl.BlockSpec((bm, kdim), lambda m, n: (m, 0)),
                  pl.BlockSpec((kdim, bn), lambda m, n: (0, n)),
                  pl.BlockSpec((bm, bn), lambda m, n: (m, n)),
                  pl.BlockSpec((1, 1, bn), lambda m, n: (mod_row(m), 0, gate_chunk * per_chunk + n))],
        out_specs=pl.BlockSpec((bm, bn), lambda m, n: (m, n)),
        out_shape=jax.ShapeDtypeStruct(x.shape, x.dtype),
        input_output_aliases={2: 0},
        compiler_params=_params("arbitrary", "arbitrary"),
        name=name,
    )(a, w, x, mod)


def _ffn_up_kernel(x_ref, xp_ref, xn_ref, g_ref, sc_ref, sh_ref, wg_ref, wv_ref, cw_ref, cb_ref, o_ref, h_sc,
                   *, lat_rows, lat_seq, ctx_seq):
    bm = x_ref.shape[0]
    m = pl.program_id(0)

    @pl.when(pl.program_id(1) == 0)
    def _():
        def nm(x):
            return (_rms(x, g_ref[...]) * (1.0 + sc_ref[0]) + sh_ref[0]).astype(BF16)
        h_sc[0:bm, :] = nm(x_ref[...])
        h_sc[bm:bm + 2 * HALO, :] = nm(jnp.concatenate([xn_ref[...], xp_ref[...]], axis=0))

    ext = bm + 2 * HALO
    row = m * bm + lax.broadcasted_iota(jnp.int32, (bm, 1), 0)
    in_ctx = row >= lat_rows
    pos = jnp.where(in_ctx, (row - lat_rows) & (ctx_seq - 1), row & (lat_seq - 1))
    seq = jnp.where(in_ctx, ctx_seq, lat_seq)
    is_first = pos == 0
    is_last = pos == seq - 1
    for c0 in range(0, o_ref.shape[1], FFN_SUBTILE):
        cs = slice(c0, c0 + FFN_SUBTILE)
        gate = jnp.dot(h_sc[...], wg_ref[:, cs], preferred_element_type=F32)
        val = jnp.dot(h_sc[0:bm, :], wv_ref[:, cs], preferred_element_type=F32)
        g_prev = pltpu.roll(gate, 1, 0)[0:bm]
        g_next = pltpu.roll(gate, ext - 1, 0)[0:bm]
        g_cur = gate[0:bm]
        g_prev = jnp.where(is_first, 0.0, g_prev)
        g_next = jnp.where(is_last, 0.0, g_next)
        cw = cw_ref[:, cs]
        z = cb_ref[:, cs] + g_prev * cw[0:1] + g_cur * cw[1:2] + g_next * cw[2:3]
        o_ref[:, cs] = (jax.nn.silu(z) * val).astype(o_ref.dtype)


def _ffn_up(x, g, mod, w_up, conv_w, conv_b, bn, n_row_tiles, mod_row, lat_rows, lat_seq, ctx_seq, name):
    bm = ROW_TILE
    rows, d = x.shape
    f = w_up.shape[1] // 2
    n_col = f // bn
    last_halo = rows // HALO - 1
    per = bm // HALO
    kern = functools.partial(_ffn_up_kernel, lat_rows=lat_rows, lat_seq=lat_seq, ctx_seq=ctx_seq)
    return pl.pallas_call(
        kern,
        grid=(n_row_tiles, n_col),
        in_specs=[pl.BlockSpec((bm, d), lambda m, n: (m, 0)),
                  pl.BlockSpec((HALO, d), lambda m, n: (jnp.maximum(m * per - 1, 0), 0)),
                  pl.BlockSpec((HALO, d), lambda m, n: (jnp.minimum((m + 1) * per, last_halo), 0)),
                  pl.BlockSpec((1, d), lambda m, n: (0, 0)),
                  pl.BlockSpec((1, 1, d), lambda m, n: (mod_row(m), 0, 4)),
                  pl.BlockSpec((1, 1, d), lambda m, n: (mod_row(m), 0, 3)),
                  pl.BlockSpec((d, bn), lambda m, n: (0, n)),
                  pl.BlockSpec((d, bn), lambda m, n: (0, n_col + n)),
                  pl.BlockSpec((CONV_WIDTH, bn), lambda m, n: (0, n)),
                  pl.BlockSpec((1, bn), lambda m, n: (0, n))],
        out_specs=pl.BlockSpec((bm, bn), lambda m, n: (m, n)),
        out_shape=jax.ShapeDtypeStruct((rows, f), BF16),
        scratch_shapes=[pltpu.VMEM((bm + 2 * HALO, d), BF16)],
        compiler_params=_params("arbitrary", "arbitrary"),
        name=name,
    )(x, x, x, g.reshape(1, d), mod, mod, w_up, w_up, conv_w, conv_b.reshape(1, f))


def _flash_t_kernel(qt_ref, k_ref, vt_ref, kc_ref, vct_ref, o_ref, m_sc, acc_sc, s_sc, *, rep, bk, n_kv, chunk,
                    unroll, lead):
    bq = qt_ref.shape[3]
    dv = vt_ref.shape[3] - ONES_ROWS
    n_chunks = rep * bq // chunk

    def scores(k, c):
        r, off = divmod(c * chunk, bq)
        return jnp.dot(k, qt_ref[0, r, :, off:off + chunk], preferred_element_type=F32)

    def softmax_pv(s, vt, c, is_first):
        cs = slice(c * chunk, (c + 1) * chunk)
        mx = jnp.max(s, axis=0, keepdims=True)
        if is_first:
            m_new = mx
            p = jnp.exp2(s - m_new)
            acc_sc[:, cs] = jnp.dot(vt, p.astype(vt.dtype), preferred_element_type=F32)
        else:
            m_prev = m_sc[:, cs]
            m_new = jnp.maximum(m_prev, mx)
            alpha = jnp.exp2(m_prev - m_new)
            p = jnp.exp2(s - m_new)
            acc_sc[:, cs] = alpha * acc_sc[:, cs] + jnp.dot(vt, p.astype(vt.dtype), preferred_element_type=F32)
        m_sc[:, cs] = m_new

    kc = kc_ref[...]
    k0 = k_ref[0:bk, :]
    n_ctx = kc_ref.shape[0]
    assert n_ctx <= bk
    for c in range(n_chunks):
        s_sc[1, c, 0:n_ctx, :] = scores(kc, c)
    for c in range(n_chunks):
        s_sc[0, c] = scores(k0, c)
        softmax_pv(s_sc[1, c, 0:n_ctx, :], vct_ref[0, 0], c, True)

    def body(i, carry):
        def issue_scores(t):
            u, c = divmod(t, n_chunks)
            j_next = jnp.minimum(unroll * i + u + 1, n_kv - 1)
            off = pl.multiple_of(j_next * bk, bk)
            s_sc[(u + 1) % 2, c] = scores(k_ref[pl.ds(off, bk), :], c)

        def consume(t):
            u, c = divmod(t, n_chunks)
            softmax_pv(s_sc[u % 2, c], vt_ref[0, 0, unroll * i + u], c, False)

        n_tasks = unroll * n_chunks
        for t in range(n_tasks + lead):
            if t < n_tasks:
                issue_scores(t)
            if t >= lead:
                consume(t - lead)
        return carry

    assert unroll % 2 == 0 and n_kv % unroll == 0 and lead < n_chunks
    lax.fori_loop(0, n_kv // unroll, body, 0)
    for r in range(rep):
        ot = acc_sc[0:dv, r * bq:(r + 1) * bq] / acc_sc[dv:dv + 1, r * bq:(r + 1) * bq]
        o_ref[:, r * dv:(r + 1) * dv] = ot.T.astype(o_ref.dtype)


def _attention_t(q_arr, k_arr, v_arr, *, batch, seq, ctx_len, heads, groups, dk, dv, qcol0, kcol0, vcol0,
                 bq, bk, chunk, out_cols, name):
    rows = q_arr.shape[0]
    lat_rows = batch * seq
    rep = heads // groups
    n_q, n_kv = seq // bq, seq // bk
    ctx_blk0 = lat_rows // ctx_len
    qt = q_arr[:lat_rows, qcol0 * dk:(qcol0 + heads) * dk].reshape(batch, seq, heads, dk).transpose(0, 2, 3, 1)
    v_lat = v_arr[:lat_rows, vcol0 * dv:(vcol0 + groups) * dv]
    vt = v_lat.reshape(batch, n_kv, bk, groups, dv).transpose(0, 3, 1, 4, 2)
    vt = jnp.concatenate([vt, jnp.ones(vt.shape[:3] + (ONES_ROWS, bk), vt.dtype)], axis=3)
    v_ctx = v_arr[lat_rows:lat_rows + batch * ctx_len, vcol0 * dv:(vcol0 + groups) * dv]
    vct = v_ctx.reshape(batch, ctx_len, groups, dv).transpose(0, 2, 3, 1)
    vct = jnp.concatenate([vct, jnp.ones(vct.shape[:2] + (ONES_ROWS, ctx_len), vct.dtype)], axis=2)
    dve = dv + ONES_ROWS
    kern = functools.partial(_flash_t_kernel, rep=rep, bk=bk, n_kv=n_kv, chunk=chunk,
                             unroll=min(FLASH_UNROLL, n_kv), lead=FLASH_LEAD)
    return pl.pallas_call(
        kern,
        grid=(batch, groups, n_q),
        in_specs=[pl.BlockSpec((1, rep, dk, bq), lambda b, g, i: (b, g, 0, i)),
                  pl.BlockSpec((seq, dk), lambda b, g, i: (b, kcol0 + g)),
                  pl.BlockSpec((1, 1, n_kv, dve, bk), lambda b, g, i: (b, g, 0, 0, 0)),
                  pl.BlockSpec((ctx_len, dk), lambda b, g, i: (ctx_blk0 + b, kcol0 + g)),
                  pl.BlockSpec((1, 1, dve, ctx_len), lambda b, g, i: (b, g, 0, 0))],
        out_specs=pl.BlockSpec((bq, rep * dv), lambda b, g, i: (b * n_q + i, g)),
        out_shape=jax.ShapeDtypeStruct((rows, out_cols), BF16),
        scratch_shapes=[pltpu.VMEM((1, rep * bq), F32), pltpu.VMEM((dve, rep * bq), F32),
                        pltpu.VMEM((2, rep * bq // chunk, bk, chunk), F32)],
        compiler_params=_params("arbitrary", "arbitrary", "arbitrary"),
        name=name,
    )(qt, k_arr, vt, k_arr, vct)


def _flash_kernel(*refs, rep, dk, dv, bk, n_kv, has_ctx, log2_domain):
    if has_ctx:
        q_ref, k_ref, v_ref, kc_ref, vc_ref, o_ref, m_sc, l_sc, acc_sc = refs
    else:
        q_ref, k_ref, v_ref, o_ref, m_sc, l_sc, acc_sc = refs
    ex = jnp.exp2 if log2_domain else jnp.exp
    bq = q_ref.shape[0]
    if rep > 1:
        q = jnp.concatenate([q_ref[:, r * dk:(r + 1) * dk] for r in range(rep)], axis=0)
    else:
        q = q_ref[...]

    def scores(k):
        return lax.dot_general(q, k, (((1,), (1,)), ((), ())), preferred_element_type=F32)

    def first(k, v):
        s = scores(k)
        m = jnp.max(s, axis=-1, keepdims=True)
        p = ex(s - m)
        m_sc[...] = m
        l_sc[...] = jnp.sum(p, axis=-1, keepdims=True)
        acc_sc[...] = jnp.dot(p.astype(v.dtype), v, preferred_element_type=F32)

    def update(k, v):
        s = scores(k)
        m_prev = m_sc[...]
        m_new = jnp.maximum(m_prev, jnp.max(s, axis=-1, keepdims=True))
        alpha = ex(m_prev - m_new)
        p = ex(s - m_new)
        l_sc[...] = alpha * l_sc[...] + jnp.sum(p, axis=-1, keepdims=True)
        acc_sc[...] = alpha * acc_sc[...] + jnp.dot(p.astype(v.dtype), v, preferred_element_type=F32)
        m_sc[...] = m_new

    if has_ctx:
        first(kc_ref[...], vc_ref[...])
        start = 0
    else:
        first(k_ref[0:bk, :], v_ref[0:bk, :])
        start = 1

    def body(j, carry):
        off = pl.multiple_of(j * bk, bk)
        update(k_ref[pl.ds(off, bk), :], v_ref[pl.ds(off, bk), :])
        return carry

    lax.fori_loop(start, n_kv, body, 0)
    o = acc_sc[...] / l_sc[...]
    for r in range(rep):
        o_ref[:, r * dv:(r + 1) * dv] = o[r * bq:(r + 1) * bq].astype(o_ref.dtype)


def _attention(q_arr, k_arr, v_arr, o_prev, *, batch, groups, rep, dk, dv, qcol0, kcol0, vcol0,
               q_blk0, n_q, bq, kv_blk0, kv_len, bk, ctx_blk0, ctx_len, out_cols, name, log2_domain):
    has_ctx = ctx_blk0 is not None
    rows = q_arr.shape[0]
    in_specs = [pl.BlockSpec((bq, rep * dk), lambda b, g, i: (q_blk0 + b * n_q + i, qcol0 + g)),
                pl.BlockSpec((kv_len, dk), lambda b, g, i: (kv_blk0 + b, kcol0 + g)),
                pl.BlockSpec((kv_len, dv), lambda b, g, i: (kv_blk0 + b, vcol0 + g))]
    args = [q_arr, k_arr, v_arr]
    if has_ctx:
        in_specs += [pl.BlockSpec((ctx_len, dk), lambda b, g, i: (ctx_blk0 + b, kcol0 + g)),
                     pl.BlockSpec((ctx_len, dv), lambda b, g, i: (ctx_blk0 + b, vcol0 + g))]
        args += [k_arr, v_arr]
    aliases = {}
    if o_prev is not None:
        in_specs.append(pl.BlockSpec(memory_space=pl.ANY))
        args.append(o_prev)
        aliases = {len(args) - 1: 0}
    kern = functools.partial(_flash_kernel, rep=rep, dk=dk, dv=dv, bk=bk, n_kv=kv_len // bk, has_ctx=has_ctx,
                             log2_domain=log2_domain)
    if o_prev is not None:
        inner = kern

        def kern(*refs):
            n_in = 5 if has_ctx else 3
            return inner(*refs[:n_in], *refs[n_in + 1:])
    return pl.pallas_call(
        kern,
        grid=(batch, groups, n_q),
        in_specs=in_specs,
        out_specs=pl.BlockSpec((bq, rep * dv), lambda b, g, i: (q_blk0 + b * n_q + i, g)),
        out_shape=jax.ShapeDtypeStruct((rows, out_cols), BF16),
        scratch_shapes=[pltpu.VMEM((rep * bq, 1), F32), pltpu.VMEM((rep * bq, 1), F32),
                        pltpu.VMEM((rep * bq, dv), F32)],
        input_output_aliases=aliases,
        compiler_params=_params("arbitrary", "arbitrary", "arbitrary"),
        name=name,
    )(*args)


def _rpb_table_kernel(rpb_ref, o_ref, *, n_tiles, grid_rows):
    var = pl.program_id(0)
    h = pl.program_id(1)
    t_rep = jnp.where(var == 0, 0, jnp.where(var == 1, 1, n_tiles - 1))
    qc = lax.broadcasted_iota(jnp.int32, (GRID_W, GRID_W), 0)
    kc = lax.broadcasted_iota(jnp.int32, (GRID_W, GRID_W), 1)
    dc = kc - qc
    cstart = jnp.clip(qc - NA_WIN_COLS // 2, 0, GRID_W - NA_WIN_COLS)
    valid_c = (kc >= cstart) & (kc < cstart + NA_WIN_COLS)
    n_dc = 2 * NA_WIN_COLS - 1
    masked = jnp.full((GRID_W, GRID_W), MASK_VALUE, F32)
    toeplitz = {}
    for dr in range(-(NA_WIN_ROWS - 1), NA_WIN_ROWS):
        acc = jnp.zeros((GRID_W, GRID_W), F32)
        for b in range(n_dc):
            bias = rpb_ref[h, (dr + NA_WIN_ROWS - 1) * n_dc + b] * LOG2_E
            acc = jnp.where(dc == b - (NA_WIN_COLS - 1), bias, acc)
        toeplitz[dr] = jnp.where(valid_c, acc, MASK_VALUE)
    for ql in range(NA_TILE_ROWS):
        qr = NA_TILE_ROWS * t_rep + ql
        rstart = jnp.clip(qr - NA_WIN_ROWS // 2, 0, grid_rows - NA_WIN_ROWS)
        for kl in range(3 * NA_TILE_ROWS):
            kr = NA_TILE_ROWS * (t_rep - 1) + kl
            dr = kl - NA_TILE_ROWS - ql
            if abs(dr) > NA_WIN_ROWS - 1:
                blk = masked
            else:
                ok = ((kr >= rstart) & (kr < rstart + NA_WIN_ROWS)).astype(F32)
                blk = toeplitz[dr] * ok + MASK_VALUE * (1.0 - ok)
            o_ref[0, 0, ql * GRID_W:(ql + 1) * GRID_W, kl * GRID_W:(kl + 1) * GRID_W] = blk


def _rpb_table(rpb, n_tiles, grid_rows):
    heads = rpb.shape[0]
    tq = NA_TILE_ROWS * GRID_W
    kern = functools.partial(_rpb_table_kernel, n_tiles=n_tiles, grid_rows=grid_rows)
    return pl.pallas_call(
        kern,
        grid=(3, heads),
        in_specs=[pl.BlockSpec(memory_space=pltpu.SMEM)],
        out_specs=pl.BlockSpec((1, 1, tq, 3 * tq), lambda v, h: (v, h, 0, 0)),
        out_shape=jax.ShapeDtypeStruct((3, heads, tq, 3 * tq), F32),
        compiler_params=_params("arbitrary", "arbitrary"),
        name="na_rpb_table",
    )(rpb.reshape(heads, -1))


def _na_kernel(q_ref, kp_ref, kc_ref, kn_ref, kx_ref, vp_ref, vc_ref, vn_ref, vx_ref, tab_ref, o_ref, *, heads, dh):
    tq = q_ref.shape[0]
    k_refs = (kp_ref, kc_ref, kn_ref, kx_ref)
    v_refs = (vp_ref, vc_ref, vn_ref, vx_ref)
    ones = jnp.ones((tq, dh), BF16)

    def scores(h):
        sl = slice(h * dh, (h + 1) * dh)
        q = q_ref[:, sl]
        return [lax.dot_general(q, k_ref[:, sl], (((1,), (1,)), ((), ())), preferred_element_type=F32)
                for k_ref in k_refs]

    s_next = scores(0)
    for h in range(heads):
        sl = slice(h * dh, (h + 1) * dh)
        s = s_next
        if h + 1 < heads:
            s_next = scores(h + 1)
        s = [s[j] + tab_ref[0, h, :, j * tq:(j + 1) * tq] for j in range(3)] + [s[3]]
        m = jnp.max(jnp.maximum(jnp.maximum(s[0], s[1]), jnp.maximum(s[2], s[3])), axis=-1, keepdims=True)
        o = None
        for sj, v_ref in zip(s, v_refs):
            p = jnp.exp2(sj - m).astype(BF16)
            part = jnp.dot(p, jnp.concatenate([v_ref[:, sl], ones], axis=1), preferred_element_type=F32)
            o = part if o is None else o + part
        o_ref[:, sl] = (o[:, 0:dh] / o[:, dh:2 * dh]).astype(o_ref.dtype)


def _neighborhood_attention(qkv, table, *, batch, seq, ctx_len, heads, dh, hb):
    rows = qkv.shape[0]
    tq = NA_TILE_ROWS * GRID_W
    assert ctx_len == tq and seq % tq == 0
    n_tiles = seq // tq
    n_hg = heads // hb
    ctx_blk0 = batch * seq // ctx_len
    wb = hb * dh

    def var(t):
        return jnp.where(t == 0, 0, jnp.where(t == n_tiles - 1, 2, 1))

    def spec(col0, shift):
        if shift is None:
            return pl.BlockSpec((tq, wb), lambda b, g, t: (ctx_blk0 + b, col0 + g))
        return pl.BlockSpec((tq, wb), lambda b, g, t: (b * n_tiles + jnp.clip(t + shift, 0, n_tiles - 1), col0 + g))

    kern = functools.partial(_na_kernel, heads=hb, dh=dh)
    return pl.pallas_call(
        kern,
        grid=(batch, n_hg, n_tiles),
        in_specs=[spec(0, 0),
                  spec(n_hg, -1), spec(n_hg, 0), spec(n_hg, 1), spec(n_hg, None),
                  spec(2 * n_hg, -1), spec(2 * n_hg, 0), spec(2 * n_hg, 1), spec(2 * n_hg, None),
                  pl.BlockSpec((1, hb, tq, 3 * tq), lambda b, g, t: (var(t), g, 0, 0))],
        out_specs=pl.BlockSpec((tq, wb), lambda b, g, t: (b * n_tiles + t, g)),
        out_shape=jax.ShapeDtypeStruct((rows, heads * dh), BF16),
        compiler_params=_params("arbitrary", "arbitrary", "arbitrary"),
        name="na_attention",
    )(qkv, qkv, qkv, qkv, qkv, qkv, qkv, qkv, qkv, table)


def _final_norm_kernel(x_ref, g_ref, o_ref):
    o_ref[...] = _rms(x_ref[...], g_ref[...])


def _final_norm(x, g, n_row_tiles):
    bm = ROW_TILE
    d = x.shape[1]
    return pl.pallas_call(
        _final_norm_kernel,
        grid=(n_row_tiles,),
        in_specs=[pl.BlockSpec((bm, d), lambda m: (m, 0)), pl.BlockSpec((1, d), lambda m: (0, 0))],
        out_specs=pl.BlockSpec((bm, d), lambda m: (m, 0)),
        out_shape=jax.ShapeDtypeStruct((n_row_tiles * bm, d), F32),
        compiler_params=_params("arbitrary"),
        name="final_norm",
    )(x, g.reshape(1, d))


def _rope_tables(n_tok, rot_dim, batch, n_ctx_rows):
    quarter = rot_dim // 4
    t = jnp.arange(n_tok)
    row = (t // GRID_W).astype(F32)
    col = (t % GRID_W).astype(F32)
    inv = ROPE_THETA ** (-jnp.arange(quarter, dtype=F32) / quarter)
    ar = row[:, None] * inv
    ac = col[:, None] * inv
    ang = jnp.concatenate([ar, ar, ac, ac], axis=-1)
    cos, sin = jnp.cos(ang), jnp.sin(ang)
    first = (jnp.arange(rot_dim) % (2 * quarter)) < quarter
    sin_a = jnp.where(first, -sin, 0.0)
    sin_b = jnp.where(first, 0.0, sin)
    pad = LANES - rot_dim

    def full(tab, fill):
        tab = jnp.pad(tab, ((0, 0), (0, pad)), constant_values=fill)
        tab = jnp.tile(tab, (batch, 1))
        return jnp.concatenate([tab, jnp.full((n_ctx_rows, LANES), fill, F32)], axis=0)

    return full(cos, 1.0), full(sin_a, 0.0), full(sin_b, 0.0)


def kernel(x, c, ctx, c_ctx, ada_w, ada_b, mix_norm, ffn_norm, ffn_up, ffn_conv_w, ffn_conv_b, ffn_down, na_wqkv, na_rpb, na_wo, mla_wq_a, mla_q_norm, mla_wq_b, mla_wkv_a, mla_kv_norm, mla_wkv_b, mla_wo, gqa_wqkv, gqa_q_norm, gqa_k_norm, gqa_wo, final_norm):
    batch, seq, d = x.shape
    ctx_len = ctx.shape[1]
    depth = ada_w.shape[0]
    ffn_dim = ffn_down.shape[1]
    bm = ROW_TILE
    lat_rows = batch * seq
    ctx_rows = batch * ctx_len
    pad_rows = -(lat_rows + ctx_rows) % bm
    rows = lat_rows + ctx_rows + pad_rows
    assert seq % bm == 0 and batch < 8
    assert seq & (seq - 1) == 0 and ctx_len & (ctx_len - 1) == 0
    tiles_per_batch = seq // bm
    lat_tiles = lat_rows // bm
    all_tiles = rows // bm
    grid_rows = seq // GRID_W

    def mod_row(m):
        return jnp.minimum(m // tiles_per_batch, batch)

    xs = jnp.concatenate([x.reshape(lat_rows, d), ctx.reshape(ctx_rows, d), jnp.zeros((pad_rows, d), F32)], axis=0)
    c_rows = jnp.zeros((8, d), F32).at[:batch].set(c).at[batch].set(c_ctx)
    mod_all = _modulation(c_rows, ada_w, ada_b)

    cos_b, sa_b, sb_b = _rope_tables(seq, MLA_ROPE_DIM, batch, ctx_rows + pad_rows)
    cos_c, sa_c, sb_c = _rope_tables(seq, GQA_HEAD_DIM, batch, ctx_rows + pad_rows)

    def row_spec(width):
        return pl.BlockSpec((bm, width), lambda m, n: (m, 0))

    rope_specs = [row_spec(LANES)] * 3

    for i in range(depth):
        last = i == depth - 1
        kind, j = i % N_MIXERS, i // N_MIXERS
        mod = mod_all[i].reshape(8, 1, 6 * d)
        n_tiles = lat_tiles if last else all_tiles

        if kind == 0:
            scale = (d // NA_HEADS) ** -0.5 * LOG2_E
            colscale = jnp.concatenate([jnp.full((1, d), scale, F32), jnp.ones((1, 2 * d), F32)], axis=1)
            bn = d // 2
            qkv = _norm_matmul(
                xs, 0, d, mix_norm[i], mod, (1, 0), na_wqkv[j].astype(BF16), bn,
                [colscale], [pl.BlockSpec((1, bn), lambda m, n: (0, n))],
                jax.ShapeDtypeStruct((rows, 3 * d), BF16), pl.BlockSpec((bm, bn), lambda m, n: (m, n)),
                _epi_colscale, all_tiles, mod_row, "na_qkv")
            table = _rpb_table(na_rpb[j], seq // (NA_TILE_ROWS * GRID_W), grid_rows)
            dh = d // NA_HEADS
            o = _neighborhood_attention(qkv, table, batch=batch, seq=seq, ctx_len=ctx_len,
                                        heads=NA_HEADS, dh=dh, hb=8)
            if not last:
                o = _attention(qkv, qkv, qkv, o, batch=batch, groups=NA_HEADS, rep=1, dk=dh, dv=dh,
                               qcol0=0, kcol0=NA_HEADS, vcol0=2 * NA_HEADS,
                               q_blk0=lat_rows // ctx_len, n_q=1, bq=ctx_len,
                               kv_blk0=lat_rows // ctx_len, kv_len=ctx_len, bk=ctx_len,
                               ctx_blk0=None, ctx_len=None, out_cols=d, name="na_ctx_attention",
                               log2_domain=True)
            wo = na_wo[j]
        elif kind == 1:
            scale = (MLA_NOPE_DIM + MLA_ROPE_DIM) ** -0.5 * LOG2_E
            q_rank = mla_wq_a.shape[2]
            kv_rank = mla_kv_norm.shape[1]
            hq = 2 * LANES
            w1 = jnp.concatenate([mla_wq_a[j], mla_wkv_a[j], jnp.zeros((d, LANES - MLA_ROPE_DIM), F32)], axis=1)
            n1 = w1.shape[1]
            qc, kpe = _norm_matmul(
                xs, 0, d, mix_norm[i], mod, (1, 0), w1.astype(BF16), n1,
                [cos_b, sa_b, sb_b], rope_specs,
                (jax.ShapeDtypeStruct((rows, q_rank + kv_rank), F32), jax.ShapeDtypeStruct((rows, LANES), BF16)),
                (pl.BlockSpec((bm, q_rank + kv_rank), lambda m, n: (m, 0)), pl.BlockSpec((bm, LANES), lambda m, n: (m, 0))),
                _epi_mla_a, all_tiles, mod_row, "mla_a")
            wqb = mla_wq_b[j].reshape(q_rank, MLA_HEADS, MLA_NOPE_DIM + MLA_ROPE_DIM)
            wqb = jnp.pad(wqb, ((0, 0), (0, 0), (0, hq - MLA_NOPE_DIM - MLA_ROPE_DIM))).reshape(q_rank, MLA_HEADS * hq)
            bn = 4 * hq
            q = _norm_matmul(
                qc, 0, q_rank, mla_q_norm[j], None, None, wqb.astype(BF16), bn,
                [cos_b, sa_b, sb_b], rope_specs,
                jax.ShapeDtypeStruct((rows, MLA_HEADS * hq), BF16), pl.BlockSpec((bm, bn), lambda m, n: (m, n)),
                functools.partial(_epi_mla_q, scale=scale), all_tiles, mod_row, "mla_q")
            wkvb = mla_wkv_b[j].reshape(kv_rank, MLA_HEADS, MLA_NOPE_DIM + MLA_V_DIM)
            wk = wkvb[:, :, :MLA_NOPE_DIM].reshape(kv_rank, MLA_HEADS * MLA_NOPE_DIM)
            wv = wkvb[:, :, MLA_NOPE_DIM:].reshape(kv_rank, MLA_HEADS * MLA_V_DIM)
            bnk = 8 * MLA_NOPE_DIM
            k = _norm_matmul(
                qc, 1, kv_rank, mla_kv_norm[j], None, None, wk.astype(BF16), bnk,
                [kpe], [row_spec(LANES)],
                jax.ShapeDtypeStruct((rows, MLA_HEADS * hq), BF16), pl.BlockSpec((bm, 2 * bnk), lambda m, n: (m, n)),
                _epi_mla_k, all_tiles, mod_row, "mla_k")
            v = _norm_matmul(
                qc, 1, kv_rank, mla_kv_norm[j], None, None, wv.astype(BF16), bnk,
                [], [],
                jax.ShapeDtypeStruct((rows, MLA_HEADS * MLA_V_DIM), BF16), pl.BlockSpec((bm, bnk), lambda m, n: (m, n)),
                _epi_plain, all_tiles, mod_row, "mla_v")
            out_cols = MLA_HEADS * MLA_V_DIM
            o = _attention_t(q, k, v, batch=batch, seq=seq, ctx_len=ctx_len, heads=MLA_HEADS, groups=MLA_HEADS,
                             dk=hq, dv=MLA_V_DIM, qcol0=0, kcol0=0, vcol0=0, bq=min(2048, seq), bk=512, chunk=256,
                             out_cols=out_cols, name="mla_attention")
            if not last:
                o = _attention(q, k, v, o, batch=batch, groups=MLA_HEADS, rep=1, dk=hq, dv=MLA_V_DIM,
                               qcol0=0, kcol0=0, vcol0=0, q_blk0=lat_rows // ctx_len, n_q=1, bq=ctx_len,
                               kv_blk0=lat_rows // ctx_len, kv_len=ctx_len, bk=ctx_len,
                               ctx_blk0=None, ctx_len=None, out_cols=out_cols, name="mla_ctx_attention",
                               log2_domain=True)
            wo = mla_wo[j]
        else:
            scale = GQA_HEAD_DIM ** -0.5 * LOG2_E
            rep = GQA_HEADS // GQA_KV_HEADS
            nq_cols = GQA_HEADS * GQA_HEAD_DIM
            nk_cols = GQA_KV_HEADS * GQA_HEAD_DIM
            bn = nk_cols
            gain = jnp.concatenate([jnp.tile(gqa_q_norm[j], GQA_HEADS) * scale, jnp.tile(gqa_k_norm[j], GQA_KV_HEADS),
                                    jnp.ones((nk_cols,), F32)]).reshape(1, -1)
            qkv = _norm_matmul(
                xs, 0, d, mix_norm[i], mod, (1, 0), gqa_wqkv[j].astype(BF16), bn,
                [gain, cos_c, sa_c, sb_c], [pl.BlockSpec((1, bn), lambda m, n: (0, n))] + rope_specs,
                jax.ShapeDtypeStruct((rows, nq_cols + 2 * nk_cols), BF16), pl.BlockSpec((bm, bn), lambda m, n: (m, n)),
                functools.partial(_epi_gqa, n_qk_tiles=(nq_cols + nk_cols) // bn), all_tiles, mod_row, "gqa_qkv")
            dh = GQA_HEAD_DIM
            o = _attention_t(qkv, qkv, qkv, batch=batch, seq=seq, ctx_len=ctx_len, heads=GQA_HEADS,
                             groups=GQA_KV_HEADS, dk=dh, dv=dh, qcol0=0, kcol0=GQA_HEADS,
                             vcol0=GQA_HEADS + GQA_KV_HEADS, bq=min(512, seq), bk=512, chunk=256,
                             out_cols=nq_cols, name="gqa_attention")
            if not last:
                o = _attention(qkv, qkv, qkv, o, batch=batch, groups=GQA_KV_HEADS, rep=rep, dk=dh, dv=dh,
                               qcol0=0, kcol0=GQA_HEADS, vcol0=GQA_HEADS + GQA_KV_HEADS,
                               q_blk0=lat_rows // ctx_len, n_q=1, bq=ctx_len,
                               kv_blk0=lat_rows // ctx_len, kv_len=ctx_len, bk=ctx_len,
                               ctx_blk0=None, ctx_len=None, out_cols=nq_cols, name="gqa_ctx_attention",
                               log2_domain=True)
            wo = gqa_wo[j]

        xs = _matmul_residual(o, wo.astype(BF16), xs, mod, 2, d // 2, n_tiles, mod_row, "attn_out")
        a = _ffn_up(xs, ffn_norm[i], mod, ffn_up[i].astype(BF16), ffn_conv_w[i], ffn_conv_b[i], 512,
                    n_tiles, mod_row, lat_rows, seq, ctx_len, "ffn_up")
        xs = _matmul_residual(a, ffn_down[i].astype(BF16), xs, mod, 5, d // 4, n_tiles, mod_row, "ffn_down")

    out = _final_norm(xs, final_norm, lat_tiles)
    return out.reshape(batch, seq, d)
```

```python
import functools

import jax
import jax.numpy as jnp
from jax import lax
from jax.experimental import pallas as pl
from jax.experimental.pallas import tpu as pltpu

F32 = jnp.float32
BF16 = jnp.bfloat16

GRID_W = 64
NA_HEADS = 16
NA_WIN_ROWS = 8
NA_WIN_COLS = 16
MLA_HEADS = 16
MLA_NOPE_DIM = 128
MLA_ROPE_DIM = 64
MLA_V_DIM = 128
GQA_HEADS = 16
GQA_KV_HEADS = 4
GQA_HEAD_DIM = 128
CONV_WIDTH = 3
ROPE_THETA = 10000.0
NORM_EPS = 1e-6
N_MIXERS = 3
LOG2_E = 1.4426950408889634

LANES = 128
ROW_TILE = 1024
HALO = 8
ONES_ROWS = 16
FLASH_UNROLL = 4
FLASH_LEAD = 2
NA_TILE_ROWS = 4
MASK_VALUE = -1e30
VMEM_LIMIT = 56 * 1024 * 1024


def _params(*sem):
    return pltpu.CompilerParams(dimension_semantics=sem, vmem_limit_bytes=VMEM_LIMIT)


def _rms(x, g):
    y = x * lax.rsqrt(jnp.mean(x * x, axis=-1, keepdims=True) + NORM_EPS)
    return y * g


def _rope(y, cos, sin_a, sin_b, quarter):
    return (y * cos + pltpu.roll(y, LANES - quarter, 1) * sin_a
            + pltpu.roll(y, quarter, 1) * sin_b)


def _mod_kernel(c_ref, w_ref, b_ref, o_ref):
    s = jax.nn.silu(c_ref[...])
    o_ref[0] = jnp.dot(s.astype(BF16), w_ref[0].astype(BF16), preferred_element_type=F32) + b_ref[0]


def _modulation(c_rows, ada_w, ada_b):
    depth, d, n = ada_w.shape
    bn = n // 8
    return pl.pallas_call(
        _mod_kernel,
        grid=(depth, n // bn),
        in_specs=[pl.BlockSpec((8, d), lambda l, j: (0, 0)),
                  pl.BlockSpec((1, d, bn), lambda l, j: (l, 0, j)),
                  pl.BlockSpec((1, 1, bn), lambda l, j: (l, 0, j))],
        out_specs=pl.BlockSpec((1, 8, bn), lambda l, j: (l, 0, j)),
        out_shape=jax.ShapeDtypeStruct((depth, 8, n), F32),
        compiler_params=_params("arbitrary", "arbitrary"),
        name="adaln_mod",
    )(c_rows, ada_w, ada_b.reshape(depth, 1, n))


def _norm_matmul_kernel(*refs, has_mod, n_extra, epilogue):
    if has_mod:
        x_ref, g_ref, sc_ref, sh_ref, w_ref = refs[:5]
        rest = refs[5:]
    else:
        x_ref, g_ref, w_ref = refs[:3]
        rest = refs[3:]
    extras, outs, h_sc = rest[:n_extra], rest[n_extra:-1], rest[-1]
    n = pl.program_id(1)

    @pl.when(n == 0)
    def _():
        h = _rms(x_ref[...], g_ref[...])
        if has_mod:
            h = h * (1.0 + sc_ref[0]) + sh_ref[0]
        h_sc[...] = h.astype(BF16)

    acc = jnp.dot(h_sc[...], w_ref[...], preferred_element_type=F32)
    epilogue(acc, n, extras, outs)


def _norm_matmul(x, xcol, kdim, g, mod, mod_chunks, w, bn, extras, extra_specs, out_shapes, out_specs,
                 epilogue, n_row_tiles, mod_row, name):
    bm = ROW_TILE
    n_col = w.shape[1] // bn
    has_mod = mod is not None
    in_specs = [pl.BlockSpec((bm, kdim), lambda m, n: (m, xcol)),
                pl.BlockSpec((1, kdim), lambda m, n: (0, 0))]
    args = [x, g.reshape(1, kdim)]
    if has_mod:
        sc_chunk, sh_chunk = mod_chunks
        in_specs += [pl.BlockSpec((1, 1, kdim), lambda m, n: (mod_row(m), 0, sc_chunk)),
                     pl.BlockSpec((1, 1, kdim), lambda m, n: (mod_row(m), 0, sh_chunk))]
        args += [mod, mod]
    in_specs.append(pl.BlockSpec((kdim, bn), lambda m, n: (0, n)))
    args.append(w)
    in_specs += extra_specs
    args += extras
    kern = functools.partial(_norm_matmul_kernel, has_mod=has_mod, n_extra=len(extras), epilogue=epilogue)
    return pl.pallas_call(
        kern,
        grid=(n_row_tiles, n_col),
        in_specs=in_specs,
        out_specs=out_specs,
        out_shape=out_shapes,
        scratch_shapes=[pltpu.VMEM((bm, kdim), BF16)],
        compiler_params=_params("arbitrary", "arbitrary"),
        name=name,
    )(*args)


def _epi_colscale(acc, n, extras, outs):
    (cs_ref,), (o_ref,) = extras, outs
    o_ref[...] = (acc * cs_ref[...]).astype(o_ref.dtype)


def _epi_plain(acc, n, extras, outs):
    (o_ref,) = outs
    o_ref[...] = acc.astype(o_ref.dtype)


def _epi_mla_a(acc, n, extras, outs):
    cos_ref, sa_ref, sb_ref = extras
    qc_ref, kpe_ref = outs
    wide = qc_ref.shape[1]
    qc_ref[...] = acc[:, :wide]
    kpe = _rope(acc[:, wide:], cos_ref[...], sa_ref[...], sb_ref[...], MLA_ROPE_DIM // 4)
    kpe_ref[...] = kpe.astype(kpe_ref.dtype)


def _epi_mla_q(acc, n, extras, outs, *, scale):
    cos_ref, sa_ref, sb_ref = extras
    (o_ref,) = outs
    for c in range(acc.shape[1] // LANES):
        y = acc[:, c * LANES:(c + 1) * LANES]
        if c % 2 == 1:
            y = _rope(y, cos_ref[...], sa_ref[...], sb_ref[...], MLA_ROPE_DIM // 4)
        o_ref[:, c * LANES:(c + 1) * LANES] = (y * scale).astype(o_ref.dtype)


def _epi_mla_k(acc, n, extras, outs):
    (kpe_ref,), (o_ref,) = extras, outs
    kpe = kpe_ref[...]
    for c in range(acc.shape[1] // LANES):
        o_ref[:, (2 * c) * LANES:(2 * c + 1) * LANES] = acc[:, c * LANES:(c + 1) * LANES].astype(o_ref.dtype)
        o_ref[:, (2 * c + 1) * LANES:(2 * c + 2) * LANES] = kpe


def _epi_gqa(acc, n, extras, outs, *, n_qk_tiles):
    gain_ref, cos_ref, sa_ref, sb_ref = extras
    (o_ref,) = outs

    @pl.when(n < n_qk_tiles)
    def _():
        for c in range(acc.shape[1] // LANES):
            y = _rms(acc[:, c * LANES:(c + 1) * LANES], gain_ref[:, c * LANES:(c + 1) * LANES])
            y = _rope(y, cos_ref[...], sa_ref[...], sb_ref[...], GQA_HEAD_DIM // 4)
            o_ref[:, c * LANES:(c + 1) * LANES] = y.astype(o_ref.dtype)

    @pl.when(n >= n_qk_tiles)
    def _():
        o_ref[...] = acc.astype(o_ref.dtype)


def _mm_res_kernel(a_ref, w_ref, x_ref, g_ref, o_ref):
    acc = jnp.dot(a_ref[...], w_ref[...], preferred_element_type=F32)
    o_ref[...] = x_ref[...] + g_ref[0] * acc


def _matmul_residual(a, w, x, mod, gate_chunk, bn, n_row_tiles, mod_row, name):
    bm = ROW_TILE
    kdim, n_out = w.shape
    assert n_out == x.shape[1] and n_out % bn == 0
    per_chunk = n_out // bn
    return pl.pallas_call(
        _mm_res_kernel,
        grid=(n_row_tiles, per_chunk),
        in_specs=[pl.BlockSpec((bm, kdim), lambda m, n: (m, 0)),
                  pl.BlockSpec((kdim, bn), lambda m, n: (0, n)),
                  pl.BlockSpec((bm, bn), lambda m, n: (m, n)),
                  pl.BlockSpec((1, 1, bn), lambda m, n: (mod_row(m), 0, gate_chunk * per_chunk + n))],
        out_specs=pl.BlockSpec((bm, bn), lambda m, n: (m, n)),
        out_shape=jax.ShapeDtypeStruct(x.shape, x.dtype),
        input_output_aliases={2: 0},
        compiler_params=_params("arbitrary", "arbitrary"),
        name=name,
    )(a, w, x, mod)


def _ffn_up_kernel(x_ref, xp_ref, xn_ref, g_ref, sc_ref, sh_ref, wg_ref, wv_ref, cw_ref, cb_ref, o_ref, h_sc,
                   *, lat_rows, lat_seq, ctx_seq):
    bm = x_ref.shape[0]
    m = pl.program_id(0)

    @pl.when(pl.program_id(1) == 0)
    def _():
        def nm(x):
            return (_rms(x, g_ref[...]) * (1.0 + sc_ref[0]) + sh_ref[0]).astype(BF16)
        h_sc[0:bm, :] = nm(x_ref[...])
        h_sc[bm:bm + 2 * HALO, :] = nm(jnp.concatenate([xn_ref[...], xp_ref[...]], axis=0))

    ext = bm + 2 * HALO
    row = m * bm + lax.broadcasted_iota(jnp.int32, (bm, 1), 0)
    in_ctx = row >= lat_rows
    pos = jnp.where(in_ctx, (row - lat_rows) & (ctx_seq - 1), row & (lat_seq - 1))
    seq = jnp.where(in_ctx, ctx_seq, lat_seq)
    gate = jnp.dot(h_sc[...], wg_ref[...], preferred_element_type=F32)
    val = jnp.dot(h_sc[0:bm, :], wv_ref[...], preferred_element_type=F32)
    g_prev = pltpu.roll(gate, 1, 0)[0:bm]
    g_next = pltpu.roll(gate, ext - 1, 0)[0:bm]
    g_cur = gate[0:bm]
    g_prev = jnp.where(pos == 0, 0.0, g_prev)
    g_next = jnp.where(pos == seq - 1, 0.0, g_next)
    cw = cw_ref[...]
    z = cb_ref[...] + g_prev * cw[0:1] + g_cur * cw[1:2] + g_next * cw[2:3]
    o_ref[...] = (jax.nn.silu(z) * val).astype(o_ref.dtype)


def _ffn_up(x, g, mod, w_up, conv_w, conv_b, bn, n_row_tiles, mod_row, lat_rows, lat_seq, ctx_seq, name):
    bm = ROW_TILE
    rows, d = x.shape
    f = w_up.shape[1] // 2
    n_col = f // bn
    last_halo = rows // HALO - 1
    per = bm // HALO
    kern = functools.partial(_ffn_up_kernel, lat_rows=lat_rows, lat_seq=lat_seq, ctx_seq=ctx_seq)
    return pl.pallas_call(
        kern,
        grid=(n_row_tiles, n_col),
        in_specs=[pl.BlockSpec((bm, d), lambda m, n: (m, 0)),
                  pl.BlockSpec((HALO, d), lambda m, n: (jnp.maximum(m * per - 1, 0), 0)),
                  pl.BlockSpec((HALO, d), lambda m, n: (jnp.minimum((m + 1) * per, last_halo), 0)),
                  pl.BlockSpec((1, d), lambda m, n: (0, 0)),
                  pl.BlockSpec((1, 1, d), lambda m, n: (mod_row(m), 0, 4)),
                  pl.BlockSpec((1, 1, d), lambda m, n: (mod_row(m), 0, 3)),
                  pl.BlockSpec((d, bn), lambda m, n: (0, n)),
                  pl.BlockSpec((d, bn), lambda m, n: (0, n_col + n)),
                  pl.BlockSpec((CONV_WIDTH, bn), lambda m, n: (0, n)),
                  pl.BlockSpec((1, bn), lambda m, n: (0, n))],
        out_specs=pl.BlockSpec((bm, bn), lambda m, n: (m, n)),
        out_shape=jax.ShapeDtypeStruct((rows, f), BF16),
        scratch_shapes=[pltpu.VMEM((bm + 2 * HALO, d), BF16)],
        compiler_params=_params("arbitrary", "arbitrary"),
        name=name,
    )(x, x, x, g.reshape(1, d), mod, mod, w_up, w_up, conv_w, conv_b.reshape(1, f))


def _flash_t_kernel(qt_ref, k_ref, vt_ref, kc_ref, vct_ref, o_ref, m_sc, acc_sc, s_sc, mx_sc, *, rep, bk, n_kv,
                    chunk, unroll, lead):
    bq = qt_ref.shape[3]
    dv = vt_ref.shape[3] - ONES_ROWS
    n_chunks = rep * bq // chunk

    def scores(k, c, half, n_keys):
        r, off = divmod(c * chunk, bq)
        s = jnp.dot(k, qt_ref[0, r, :, off:off + chunk], preferred_element_type=F32)
        s_sc[half, c, 0:n_keys, :] = s
        mx_sc[half, c] = jnp.max(s, axis=0, keepdims=True)

    def softmax_pv(half, n_keys, vt, c, is_first):
        cs = slice(c * chunk, (c + 1) * chunk)
        s = s_sc[half, c, 0:n_keys, :]
        mx = mx_sc[half, c]
        if is_first:
            m_new = mx
            p = jnp.exp2(s - m_new)
            acc_sc[:, cs] = jnp.dot(vt, p.astype(vt.dtype), preferred_element_type=F32)
        else:
            m_prev = m_sc[:, cs]
            m_new = jnp.maximum(m_prev, mx)
            alpha = jnp.exp2(m_prev - m_new)
            p = jnp.exp2(s - m_new)
            acc_sc[:, cs] = alpha * acc_sc[:, cs] + jnp.dot(vt, p.astype(vt.dtype), preferred_element_type=F32)
        m_sc[:, cs] = m_new

    kc = kc_ref[...]
    k0 = k_ref[0:bk, :]
    n_ctx = kc_ref.shape[0]
    assert n_ctx <= bk
    for c in range(n_chunks):
        scores(kc, c, 1, n_ctx)
    for c in range(n_chunks):
        scores(k0, c, 0, bk)
        softmax_pv(1, n_ctx, vct_ref[0, 0], c, True)

    def body(i, carry):
        def issue_scores(t):
            u, c = divmod(t, n_chunks)
            j_next = jnp.minimum(unroll * i + u + 1, n_kv - 1)
            off = pl.multiple_of(j_next * bk, bk)
            scores(k_ref[pl.ds(off, bk), :], c, (u + 1) % 2, bk)

        def consume(t):
            u, c = divmod(t, n_chunks)
            softmax_pv(u % 2, bk, vt_ref[0, 0, unroll * i + u], c, False)

        n_tasks = unroll * n_chunks
        for t in range(n_tasks + lead):
            if t < n_tasks:
                issue_scores(t)
            if t >= lead:
                consume(t - lead)
        return carry

    assert unroll % 2 == 0 and n_kv % unroll == 0 and lead < n_chunks
    lax.fori_loop(0, n_kv // unroll, body, 0)
    for r in range(rep):
        ot = acc_sc[0:dv, r * bq:(r + 1) * bq] / acc_sc[dv:dv + 1, r * bq:(r + 1) * bq]
        o_ref[:, r * dv:(r + 1) * dv] = ot.T.astype(o_ref.dtype)


def _attention_t(q_arr, k_arr, v_arr, *, batch, seq, ctx_len, heads, groups, dk, dv, qcol0, kcol0, vcol0,
                 bq, bk, chunk, out_cols, name):
    rows = q_arr.shape[0]
    lat_rows = batch * seq
    rep = heads // groups
    n_q, n_kv = seq // bq, seq // bk
    ctx_blk0 = lat_rows // ctx_len
    qt = q_arr[:lat_rows, qcol0 * dk:(qcol0 + heads) * dk].reshape(batch, seq, heads, dk).transpose(0, 2, 3, 1)
    v_lat = v_arr[:lat_rows, vcol0 * dv:(vcol0 + groups) * dv]
    vt = v_lat.reshape(batch, n_kv, bk, groups, dv).transpose(0, 3, 1, 4, 2)
    vt = jnp.concatenate([vt, jnp.ones(vt.shape[:3] + (ONES_ROWS, bk), vt.dtype)], axis=3)
    v_ctx = v_arr[lat_rows:lat_rows + batch * ctx_len, vcol0 * dv:(vcol0 + groups) * dv]
    vct = v_ctx.reshape(batch, ctx_len, groups, dv).transpose(0, 2, 3, 1)
    vct = jnp.concatenate([vct, jnp.ones(vct.shape[:2] + (ONES_ROWS, ctx_len), vct.dtype)], axis=2)
    dve = dv + ONES_ROWS
    kern = functools.partial(_flash_t_kernel, rep=rep, bk=bk, n_kv=n_kv, chunk=chunk,
                             unroll=min(FLASH_UNROLL, n_kv), lead=FLASH_LEAD)
    return pl.pallas_call(
        kern,
        grid=(batch, groups, n_q),
        in_specs=[pl.BlockSpec((1, rep, dk, bq), lambda b, g, i: (b, g, 0, i)),
                  pl.BlockSpec((seq, dk), lambda b, g, i: (b, kcol0 + g)),
                  pl.BlockSpec((1, 1, n_kv, dve, bk), lambda b, g, i: (b, g, 0, 0, 0)),
                  pl.BlockSpec((ctx_len, dk), lambda b, g, i: (ctx_blk0 + b, kcol0 + g)),
                  pl.BlockSpec((1, 1, dve, ctx_len), lambda b, g, i: (b, g, 0, 0))],
        out_specs=pl.BlockSpec((bq, rep * dv), lambda b, g, i: (b * n_q + i, g)),
        out_shape=jax.ShapeDtypeStruct((rows, out_cols), BF16),
        scratch_shapes=[pltpu.VMEM((1, rep * bq), F32), pltpu.VMEM((dve, rep * bq), F32),
                        pltpu.VMEM((2, rep * bq // chunk, bk, chunk), F32),
                        pltpu.VMEM((2, rep * bq // chunk, 1, chunk), F32)],
        compiler_params=_params("arbitrary", "arbitrary", "arbitrary"),
        name=name,
    )(qt, k_arr, vt, k_arr, vct)


def _flash_kernel(*refs, rep, dk, dv, bk, n_kv, has_ctx, log2_domain):
    if has_ctx:
        q_ref, k_ref, v_ref, kc_ref, vc_ref, o_ref, m_sc, l_sc, acc_sc = refs
    else:
        q_ref, k_ref, v_ref, o_ref, m_sc, l_sc, acc_sc = refs
    ex = jnp.exp2 if log2_domain else jnp.exp
    bq = q_ref.shape[0]
    if rep > 1:
        q = jnp.concatenate([q_ref[:, r * dk:(r + 1) * dk] for r in range(rep)], axis=0)
    else:
        q = q_ref[...]

    def scores(k):
        return lax.dot_general(q, k, (((1,), (1,)), ((), ())), preferred_element_type=F32)

    def first(k, v):
        s = scores(k)
        m = jnp.max(s, axis=-1, keepdims=True)
        p = ex(s - m)
        m_sc[...] = m
        l_sc[...] = jnp.sum(p, axis=-1, keepdims=True)
        acc_sc[...] = jnp.dot(p.astype(v.dtype), v, preferred_element_type=F32)

    def update(k, v):
        s = scores(k)
        m_prev = m_sc[...]
        m_new = jnp.maximum(m_prev, jnp.max(s, axis=-1, keepdims=True))
        alpha = ex(m_prev - m_new)
        p = ex(s - m_new)
        l_sc[...] = alpha * l_sc[...] + jnp.sum(p, axis=-1, keepdims=True)
        acc_sc[...] = alpha * acc_sc[...] + jnp.dot(p.astype(v.dtype), v, preferred_element_type=F32)
        m_sc[...] = m_new

    if has_ctx:
        first(kc_ref[...], vc_ref[...])
        start = 0
    else:
        first(k_ref[0:bk, :], v_ref[0:bk, :])
        start = 1

    def body(j, carry):
        off = pl.multiple_of(j * bk, bk)
        update(k_ref[pl.ds(off, bk), :], v_ref[pl.ds(off, bk), :])
        return carry

    lax.fori_loop(start, n_kv, body, 0)
    o = acc_sc[...] / l_sc[...]
    for r in range(rep):
        o_ref[:, r * dv:(r + 1) * dv] = o[r * bq:(r + 1) * bq].astype(o_ref.dtype)


def _attention(q_arr, k_arr, v_arr, o_prev, *, batch, groups, rep, dk, dv, qcol0, kcol0, vcol0,
               q_blk0, n_q, bq, kv_blk0, kv_len, bk, ctx_blk0, ctx_len, out_cols, name, log2_domain):
    has_ctx = ctx_blk0 is not None
    rows = q_arr.shape[0]
    in_specs = [pl.BlockSpec((bq, rep * dk), lambda b, g, i: (q_blk0 + b * n_q + i, qcol0 + g)),
                pl.BlockSpec((kv_len, dk), lambda b, g, i: (kv_blk0 + b, kcol0 + g)),
                pl.BlockSpec((kv_len, dv), lambda b, g, i: (kv_blk0 + b, vcol0 + g))]
    args = [q_arr, k_arr, v_arr]
    if has_ctx:
        in_specs += [pl.BlockSpec((ctx_len, dk), lambda b, g, i: (ctx_blk0 + b, kcol0 + g)),
                     pl.BlockSpec((ctx_len, dv), lambda b, g, i: (ctx_blk0 + b, vcol0 + g))]
        args += [k_arr, v_arr]
    aliases = {}
    if o_prev is not None:
        in_specs.append(pl.BlockSpec(memory_space=pl.ANY))
        args.append(o_prev)
        aliases = {len(args) - 1: 0}
    kern = functools.partial(_flash_kernel, rep=rep, dk=dk, dv=dv, bk=bk, n_kv=kv_len // bk, has_ctx=has_ctx,
                             log2_domain=log2_domain)
    if o_prev is not None:
        inner = kern

        def kern(*refs):
            n_in = 5 if has_ctx else 3
            return inner(*refs[:n_in], *refs[n_in + 1:])
    return pl.pallas_call(
        kern,
        grid=(batch, groups, n_q),
        in_specs=in_specs,
        out_specs=pl.BlockSpec((bq, rep * dv), lambda b, g, i: (q_blk0 + b * n_q + i, g)),
        out_shape=jax.ShapeDtypeStruct((rows, out_cols), BF16),
        scratch_shapes=[pltpu.VMEM((rep * bq, 1), F32), pltpu.VMEM((rep * bq, 1), F32),
                        pltpu.VMEM((rep * bq, dv), F32)],
        input_output_aliases=aliases,
        compiler_params=_params("arbitrary", "arbitrary", "arbitrary"),
        name=name,
    )(*args)


def _rpb_table_kernel(rpb_ref, o_ref, *, n_tiles, grid_rows):
    var = pl.program_id(0)
    h = pl.program_id(1)
    t_rep = jnp.where(var == 0, 0, jnp.where(var == 1, 1, n_tiles - 1))
    qc = lax.broadcasted_iota(jnp.int32, (GRID_W, GRID_W), 0)
    kc = lax.broadcasted_iota(jnp.int32, (GRID_W, GRID_W), 1)
    dc = kc - qc
    cstart = jnp.clip(qc - NA_WIN_COLS // 2, 0, GRID_W - NA_WIN_COLS)
    valid_c = (kc >= cstart) & (kc < cstart + NA_WIN_COLS)
    n_dc = 2 * NA_WIN_COLS - 1
    masked = jnp.full((GRID_W, GRID_W), MASK_VALUE, F32)
    toeplitz = {}
    for dr in range(-(NA_WIN_ROWS - 1), NA_WIN_ROWS):
        acc = jnp.zeros((GRID_W, GRID_W), F32)
        for b in range(n_dc):
            bias = rpb_ref[h, (dr + NA_WIN_ROWS - 1) * n_dc + b] * LOG2_E
            acc = jnp.where(dc == b - (NA_WIN_COLS - 1), bias, acc)
        toeplitz[dr] = jnp.where(valid_c, acc, MASK_VALUE)
    for ql in range(NA_TILE_ROWS):
        qr = NA_TILE_ROWS * t_rep + ql
        rstart = jnp.clip(qr - NA_WIN_ROWS // 2, 0, grid_rows - NA_WIN_ROWS)
        for kl in range(3 * NA_TILE_ROWS):
            kr = NA_TILE_ROWS * (t_rep - 1) + kl
            dr = kl - NA_TILE_ROWS - ql
            if abs(dr) > NA_WIN_ROWS - 1:
                blk = masked
            else:
                ok = ((kr >= rstart) & (kr < rstart + NA_WIN_ROWS)).astype(F32)
                blk = toeplitz[dr] * ok + MASK_VALUE * (1.0 - ok)
            o_ref[0, 0, ql * GRID_W:(ql + 1) * GRID_W, kl * GRID_W:(kl + 1) * GRID_W] = blk


def _rpb_table(rpb, n_tiles, grid_rows):
    heads = rpb.shape[0]
    tq = NA_TILE_ROWS * GRID_W
    kern = functools.partial(_rpb_table_kernel, n_tiles=n_tiles, grid_rows=grid_rows)
    return pl.pallas_call(
        kern,
        grid=(3, heads),
        in_specs=[pl.BlockSpec(memory_space=pltpu.SMEM)],
        out_specs=pl.BlockSpec((1, 1, tq, 3 * tq), lambda v, h: (v, h, 0, 0)),
        out_shape=jax.ShapeDtypeStruct((3, heads, tq, 3 * tq), F32),
        compiler_params=_params("arbitrary", "arbitrary"),
        name="na_rpb_table",
    )(rpb.reshape(heads, -1))


def _na_kernel(q_ref, kp_ref, kc_ref, kn_ref, kx_ref, vp_ref, vc_ref, vn_ref, vx_ref, tab_ref, o_ref, *, heads, dh):
    tq = q_ref.shape[0]
    k_refs = (kp_ref, kc_ref, kn_ref, kx_ref)
    v_refs = (vp_ref, vc_ref, vn_ref, vx_ref)
    ones = jnp.ones((tq, dh), BF16)

    def scores(h):
        sl = slice(h * dh, (h + 1) * dh)
        q = q_ref[:, sl]
        return [lax.dot_general(q, k_ref[:, sl], (((1,), (1,)), ((), ())), preferred_element_type=F32)
                for k_ref in k_refs]

    s_next = scores(0)
    for h in range(heads):
        sl = slice(h * dh, (h + 1) * dh)
        s = s_next
        if h + 1 < heads:
            s_next = scores(h + 1)
        s = [s[j] + tab_ref[0, h, :, j * tq:(j + 1) * tq] for j in range(3)] + [s[3]]
        m = jnp.max(jnp.maximum(jnp.maximum(s[0], s[1]), jnp.maximum(s[2], s[3])), axis=-1, keepdims=True)
        o = None
        for sj, v_ref in zip(s, v_refs):
            p = jnp.exp2(sj - m).astype(BF16)
            part = jnp.dot(p, jnp.concatenate([v_ref[:, sl], ones], axis=1), preferred_element_type=F32)
            o = part if o is None else o + part
        o_ref[:, sl] = (o[:, 0:dh] / o[:, dh:2 * dh]).astype(o_ref.dtype)


def _neighborhood_attention(qkv, table, *, batch, seq, ctx_len, heads, dh, hb):
    rows = qkv.shape[0]
    tq = NA_TILE_ROWS * GRID_W
    assert ctx_len == tq and seq % tq == 0
    n_tiles = seq // tq
    n_hg = heads // hb
    ctx_blk0 = batch * seq // ctx_len
    wb = hb * dh

    def var(t):
        return jnp.where(t == 0, 0, jnp.where(t == n_tiles - 1, 2, 1))

    def spec(col0, shift):
        if shift is None:
            return pl.BlockSpec((tq, wb), lambda b, g, t: (ctx_blk0 + b, col0 + g))
        return pl.BlockSpec((tq, wb), lambda b, g, t: (b * n_tiles + jnp.clip(t + shift, 0, n_tiles - 1), col0 + g))

    kern = functools.partial(_na_kernel, heads=hb, dh=dh)
    return pl.pallas_call(
        kern,
        grid=(batch, n_hg, n_tiles),
        in_specs=[spec(0, 0),
                  spec(n_hg, -1), spec(n_hg, 0), spec(n_hg, 1), spec(n_hg, None),
                  spec(2 * n_hg, -1), spec(2 * n_hg, 0), spec(2 * n_hg, 1), spec(2 * n_hg, None),
                  pl.BlockSpec((1, hb, tq, 3 * tq), lambda b, g, t: (var(t), g, 0, 0))],
        out_specs=pl.BlockSpec((tq, wb), lambda b, g, t: (b * n_tiles + t, g)),
        out_shape=jax.ShapeDtypeStruct((rows, heads * dh), BF16),
        compiler_params=_params("arbitrary", "arbitrary", "arbitrary"),
        name="na_attention",
    )(qkv, qkv, qkv, qkv, qkv, qkv, qkv, qkv, qkv, table)


def _final_norm_kernel(x_ref, g_ref, o_ref):
    o_ref[...] = _rms(x_ref[...], g_ref[...])


def _final_norm(x, g, n_row_tiles):
    bm = ROW_TILE
    d = x.shape[1]
    return pl.pallas_call(
        _final_norm_kernel,
        grid=(n_row_tiles,),
        in_specs=[pl.BlockSpec((bm, d), lambda m: (m, 0)), pl.BlockSpec((1, d), lambda m: (0, 0))],
        out_specs=pl.BlockSpec((bm, d), lambda m: (m, 0)),
        out_shape=jax.ShapeDtypeStruct((n_row_tiles * bm, d), F32),
        compiler_params=_params("arbitrary"),
        name="final_norm",
    )(x, g.reshape(1, d))


def _rope_tables(n_tok, rot_dim, batch, n_ctx_rows):
    quarter = rot_dim // 4
    t = jnp.arange(n_tok)
    row = (t // GRID_W).astype(F32)
    col = (t % GRID_W).astype(F32)
    inv = ROPE_THETA ** (-jnp.arange(quarter, dtype=F32) / quarter)
    ar = row[:, None] * inv
    ac = col[:, None] * inv
    ang = jnp.concatenate([ar, ar, ac, ac], axis=-1)
    cos, sin = jnp.cos(ang), jnp.sin(ang)
    first = (jnp.arange(rot_dim) % (2 * quarter)) < quarter
    sin_a = jnp.where(first, -sin, 0.0)
    sin_b = jnp.where(first, 0.0, sin)
    pad = LANES - rot_dim

    def full(tab, fill):
        tab = jnp.pad(tab, ((0, 0), (0, pad)), constant_values=fill)
        tab = jnp.tile(tab, (batch, 1))
        return jnp.concatenate([tab, jnp.full((n_ctx_rows, LANES), fill, F32)], axis=0)

    return full(cos, 1.0), full(sin_a, 0.0), full(sin_b, 0.0)


def kernel(x, c, ctx, c_ctx, ada_w, ada_b, mix_norm, ffn_norm, ffn_up, ffn_conv_w, ffn_conv_b, ffn_down, na_wqkv, na_rpb, na_wo, mla_wq_a, mla_q_norm, mla_wq_b, mla_wkv_a, mla_kv_norm, mla_wkv_b, mla_wo, gqa_wqkv, gqa_q_norm, gqa_k_norm, gqa_wo, final_norm):
    batch, seq, d = x.shape
    ctx_len = ctx.shape[1]
    depth = ada_w.shape[0]
    ffn_dim = ffn_down.shape[1]
    bm = ROW_TILE
    lat_rows = batch * seq
    ctx_rows = batch * ctx_len
    pad_rows = -(lat_rows + ctx_rows) % bm
    rows = lat_rows + ctx_rows + pad_rows
    assert seq % bm == 0 and batch < 8
    assert seq & (seq - 1) == 0 and ctx_len & (ctx_len - 1) == 0
    tiles_per_batch = seq // bm
    lat_tiles = lat_rows // bm
    all_tiles = rows // bm
    grid_rows = seq // GRID_W

    def mod_row(m):
        return jnp.minimum(m // tiles_per_batch, batch)

    xs = jnp.concatenate([x.reshape(lat_rows, d), ctx.reshape(ctx_rows, d), jnp.zeros((pad_rows, d), F32)], axis=0)
    c_rows = jnp.zeros((8, d), F32).at[:batch].set(c).at[batch].set(c_ctx)
    mod_all = _modulation(c_rows, ada_w, ada_b)

    cos_b, sa_b, sb_b = _rope_tables(seq, MLA_ROPE_DIM, batch, ctx_rows + pad_rows)
    cos_c, sa_c, sb_c = _rope_tables(seq, GQA_HEAD_DIM, batch, ctx_rows + pad_rows)

    def row_spec(width):
        return pl.BlockSpec((bm, width), lambda m, n: (m, 0))

    rope_specs = [row_spec(LANES)] * 3

    for i in range(depth):
        last = i == depth - 1
        kind, j = i % N_MIXERS, i // N_MIXERS
        mod = mod_all[i].reshape(8, 1, 6 * d)
        n_tiles = lat_tiles if last else all_tiles

        if kind == 0:
            scale = (d // NA_HEADS) ** -0.5 * LOG2_E
            colscale = jnp.concatenate([jnp.full((1, d), scale, F32), jnp.ones((1, 2 * d), F32)], axis=1)
            bn = d // 2
            qkv = _norm_matmul(
                xs, 0, d, mix_norm[i], mod, (1, 0), na_wqkv[j].astype(BF16), bn,
                [colscale], [pl.BlockSpec((1, bn), lambda m, n: (0, n))],
                jax.ShapeDtypeStruct((rows, 3 * d), BF16), pl.BlockSpec((bm, bn), lambda m, n: (m, n)),
                _epi_colscale, all_tiles, mod_row, "na_qkv")
            table = _rpb_table(na_rpb[j], seq // (NA_TILE_ROWS * GRID_W), grid_rows)
            dh = d // NA_HEADS
            o = _neighborhood_attention(qkv, table, batch=batch, seq=seq, ctx_len=ctx_len,
                                        heads=NA_HEADS, dh=dh, hb=8)
            if not last:
                o = _attention(qkv, qkv, qkv, o, batch=batch, groups=NA_HEADS, rep=1, dk=dh, dv=dh,
                               qcol0=0, kcol0=NA_HEADS, vcol0=2 * NA_HEADS,
                               q_blk0=lat_rows // ctx_len, n_q=1, bq=ctx_len,
                               kv_blk0=lat_rows // ctx_len, kv_len=ctx_len, bk=ctx_len,
                               ctx_blk0=None, ctx_len=None, out_cols=d, name="na_ctx_attention",
                               log2_domain=True)
            wo = na_wo[j]
        elif kind == 1:
            scale = (MLA_NOPE_DIM + MLA_ROPE_DIM) ** -0.5 * LOG2_E
            q_rank = mla_wq_a.shape[2]
            kv_rank = mla_kv_norm.shape[1]
            hq = 2 * LANES
            w1 = jnp.concatenate([mla_wq_a[j], mla_wkv_a[j], jnp.zeros((d, LANES - MLA_ROPE_DIM), F32)], axis=1)
            n1 = w1.shape[1]
            qc, kpe = _norm_matmul(
                xs, 0, d, mix_norm[i], mod, (1, 0), w1.astype(BF16), n1,
                [cos_b, sa_b, sb_b], rope_specs,
                (jax.ShapeDtypeStruct((rows, q_rank + kv_rank), F32), jax.ShapeDtypeStruct((rows, LANES), BF16)),
                (pl.BlockSpec((bm, q_rank + kv_rank), lambda m, n: (m, 0)), pl.BlockSpec((bm, LANES), lambda m, n: (m, 0))),
                _epi_mla_a, all_tiles, mod_row, "mla_a")
            wqb = mla_wq_b[j].reshape(q_rank, MLA_HEADS, MLA_NOPE_DIM + MLA_ROPE_DIM)
            wqb = jnp.pad(wqb, ((0, 0), (0, 0), (0, hq - MLA_NOPE_DIM - MLA_ROPE_DIM))).reshape(q_rank, MLA_HEADS * hq)
            bn = 4 * hq
            q = _norm_matmul(
                qc, 0, q_rank, mla_q_norm[j], None, None, wqb.astype(BF16), bn,
                [cos_b, sa_b, sb_b], rope_specs,
                jax.ShapeDtypeStruct((rows, MLA_HEADS * hq), BF16), pl.BlockSpec((bm, bn), lambda m, n: (m, n)),
                functools.partial(_epi_mla_q, scale=scale), all_tiles, mod_row, "mla_q")
            wkvb = mla_wkv_b[j].reshape(kv_rank, MLA_HEADS, MLA_NOPE_DIM + MLA_V_DIM)
            wk = wkvb[:, :, :MLA_NOPE_DIM].reshape(kv_rank, MLA_HEADS * MLA_NOPE_DIM)
            wv = wkvb[:, :, MLA_NOPE_DIM:].reshape(kv_rank, MLA_HEADS * MLA_V_DIM)
            bnk = 8 * MLA_NOPE_DIM
            k = _norm_matmul(
                qc, 1, kv_rank, mla_kv_norm[j], None, None, wk.astype(BF16), bnk,
                [kpe], [row_spec(LANES)],
                jax.ShapeDtypeStruct((rows, MLA_HEADS * hq), BF16), pl.BlockSpec((bm, 2 * bnk), lambda m, n: (m, n)),
                _epi_mla_k, all_tiles, mod_row, "mla_k")
            v = _norm_matmul(
                qc, 1, kv_rank, mla_kv_norm[j], None, None, wv.astype(BF16), bnk,
                [], [],
                jax.ShapeDtypeStruct((rows, MLA_HEADS * MLA_V_DIM), BF16), pl.BlockSpec((bm, bnk), lambda m, n: (m, n)),
                _epi_plain, all_tiles, mod_row, "mla_v")
            out_cols = MLA_HEADS * MLA_V_DIM
            o = _attention_t(q, k, v, batch=batch, seq=seq, ctx_len=ctx_len, heads=MLA_HEADS, groups=MLA_HEADS,
                             dk=hq, dv=MLA_V_DIM, qcol0=0, kcol0=0, vcol0=0, bq=min(2048, seq), bk=512, chunk=256,
                             out_cols=out_cols, name="mla_attention")
            if not last:
                o = _attention(q, k, v, o, batch=batch, groups=MLA_HEADS, rep=1, dk=hq, dv=MLA_V_DIM,
                               qcol0=0, kcol0=0, vcol0=0, q_blk0=lat_rows // ctx_len, n_q=1, bq=ctx_len,
                               kv_blk0=lat_rows // ctx_len, kv_len=ctx_len, bk=ctx_len,
                               ctx_blk0=None, ctx_len=None, out_cols=out_cols, name="mla_ctx_attention",
                               log2_domain=True)
            wo = mla_wo[j]
        else:
            scale = GQA_HEAD_DIM ** -0.5 * LOG2_E
            rep = GQA_HEADS // GQA_KV_HEADS
            nq_cols = GQA_HEADS * GQA_HEAD_DIM
            nk_cols = GQA_KV_HEADS * GQA_HEAD_DIM
            bn = nk_cols
            gain = jnp.concatenate([jnp.tile(gqa_q_norm[j], GQA_HEADS) * scale, jnp.tile(gqa_k_norm[j], GQA_KV_HEADS),
                                    jnp.ones((nk_cols,), F32)]).reshape(1, -1)
            qkv = _norm_matmul(
                xs, 0, d, mix_norm[i], mod, (1, 0), gqa_wqkv[j].astype(BF16), bn,
                [gain, cos_c, sa_c, sb_c], [pl.BlockSpec((1, bn), lambda m, n: (0, n))] + rope_specs,
                jax.ShapeDtypeStruct((rows, nq_cols + 2 * nk_cols), BF16), pl.BlockSpec((bm, bn), lambda m, n: (m, n)),
                functools.partial(_epi_gqa, n_qk_tiles=(nq_cols + nk_cols) // bn), all_tiles, mod_row, "gqa_qkv")
            dh = GQA_HEAD_DIM
            o = _attention_t(qkv, qkv, qkv, batch=batch, seq=seq, ctx_len=ctx_len, heads=GQA_HEADS,
                             groups=GQA_KV_HEADS, dk=dh, dv=dh, qcol0=0, kcol0=GQA_HEADS,
                             vcol0=GQA_HEADS + GQA_KV_HEADS, bq=min(512, seq), bk=512, chunk=256,
                             out_cols=nq_cols, name="gqa_attention")
            if not last:
                o = _attention(qkv, qkv, qkv, o, batch=batch, groups=GQA_KV_HEADS, rep=rep, dk=dh, dv=dh,
                               qcol0=0, kcol0=GQA_HEADS, vcol0=GQA_HEADS + GQA_KV_HEADS,
                               q_blk0=lat_rows // ctx_len, n_q=1, bq=ctx_len,
                               kv_blk0=lat_rows // ctx_len, kv_len=ctx_len, bk=ctx_len,
                               ctx_blk0=None, ctx_len=None, out_cols=nq_cols, name="gqa_ctx_attention",
                               log2_domain=True)
            wo = gqa_wo[j]

        xs = _matmul_residual(o, wo.astype(BF16), xs, mod, 2, d // 2, n_tiles, mod_row, "attn_out")
        a = _ffn_up(xs, ffn_norm[i], mod, ffn_up[i].astype(BF16), ffn_conv_w[i], ffn_conv_b[i], 512,
                    n_tiles, mod_row, lat_rows, seq, ctx_len, "ffn_up")
        xs = _matmul_residual(a, ffn_down[i].astype(BF16), xs, mod, 5, d // 4, n_tiles, mod_row, "ffn_down")

    out = _final_norm(xs, final_norm, lat_tiles)
    return out.reshape(batch, seq, d)
```

```python
import functools

import jax
import jax.numpy as jnp
from jax import lax
from jax.experimental import pallas as pl
from jax.experimental.pallas import tpu as pltpu

F32 = jnp.float32
BF16 = jnp.bfloat16

GRID_W = 64
NA_HEADS = 16
NA_WIN_ROWS = 8
NA_WIN_COLS = 16
MLA_HEADS = 16
MLA_NOPE_DIM = 128
MLA_ROPE_DIM = 64
MLA_V_DIM = 128
GQA_HEADS = 16
GQA_KV_HEADS = 4
GQA_HEAD_DIM = 128
CONV_WIDTH = 3
ROPE_THETA = 10000.0
NORM_EPS = 1e-6
N_MIXERS = 3
LOG2_E = 1.4426950408889634

LANES = 128
ROW_TILE = 1024
HALO = 8
ONES_ROWS = 16
FLASH_UNROLL = 8
FLASH_LEAD = 2
NA_TILE_ROWS = 4
MASK_VALUE = -1e30
VMEM_LIMIT = 56 * 1024 * 1024


def _params(*sem):
    return pltpu.CompilerParams(dimension_semantics=sem, vmem_limit_bytes=VMEM_LIMIT)


def _rms(x, g):
    y = x * lax.rsqrt(jnp.mean(x * x, axis=-1, keepdims=True) + NORM_EPS)
    return y * g


def _rope(y, cos, sin_a, sin_b, quarter):
    return (y * cos + pltpu.roll(y, LANES - quarter, 1) * sin_a
            + pltpu.roll(y, quarter, 1) * sin_b)


def _mod_kernel(c_ref, w_ref, b_ref, o_ref):
    s = jax.nn.silu(c_ref[...])
    o_ref[0] = jnp.dot(s.astype(BF16), w_ref[0].astype(BF16), preferred_element_type=F32) + b_ref[0]


def _modulation(c_rows, ada_w, ada_b):
    depth, d, n = ada_w.shape
    bn = n // 8
    return pl.pallas_call(
        _mod_kernel,
        grid=(depth, n // bn),
        in_specs=[pl.BlockSpec((8, d), lambda l, j: (0, 0)),
                  pl.BlockSpec((1, d, bn), lambda l, j: (l, 0, j)),
                  pl.BlockSpec((1, 1, bn), lambda l, j: (l, 0, j))],
        out_specs=pl.BlockSpec((1, 8, bn), lambda l, j: (l, 0, j)),
        out_shape=jax.ShapeDtypeStruct((depth, 8, n), F32),
        compiler_params=_params("arbitrary", "arbitrary"),
        name="adaln_mod",
    )(c_rows, ada_w, ada_b.reshape(depth, 1, n))


def _norm_matmul_kernel(*refs, has_mod, n_extra, epilogue):
    if has_mod:
        x_ref, g_ref, sc_ref, sh_ref, w_ref = refs[:5]
        rest = refs[5:]
    else:
        x_ref, g_ref, w_ref = refs[:3]
        rest = refs[3:]
    extras, outs, h_sc = rest[:n_extra], rest[n_extra:-1], rest[-1]
    n = pl.program_id(1)

    @pl.when(n == 0)
    def _():
        h = _rms(x_ref[...], g_ref[...])
        if has_mod:
            h = h * (1.0 + sc_ref[0]) + sh_ref[0]
        h_sc[...] = h.astype(BF16)

    acc = jnp.dot(h_sc[...], w_ref[...], preferred_element_type=F32)
    epilogue(acc, n, extras, outs)


def _norm_matmul(x, xcol, kdim, g, mod, mod_chunks, w, bn, extras, extra_specs, out_shapes, out_specs,
                 epilogue, n_row_tiles, mod_row, name):
    bm = ROW_TILE
    n_col = w.shape[1] // bn
    has_mod = mod is not None
    in_specs = [pl.BlockSpec((bm, kdim), lambda m, n: (m, xcol)),
                pl.BlockSpec((1, kdim), lambda m, n: (0, 0))]
    args = [x, g.reshape(1, kdim)]
    if has_mod:
        sc_chunk, sh_chunk = mod_chunks
        in_specs += [pl.BlockSpec((1, 1, kdim), lambda m, n: (mod_row(m), 0, sc_chunk)),
                     pl.BlockSpec((1, 1, kdim), lambda m, n: (mod_row(m), 0, sh_chunk))]
        args += [mod, mod]
    in_specs.append(pl.BlockSpec((kdim, bn), lambda m, n: (0, n)))
    args.append(w)
    in_specs += extra_specs
    args += extras
    kern = functools.partial(_norm_matmul_kernel, has_mod=has_mod, n_extra=len(extras), epilogue=epilogue)
    return pl.pallas_call(
        kern,
        grid=(n_row_tiles, n_col),
        in_specs=in_specs,
        out_specs=out_specs,
        out_shape=out_shapes,
        scratch_shapes=[pltpu.VMEM((bm, kdim), BF16)],
        compiler_params=_params("arbitrary", "arbitrary"),
        name=name,
    )(*args)


def _epi_colscale(acc, n, extras, outs):
    (cs_ref,), (o_ref,) = extras, outs
    o_ref[...] = (acc * cs_ref[...]).astype(o_ref.dtype)


def _epi_plain(acc, n, extras, outs):
    (o_ref,) = outs
    o_ref[...] = acc.astype(o_ref.dtype)


def _epi_mla_a(acc, n, extras, outs):
    cos_ref, sa_ref, sb_ref = extras
    qc_ref, kpe_ref = outs
    wide = qc_ref.shape[1]
    qc_ref[...] = acc[:, :wide]
    kpe = _rope(acc[:, wide:], cos_ref[...], sa_ref[...], sb_ref[...], MLA_ROPE_DIM // 4)
    kpe_ref[...] = kpe.astype(kpe_ref.dtype)


def _epi_mla_q(acc, n, extras, outs, *, scale):
    cos_ref, sa_ref, sb_ref = extras
    (o_ref,) = outs
    for c in range(acc.shape[1] // LANES):
        y = acc[:, c * LANES:(c + 1) * LANES]
        if c % 2 == 1:
            y = _rope(y, cos_ref[...], sa_ref[...], sb_ref[...], MLA_ROPE_DIM // 4)
        o_ref[:, c * LANES:(c + 1) * LANES] = (y * scale).astype(o_ref.dtype)


def _epi_mla_k(acc, n, extras, outs):
    (kpe_ref,), (o_ref,) = extras, outs
    kpe = kpe_ref[...]
    for c in range(acc.shape[1] // LANES):
        o_ref[:, (2 * c) * LANES:(2 * c + 1) * LANES] = acc[:, c * LANES:(c + 1) * LANES].astype(o_ref.dtype)
        o_ref[:, (2 * c + 1) * LANES:(2 * c + 2) * LANES] = kpe


def _epi_gqa(acc, n, extras, outs, *, n_qk_tiles):
    gain_ref, cos_ref, sa_ref, sb_ref = extras
    (o_ref,) = outs

    @pl.when(n < n_qk_tiles)
    def _():
        for c in range(acc.shape[1] // LANES):
            y = _rms(acc[:, c * LANES:(c + 1) * LANES], gain_ref[:, c * LANES:(c + 1) * LANES])
            y = _rope(y, cos_ref[...], sa_ref[...], sb_ref[...], GQA_HEAD_DIM // 4)
            o_ref[:, c * LANES:(c + 1) * LANES] = y.astype(o_ref.dtype)

    @pl.when(n >= n_qk_tiles)
    def _():
        o_ref[...] = acc.astype(o_ref.dtype)


def _mm_res_kernel(a_ref, w_ref, x_ref, g_ref, o_ref):
    acc = jnp.dot(a_ref[...], w_ref[...], preferred_element_type=F32)
    o_ref[...] = x_ref[...] + g_ref[0] * acc


def _matmul_residual(a, w, x, mod, gate_chunk, bn, n_row_tiles, mod_row, name):
    bm = ROW_TILE
    kdim, n_out = w.shape
    assert n_out == x.shape[1] and n_out % bn == 0
    per_chunk = n_out // bn
    return pl.pallas_call(
        _mm_res_kernel,
        grid=(n_row_tiles, per_chunk),
        in_specs=[pl.BlockSpec((bm, kdim), lambda m, n: (m, 0)),
                  pl.BlockSpec((kdim, bn), lambda m, n: (0, n)),
                  pl.BlockSpec((bm, bn), lambda m, n: (m, n)),
                  pl.BlockSpec((1, 1, bn), lambda m, n: (mod_row(m), 0, gate_chunk * per_chunk + n))],
        out_specs=pl.BlockSpec((bm, bn), lambda m, n: (m, n)),
        out_shape=jax.ShapeDtypeStruct(x.shape, x.dtype),
        input_output_aliases={2: 0},
        compiler_params=_params("arbitrary", "arbitrary"),
        name=name,
    )(a, w, x, mod)


def _ffn_up_kernel(x_ref, xp_ref, xn_ref, g_ref, sc_ref, sh_ref, wg_ref, wv_ref, cw_ref, cb_ref, o_ref, h_sc,
                   *, lat_rows, lat_seq, ctx_seq):
    bm = x_ref.shape[0]
    m = pl.program_id(0)

    @pl.when(pl.program_id(1) == 0)
    def _():
        def nm(x):
            return (_rms(x, g_ref[...]) * (1.0 + sc_ref[0]) + sh_ref[0]).astype(BF16)
        h_sc[0:bm, :] = nm(x_ref[...])
        h_sc[bm:bm + 2 * HALO, :] = nm(jnp.concatenate([xn_ref[...], xp_ref[...]], axis=0))

    ext = bm + 2 * HALO
    row = m * bm + lax.broadcasted_iota(jnp.int32, (bm, 1), 0)
    in_ctx = row >= lat_rows
    pos = jnp.where(in_ctx, (row - lat_rows) & (ctx_seq - 1), row & (lat_seq - 1))
    seq = jnp.where(in_ctx, ctx_seq, lat_seq)
    gate = jnp.dot(h_sc[...], wg_ref[...], preferred_element_type=F32)
    val = jnp.dot(h_sc[0:bm, :], wv_ref[...], preferred_element_type=F32)
    g_prev = pltpu.roll(gate, 1, 0)[0:bm]
    g_next = pltpu.roll(gate, ext - 1, 0)[0:bm]
    g_cur = gate[0:bm]
    g_prev = jnp.where(pos == 0, 0.0, g_prev)
    g_next = jnp.where(pos == seq - 1, 0.0, g_next)
    cw = cw_ref[...]
    z = cb_ref[...] + g_prev * cw[0:1] + g_cur * cw[1:2] + g_next * cw[2:3]
    o_ref[...] = (jax.nn.silu(z) * val).astype(o_ref.dtype)


def _ffn_up(x, g, mod, w_up, conv_w, conv_b, bn, n_row_tiles, mod_row, lat_rows, lat_seq, ctx_seq, name):
    bm = ROW_TILE
    rows, d = x.shape
    f = w_up.shape[1] // 2
    n_col = f // bn
    last_halo = rows // HALO - 1
    per = bm // HALO
    kern = functools.partial(_ffn_up_kernel, lat_rows=lat_rows, lat_seq=lat_seq, ctx_seq=ctx_seq)
    return pl.pallas_call(
        kern,
        grid=(n_row_tiles, n_col),
        in_specs=[pl.BlockSpec((bm, d), lambda m, n: (m, 0)),
                  pl.BlockSpec((HALO, d), lambda m, n: (jnp.maximum(m * per - 1, 0), 0)),
                  pl.BlockSpec((HALO, d), lambda m, n: (jnp.minimum((m + 1) * per, last_halo), 0)),
                  pl.BlockSpec((1, d), lambda m, n: (0, 0)),
                  pl.BlockSpec((1, 1, d), lambda m, n: (mod_row(m), 0, 4)),
                  pl.BlockSpec((1, 1, d), lambda m, n: (mod_row(m), 0, 3)),
                  pl.BlockSpec((d, bn), lambda m, n: (0, n)),
                  pl.BlockSpec((d, bn), lambda m, n: (0, n_col + n)),
                  pl.BlockSpec((CONV_WIDTH, bn), lambda m, n: (0, n)),
                  pl.BlockSpec((1, bn), lambda m, n: (0, n))],
        out_specs=pl.BlockSpec((bm, bn), lambda m, n: (m, n)),
        out_shape=jax.ShapeDtypeStruct((rows, f), BF16),
        scratch_shapes=[pltpu.VMEM((bm + 2 * HALO, d), BF16)],
        compiler_params=_params("arbitrary", "arbitrary"),
        name=name,
    )(x, x, x, g.reshape(1, d), mod, mod, w_up, w_up, conv_w, conv_b.reshape(1, f))


def _flash_t_kernel(qt_ref, k_ref, vt_ref, kc_ref, vct_ref, o_ref, m_sc, acc_sc, s_sc, mx_sc, *, rep, bk, n_kv,
                    chunk, unroll, lead):
    bq = qt_ref.shape[2]
    dv = vt_ref.shape[3] - ONES_ROWS
    n_chunks = rep * bq // chunk

    def scores(k, c, half, n_keys):
        r, off = divmod(c * chunk, bq)
        s = jnp.dot(k, qt_ref[r, :, off:off + chunk], preferred_element_type=F32)
        s_sc[half, c, 0:n_keys, :] = s
        mx_sc[half, c] = jnp.max(s, axis=0, keepdims=True)

    def softmax_pv(half, n_keys, vt, c, is_first):
        cs = slice(c * chunk, (c + 1) * chunk)
        s = s_sc[half, c, 0:n_keys, :]
        mx = mx_sc[half, c]
        if is_first:
            m_new = mx
            p = jnp.exp2(s - m_new)
            acc_sc[:, cs] = jnp.dot(vt, p.astype(vt.dtype), preferred_element_type=F32)
        else:
            m_prev = m_sc[:, cs]
            m_new = jnp.maximum(m_prev, mx)
            alpha = jnp.exp2(m_prev - m_new)
            p = jnp.exp2(s - m_new)
            acc_sc[:, cs] = alpha * acc_sc[:, cs] + jnp.dot(vt, p.astype(vt.dtype), preferred_element_type=F32)
        m_sc[:, cs] = m_new

    kc = kc_ref[...]
    k0 = k_ref[0:bk, :]
    n_ctx = kc_ref.shape[0]
    assert n_ctx <= bk
    for c in range(n_chunks):
        scores(kc, c, 1, n_ctx)
    for c in range(n_chunks):
        scores(k0, c, 0, bk)
        softmax_pv(1, n_ctx, vct_ref[0, 0], c, True)

    def body(i, carry):
        def issue_scores(t):
            u, c = divmod(t, n_chunks)
            j_next = jnp.minimum(unroll * i + u + 1, n_kv - 1)
            off = pl.multiple_of(j_next * bk, bk)
            scores(k_ref[pl.ds(off, bk), :], c, (u + 1) % 2, bk)

        def consume(t):
            u, c = divmod(t, n_chunks)
            softmax_pv(u % 2, bk, vt_ref[0, 0, unroll * i + u], c, False)

        n_tasks = unroll * n_chunks
        for t in range(n_tasks + lead):
            if t < n_tasks:
                issue_scores(t)
            if t >= lead:
                consume(t - lead)
        return carry

    assert unroll % 2 == 0 and n_kv % unroll == 0 and lead < n_chunks
    lax.fori_loop(0, n_kv // unroll, body, 0)
    for r in range(rep):
        ot = acc_sc[0:dv, r * bq:(r + 1) * bq] / acc_sc[dv:dv + 1, r * bq:(r + 1) * bq]
        o_ref[:, r * dv:(r + 1) * dv] = ot.T.astype(o_ref.dtype)


def _attention_t(q_arr, k_arr, v_arr, *, batch, seq, ctx_len, heads, groups, dk, dv, qcol0, kcol0, vcol0,
                 bq, bk, chunk, out_cols, name):
    rows = q_arr.shape[0]
    lat_rows = batch * seq
    rep = heads // groups
    n_q, n_kv = seq // bq, seq // bk
    ctx_blk0 = lat_rows // ctx_len
    qt = q_arr[:, qcol0 * dk:(qcol0 + heads) * dk].reshape(rows, heads, dk).transpose(1, 2, 0)
    v_lat = v_arr[:lat_rows, vcol0 * dv:(vcol0 + groups) * dv]
    vt = v_lat.reshape(batch, n_kv, bk, groups, dv).transpose(0, 3, 1, 4, 2)
    vt = jnp.concatenate([vt, jnp.ones(vt.shape[:3] + (ONES_ROWS, bk), vt.dtype)], axis=3)
    v_ctx = v_arr[lat_rows:lat_rows + batch * ctx_len, vcol0 * dv:(vcol0 + groups) * dv]
    vct = v_ctx.reshape(batch, ctx_len, groups, dv).transpose(0, 2, 3, 1)
    vct = jnp.concatenate([vct, jnp.ones(vct.shape[:2] + (ONES_ROWS, ctx_len), vct.dtype)], axis=2)
    dve = dv + ONES_ROWS
    kern = functools.partial(_flash_t_kernel, rep=rep, bk=bk, n_kv=n_kv, chunk=chunk,
                             unroll=min(FLASH_UNROLL, n_kv), lead=FLASH_LEAD)
    return pl.pallas_call(
        kern,
        grid=(batch, groups, n_q),
        in_specs=[pl.BlockSpec((rep, dk, bq), lambda b, g, i: (g, 0, b * n_q + i)),
                  pl.BlockSpec((seq, dk), lambda b, g, i: (b, kcol0 + g)),
                  pl.BlockSpec((1, 1, n_kv, dve, bk), lambda b, g, i: (b, g, 0, 0, 0)),
                  pl.BlockSpec((ctx_len, dk), lambda b, g, i: (ctx_blk0 + b, kcol0 + g)),
                  pl.BlockSpec((1, 1, dve, ctx_len), lambda b, g, i: (b, g, 0, 0))],
        out_specs=pl.BlockSpec((bq, rep * dv), lambda b, g, i: (b * n_q + i, g)),
        out_shape=jax.ShapeDtypeStruct((rows, out_cols), BF16),
        scratch_shapes=[pltpu.VMEM((1, rep * bq), F32), pltpu.VMEM((dve, rep * bq), F32),
                        pltpu.VMEM((2, rep * bq // chunk, bk, chunk), F32),
                        pltpu.VMEM((2, rep * bq // chunk, 1, chunk), F32)],
        compiler_params=_params("arbitrary", "arbitrary", "arbitrary"),
        name=name,
    )(qt, k_arr, vt, k_arr, vct)


def _flash_kernel(*refs, rep, dk, dv, bk, n_kv, has_ctx, log2_domain):
    if has_ctx:
        q_ref, k_ref, v_ref, kc_ref, vc_ref, o_ref, m_sc, l_sc, acc_sc = refs
    else:
        q_ref, k_ref, v_ref, o_ref, m_sc, l_sc, acc_sc = refs
    ex = jnp.exp2 if log2_domain else jnp.exp
    bq = q_ref.shape[0]
    if rep > 1:
        q = jnp.concatenate([q_ref[:, r * dk:(r + 1) * dk] for r in range(rep)], axis=0)
    else:
        q = q_ref[...]

    def scores(k):
        return lax.dot_general(q, k, (((1,), (1,)), ((), ())), preferred_element_type=F32)

    def first(k, v):
        s = scores(k)
        m = jnp.max(s, axis=-1, keepdims=True)
        p = ex(s - m)
        m_sc[...] = m
        l_sc[...] = jnp.sum(p, axis=-1, keepdims=True)
        acc_sc[...] = jnp.dot(p.astype(v.dtype), v, preferred_element_type=F32)

    def update(k, v):
        s = scores(k)
        m_prev = m_sc[...]
        m_new = jnp.maximum(m_prev, jnp.max(s, axis=-1, keepdims=True))
        alpha = ex(m_prev - m_new)
        p = ex(s - m_new)
        l_sc[...] = alpha * l_sc[...] + jnp.sum(p, axis=-1, keepdims=True)
        acc_sc[...] = alpha * acc_sc[...] + jnp.dot(p.astype(v.dtype), v, preferred_element_type=F32)
        m_sc[...] = m_new

    if has_ctx:
        first(kc_ref[...], vc_ref[...])
        start = 0
    else:
        first(k_ref[0:bk, :], v_ref[0:bk, :])
        start = 1

    def body(j, carry):
        off = pl.multiple_of(j * bk, bk)
        update(k_ref[pl.ds(off, bk), :], v_ref[pl.ds(off, bk), :])
        return carry

    lax.fori_loop(start, n_kv, body, 0)
    o = acc_sc[...] / l_sc[...]
    for r in range(rep):
        o_ref[:, r * dv:(r + 1) * dv] = o[r * bq:(r + 1) * bq].astype(o_ref.dtype)


def _attention(q_arr, k_arr, v_arr, o_prev, *, batch, groups, rep, dk, dv, qcol0, kcol0, vcol0,
               q_blk0, n_q, bq, kv_blk0, kv_len, bk, ctx_blk0, ctx_len, out_cols, name, log2_domain):
    has_ctx = ctx_blk0 is not None
    rows = q_arr.shape[0]
    in_specs = [pl.BlockSpec((bq, rep * dk), lambda b, g, i: (q_blk0 + b * n_q + i, qcol0 + g)),
                pl.BlockSpec((kv_len, dk), lambda b, g, i: (kv_blk0 + b, kcol0 + g)),
                pl.BlockSpec((kv_len, dv), lambda b, g, i: (kv_blk0 + b, vcol0 + g))]
    args = [q_arr, k_arr, v_arr]
    if has_ctx:
        in_specs += [pl.BlockSpec((ctx_len, dk), lambda b, g, i: (ctx_blk0 + b, kcol0 + g)),
                     pl.BlockSpec((ctx_len, dv), lambda b, g, i: (ctx_blk0 + b, vcol0 + g))]
        args += [k_arr, v_arr]
    aliases = {}
    if o_prev is not None:
        in_specs.append(pl.BlockSpec(memory_space=pl.ANY))
        args.append(o_prev)
        aliases = {len(args) - 1: 0}
    kern = functools.partial(_flash_kernel, rep=rep, dk=dk, dv=dv, bk=bk, n_kv=kv_len // bk, has_ctx=has_ctx,
                             log2_domain=log2_domain)
    if o_prev is not None:
        inner = kern

        def kern(*refs):
            n_in = 5 if has_ctx else 3
            return inner(*refs[:n_in], *refs[n_in + 1:])
    return pl.pallas_call(
        kern,
        grid=(batch, groups, n_q),
        in_specs=in_specs,
        out_specs=pl.BlockSpec((bq, rep * dv), lambda b, g, i: (q_blk0 + b * n_q + i, g)),
        out_shape=jax.ShapeDtypeStruct((rows, out_cols), BF16),
        scratch_shapes=[pltpu.VMEM((rep * bq, 1), F32), pltpu.VMEM((rep * bq, 1), F32),
                        pltpu.VMEM((rep * bq, dv), F32)],
        input_output_aliases=aliases,
        compiler_params=_params("arbitrary", "arbitrary", "arbitrary"),
        name=name,
    )(*args)


def _rpb_table_kernel(rpb_ref, o_ref, *, n_tiles, grid_rows):
    var = pl.program_id(0)
    h = pl.program_id(1)
    t_rep = jnp.where(var == 0, 0, jnp.where(var == 1, 1, n_tiles - 1))
    qc = lax.broadcasted_iota(jnp.int32, (GRID_W, GRID_W), 0)
    kc = lax.broadcasted_iota(jnp.int32, (GRID_W, GRID_W), 1)
    dc = kc - qc
    cstart = jnp.clip(qc - NA_WIN_COLS // 2, 0, GRID_W - NA_WIN_COLS)
    valid_c = (kc >= cstart) & (kc < cstart + NA_WIN_COLS)
    n_dc = 2 * NA_WIN_COLS - 1
    masked = jnp.full((GRID_W, GRID_W), MASK_VALUE, F32)
    toeplitz = {}
    for dr in range(-(NA_WIN_ROWS - 1), NA_WIN_ROWS):
        acc = jnp.zeros((GRID_W, GRID_W), F32)
        for b in range(n_dc):
            bias = rpb_ref[h, (dr + NA_WIN_ROWS - 1) * n_dc + b] * LOG2_E
            acc = jnp.where(dc == b - (NA_WIN_COLS - 1), bias, acc)
        toeplitz[dr] = jnp.where(valid_c, acc, MASK_VALUE)
    for ql in range(NA_TILE_ROWS):
        qr = NA_TILE_ROWS * t_rep + ql
        rstart = jnp.clip(qr - NA_WIN_ROWS // 2, 0, grid_rows - NA_WIN_ROWS)
        for kl in range(3 * NA_TILE_ROWS):
            kr = NA_TILE_ROWS * (t_rep - 1) + kl
            dr = kl - NA_TILE_ROWS - ql
            if abs(dr) > NA_WIN_ROWS - 1:
                blk = masked
            else:
                ok = ((kr >= rstart) & (kr < rstart + NA_WIN_ROWS)).astype(F32)
                blk = toeplitz[dr] * ok + MASK_VALUE * (1.0 - ok)
            o_ref[0, 0, ql * GRID_W:(ql + 1) * GRID_W, kl * GRID_W:(kl + 1) * GRID_W] = blk


def _rpb_table(rpb, n_tiles, grid_rows):
    heads = rpb.shape[0]
    tq = NA_TILE_ROWS * GRID_W
    kern = functools.partial(_rpb_table_kernel, n_tiles=n_tiles, grid_rows=grid_rows)
    return pl.pallas_call(
        kern,
        grid=(3, heads),
        in_specs=[pl.BlockSpec(memory_space=pltpu.SMEM)],
        out_specs=pl.BlockSpec((1, 1, tq, 3 * tq), lambda v, h: (v, h, 0, 0)),
        out_shape=jax.ShapeDtypeStruct((3, heads, tq, 3 * tq), F32),
        compiler_params=_params("arbitrary", "arbitrary"),
        name="na_rpb_table",
    )(rpb.reshape(heads, -1))


def _na_kernel(q_ref, kp_ref, kc_ref, kn_ref, kx_ref, vp_ref, vc_ref, vn_ref, vx_ref, tab_ref, o_ref, *, heads, dh):
    tq = q_ref.shape[0]
    k_refs = (kp_ref, kc_ref, kn_ref, kx_ref)
    v_refs = (vp_ref, vc_ref, vn_ref, vx_ref)
    ones = jnp.ones((tq, dh), BF16)

    def scores(h):
        sl = slice(h * dh, (h + 1) * dh)
        q = q_ref[:, sl]
        return [lax.dot_general(q, k_ref[:, sl], (((1,), (1,)), ((), ())), preferred_element_type=F32)
                for k_ref in k_refs]

    s_next = scores(0)
    for h in range(heads):
        sl = slice(h * dh, (h + 1) * dh)
        s = s_next
        if h + 1 < heads:
            s_next = scores(h + 1)
        s = [s[j] + tab_ref[0, h, :, j * tq:(j + 1) * tq] for j in range(3)] + [s[3]]
        m = jnp.max(jnp.maximum(jnp.maximum(s[0], s[1]), jnp.maximum(s[2], s[3])), axis=-1, keepdims=True)
        o = None
        for sj, v_ref in zip(s, v_refs):
            p = jnp.exp2(sj - m).astype(BF16)
            part = jnp.dot(p, jnp.concatenate([v_ref[:, sl], ones], axis=1), preferred_element_type=F32)
            o = part if o is None else o + part
        o_ref[:, sl] = (o[:, 0:dh] / o[:, dh:2 * dh]).astype(o_ref.dtype)


def _neighborhood_attention(qkv, table, *, batch, seq, ctx_len, heads, dh, hb):
    rows = qkv.shape[0]
    tq = NA_TILE_ROWS * GRID_W
    assert ctx_len == tq and seq % tq == 0
    n_tiles = seq // tq
    n_hg = heads // hb
    ctx_blk0 = batch * seq // ctx_len
    wb = hb * dh

    def var(t):
        return jnp.where(t == 0, 0, jnp.where(t == n_tiles - 1, 2, 1))

    def spec(col0, shift):
        if shift is None:
            return pl.BlockSpec((tq, wb), lambda b, g, t: (ctx_blk0 + b, col0 + g))
        return pl.BlockSpec((tq, wb), lambda b, g, t: (b * n_tiles + jnp.clip(t + shift, 0, n_tiles - 1), col0 + g))

    kern = functools.partial(_na_kernel, heads=hb, dh=dh)
    return pl.pallas_call(
        kern,
        grid=(batch, n_hg, n_tiles),
        in_specs=[spec(0, 0),
                  spec(n_hg, -1), spec(n_hg, 0), spec(n_hg, 1), spec(n_hg, None),
                  spec(2 * n_hg, -1), spec(2 * n_hg, 0), spec(2 * n_hg, 1), spec(2 * n_hg, None),
                  pl.BlockSpec((1, hb, tq, 3 * tq), lambda b, g, t: (var(t), g, 0, 0))],
        out_specs=pl.BlockSpec((tq, wb), lambda b, g, t: (b * n_tiles + t, g)),
        out_shape=jax.ShapeDtypeStruct((rows, heads * dh), BF16),
        compiler_params=_params("arbitrary", "arbitrary", "arbitrary"),
        name="na_attention",
    )(qkv, qkv, qkv, qkv, qkv, qkv, qkv, qkv, qkv, table)


def _final_norm_kernel(x_ref, g_ref, o_ref):
    o_ref[...] = _rms(x_ref[...], g_ref[...])


def _final_norm(x, g, n_row_tiles):
    bm = ROW_TILE
    d = x.shape[1]
    return pl.pallas_call(
        _final_norm_kernel,
        grid=(n_row_tiles,),
        in_specs=[pl.BlockSpec((bm, d), lambda m: (m, 0)), pl.BlockSpec((1, d), lambda m: (0, 0))],
        out_specs=pl.BlockSpec((bm, d), lambda m: (m, 0)),
        out_shape=jax.ShapeDtypeStruct((n_row_tiles * bm, d), F32),
        compiler_params=_params("arbitrary"),
        name="final_norm",
    )(x, g.reshape(1, d))


def _rope_tables(n_tok, rot_dim, batch, n_ctx_rows):
    quarter = rot_dim // 4
    t = jnp.arange(n_tok)
    row = (t // GRID_W).astype(F32)
    col = (t % GRID_W).astype(F32)
    inv = ROPE_THETA ** (-jnp.arange(quarter, dtype=F32) / quarter)
    ar = row[:, None] * inv
    ac = col[:, None] * inv
    ang = jnp.concatenate([ar, ar, ac, ac], axis=-1)
    cos, sin = jnp.cos(ang), jnp.sin(ang)
    first = (jnp.arange(rot_dim) % (2 * quarter)) < quarter
    sin_a = jnp.where(first, -sin, 0.0)
    sin_b = jnp.where(first, 0.0, sin)
    pad = LANES - rot_dim

    def full(tab, fill):
        tab = jnp.pad(tab, ((0, 0), (0, pad)), constant_values=fill)
        tab = jnp.tile(tab, (batch, 1))
        return jnp.concatenate([tab, jnp.full((n_ctx_rows, LANES), fill, F32)], axis=0)

    return full(cos, 1.0), full(sin_a, 0.0), full(sin_b, 0.0)


def kernel(x, c, ctx, c_ctx, ada_w, ada_b, mix_norm, ffn_norm, ffn_up, ffn_conv_w, ffn_conv_b, ffn_down, na_wqkv, na_rpb, na_wo, mla_wq_a, mla_q_norm, mla_wq_b, mla_wkv_a, mla_kv_norm, mla_wkv_b, mla_wo, gqa_wqkv, gqa_q_norm, gqa_k_norm, gqa_wo, final_norm):
    batch, seq, d = x.shape
    ctx_len = ctx.shape[1]
    depth = ada_w.shape[0]
    ffn_dim = ffn_down.shape[1]
    bm = ROW_TILE
    lat_rows = batch * seq
    ctx_rows = batch * ctx_len
    pad_rows = -(lat_rows + ctx_rows) % bm
    rows = lat_rows + ctx_rows + pad_rows
    assert seq % bm == 0 and batch < 8
    assert seq & (seq - 1) == 0 and ctx_len & (ctx_len - 1) == 0
    tiles_per_batch = seq // bm
    lat_tiles = lat_rows // bm
    all_tiles = rows // bm
    grid_rows = seq // GRID_W

    def mod_row(m):
        return jnp.minimum(m // tiles_per_batch, batch)

    xs = jnp.concatenate([x.reshape(lat_rows, d), ctx.reshape(ctx_rows, d), jnp.zeros((pad_rows, d), F32)], axis=0)
    c_rows = jnp.zeros((8, d), F32).at[:batch].set(c).at[batch].set(c_ctx)
    mod_all = _modulation(c_rows, ada_w, ada_b)

    cos_b, sa_b, sb_b = _rope_tables(seq, MLA_ROPE_DIM, batch, ctx_rows + pad_rows)
    cos_c, sa_c, sb_c = _rope_tables(seq, GQA_HEAD_DIM, batch, ctx_rows + pad_rows)

    def row_spec(width):
        return pl.BlockSpec((bm, width), lambda m, n: (m, 0))

    rope_specs = [row_spec(LANES)] * 3

    for i in range(depth):
        last = i == depth - 1
        kind, j = i % N_MIXERS, i // N_MIXERS
        mod = mod_all[i].reshape(8, 1, 6 * d)
        n_tiles = lat_tiles if last else all_tiles

        if kind == 0:
            scale = (d // NA_HEADS) ** -0.5 * LOG2_E
            colscale = jnp.concatenate([jnp.full((1, d), scale, F32), jnp.ones((1, 2 * d), F32)], axis=1)
            bn = d // 2
            qkv = _norm_matmul(
                xs, 0, d, mix_norm[i], mod, (1, 0), na_wqkv[j].astype(BF16), bn,
                [colscale], [pl.BlockSpec((1, bn), lambda m, n: (0, n))],
                jax.ShapeDtypeStruct((rows, 3 * d), BF16), pl.BlockSpec((bm, bn), lambda m, n: (m, n)),
                _epi_colscale, all_tiles, mod_row, "na_qkv")
            table = _rpb_table(na_rpb[j], seq // (NA_TILE_ROWS * GRID_W), grid_rows)
            dh = d // NA_HEADS
            o = _neighborhood_attention(qkv, table, batch=batch, seq=seq, ctx_len=ctx_len,
                                        heads=NA_HEADS, dh=dh, hb=8)
            if not last:
                o = _attention(qkv, qkv, qkv, o, batch=batch, groups=NA_HEADS, rep=1, dk=dh, dv=dh,
                               qcol0=0, kcol0=NA_HEADS, vcol0=2 * NA_HEADS,
                               q_blk0=lat_rows // ctx_len, n_q=1, bq=ctx_len,
                               kv_blk0=lat_rows // ctx_len, kv_len=ctx_len, bk=ctx_len,
                               ctx_blk0=None, ctx_len=None, out_cols=d, name="na_ctx_attention",
                               log2_domain=True)
            wo = na_wo[j]
        elif kind == 1:
            scale = (MLA_NOPE_DIM + MLA_ROPE_DIM) ** -0.5 * LOG2_E
            q_rank = mla_wq_a.shape[2]
            kv_rank = mla_kv_norm.shape[1]
            hq = 2 * LANES
            w1 = jnp.concatenate([mla_wq_a[j], mla_wkv_a[j], jnp.zeros((d, LANES - MLA_ROPE_DIM), F32)], axis=1)
            n1 = w1.shape[1]
            qc, kpe = _norm_matmul(
                xs, 0, d, mix_norm[i], mod, (1, 0), w1.astype(BF16), n1,
                [cos_b, sa_b, sb_b], rope_specs,
                (jax.ShapeDtypeStruct((rows, q_rank + kv_rank), F32), jax.ShapeDtypeStruct((rows, LANES), BF16)),
                (pl.BlockSpec((bm, q_rank + kv_rank), lambda m, n: (m, 0)), pl.BlockSpec((bm, LANES), lambda m, n: (m, 0))),
                _epi_mla_a, all_tiles, mod_row, "mla_a")
            wqb = mla_wq_b[j].reshape(q_rank, MLA_HEADS, MLA_NOPE_DIM + MLA_ROPE_DIM)
            wqb = jnp.pad(wqb, ((0, 0), (0, 0), (0, hq - MLA_NOPE_DIM - MLA_ROPE_DIM))).reshape(q_rank, MLA_HEADS * hq)
            bn = 4 * hq
            q = _norm_matmul(
                qc, 0, q_rank, mla_q_norm[j], None, None, wqb.astype(BF16), bn,
                [cos_b, sa_b, sb_b], rope_specs,
                jax.ShapeDtypeStruct((rows, MLA_HEADS * hq), BF16), pl.BlockSpec((bm, bn), lambda m, n: (m, n)),
                functools.partial(_epi_mla_q, scale=scale), all_tiles, mod_row, "mla_q")
            wkvb = mla_wkv_b[j].reshape(kv_rank, MLA_HEADS, MLA_NOPE_DIM + MLA_V_DIM)
            wk = wkvb[:, :, :MLA_NOPE_DIM].reshape(kv_rank, MLA_HEADS * MLA_NOPE_DIM)
            wv = wkvb[:, :, MLA_NOPE_DIM:].reshape(kv_rank, MLA_HEADS * MLA_V_DIM)
            bnk = 8 * MLA_NOPE_DIM
            k = _norm_matmul(
                qc, 1, kv_rank, mla_kv_norm[j], None, None, wk.astype(BF16), bnk,
                [kpe], [row_spec(LANES)],
                jax.ShapeDtypeStruct((rows, MLA_HEADS * hq), BF16), pl.BlockSpec((bm, 2 * bnk), lambda m, n: (m, n)),
                _epi_mla_k, all_tiles, mod_row, "mla_k")
            v = _norm_matmul(
                qc, 1, kv_rank, mla_kv_norm[j], None, None, wv.astype(BF16), bnk,
                [], [],
                jax.ShapeDtypeStruct((rows, MLA_HEADS * MLA_V_DIM), BF16), pl.BlockSpec((bm, bnk), lambda m, n: (m, n)),
                _epi_plain, all_tiles, mod_row, "mla_v")
            out_cols = MLA_HEADS * MLA_V_DIM
            o = _attention_t(q, k, v, batch=batch, seq=seq, ctx_len=ctx_len, heads=MLA_HEADS, groups=MLA_HEADS,
                             dk=hq, dv=MLA_V_DIM, qcol0=0, kcol0=0, vcol0=0, bq=min(2048, seq), bk=512, chunk=256,
                             out_cols=out_cols, name="mla_attention")
            if not last:
                o = _attention(q, k, v, o, batch=batch, groups=MLA_HEADS, rep=1, dk=hq, dv=MLA_V_DIM,
                               qcol0=0, kcol0=0, vcol0=0, q_blk0=lat_rows // ctx_len, n_q=1, bq=ctx_len,
                               kv_blk0=lat_rows // ctx_len, kv_len=ctx_len, bk=ctx_len,
                               ctx_blk0=None, ctx_len=None, out_cols=out_cols, name="mla_ctx_attention",
                               log2_domain=True)
            wo = mla_wo[j]
        else:
            scale = GQA_HEAD_DIM ** -0.5 * LOG2_E
            rep = GQA_HEADS // GQA_KV_HEADS
            nq_cols = GQA_HEADS * GQA_HEAD_DIM
            nk_cols = GQA_KV_HEADS * GQA_HEAD_DIM
            bn = nk_cols
            gain = jnp.concatenate([jnp.tile(gqa_q_norm[j], GQA_HEADS) * scale, jnp.tile(gqa_k_norm[j], GQA_KV_HEADS),
                                    jnp.ones((nk_cols,), F32)]).reshape(1, -1)
            qkv = _norm_matmul(
                xs, 0, d, mix_norm[i], mod, (1, 0), gqa_wqkv[j].astype(BF16), bn,
                [gain, cos_c, sa_c, sb_c], [pl.BlockSpec((1, bn), lambda m, n: (0, n))] + rope_specs,
                jax.ShapeDtypeStruct((rows, nq_cols + 2 * nk_cols), BF16), pl.BlockSpec((bm, bn), lambda m, n: (m, n)),
                functools.partial(_epi_gqa, n_qk_tiles=(nq_cols + nk_cols) // bn), all_tiles, mod_row, "gqa_qkv")
            dh = GQA_HEAD_DIM
            o = _attention_t(qkv, qkv, qkv, batch=batch, seq=seq, ctx_len=ctx_len, heads=GQA_HEADS,
                             groups=GQA_KV_HEADS, dk=dh, dv=dh, qcol0=0, kcol0=GQA_HEADS,
                             vcol0=GQA_HEADS + GQA_KV_HEADS, bq=min(512, seq), bk=512, chunk=256,
                             out_cols=nq_cols, name="gqa_attention")
            if not last:
                o = _attention(qkv, qkv, qkv, o, batch=batch, groups=GQA_KV_HEADS, rep=rep, dk=dh, dv=dh,
                               qcol0=0, kcol0=GQA_HEADS, vcol0=GQA_HEADS + GQA_KV_HEADS,
                               q_blk0=lat_rows // ctx_len, n_q=1, bq=ctx_len,
                               kv_blk0=lat_rows // ctx_len, kv_len=ctx_len, bk=ctx_len,
                               ctx_blk0=None, ctx_len=None, out_cols=nq_cols, name="gqa_ctx_attention",
                               log2_domain=True)
            wo = gqa_wo[j]

        xs = _matmul_residual(o, wo.astype(BF16), xs, mod, 2, d // 2, n_tiles, mod_row, "attn_out")
        a = _ffn_up(xs, ffn_norm[i], mod, ffn_up[i].astype(BF16), ffn_conv_w[i], ffn_conv_b[i], 512,
                    n_tiles, mod_row, lat_rows, seq, ctx_len, "ffn_up")
        xs = _matmul_residual(a, ffn_down[i].astype(BF16), xs, mod, 5, d // 4, n_tiles, mod_row, "ffn_down")

    out = _final_norm(xs, final_norm, lat_tiles)
    return out.reshape(batch, seq, d)
```

```python
import functools

import jax
import jax.numpy as jnp
from jax import lax
from jax.experimental import pallas as pl
from jax.experimental.pallas import tpu as pltpu

F32 = jnp.float32
BF16 = jnp.bfloat16

GRID_W = 64
NA_HEADS = 16
NA_WIN_ROWS = 8
NA_WIN_COLS = 16
MLA_HEADS = 16
MLA_NOPE_DIM = 128
MLA_ROPE_DIM = 64
MLA_V_DIM = 128
GQA_HEADS = 16
GQA_KV_HEADS = 4
GQA_HEAD_DIM = 128
CONV_WIDTH = 3
ROPE_THETA = 10000.0
NORM_EPS = 1e-6
N_MIXERS = 3
LOG2_E = 1.4426950408889634

LANES = 128
ROW_TILE = 1024
HALO = 8
ONES_ROWS = 16
FLASH_KEY_BLOCK = 512
FLASH_CHUNK = 256
FLASH_Q_TILE_MLA = 2048
FLASH_Q_TILE_GQA = 512
NA_HEADS_PER_STEP = 8
FFN_COL_TILE = 512
FLASH_UNROLL = 8
FLASH_LEAD = 2
NA_TILE_ROWS = 4
MASK_VALUE = -1e30
VMEM_LIMIT = 56 * 1024 * 1024


def _params(*sem):
    return pltpu.CompilerParams(dimension_semantics=sem, vmem_limit_bytes=VMEM_LIMIT)


def _rms(x, g):
    y = x * lax.rsqrt(jnp.mean(x * x, axis=-1, keepdims=True) + NORM_EPS)
    return y * g


def _rope(y, cos, sin_a, sin_b, quarter):
    return (y * cos + pltpu.roll(y, LANES - quarter, 1) * sin_a
            + pltpu.roll(y, quarter, 1) * sin_b)


def _mod_kernel(c_ref, w_ref, b_ref, o_ref):
    s = jax.nn.silu(c_ref[...])
    o_ref[0] = jnp.dot(s.astype(BF16), w_ref[0].astype(BF16), preferred_element_type=F32) + b_ref[0]


def _modulation(c_rows, ada_w, ada_b):
    depth, d, n = ada_w.shape
    bn = n // 8
    return pl.pallas_call(
        _mod_kernel,
        grid=(depth, n // bn),
        in_specs=[pl.BlockSpec((8, d), lambda l, j: (0, 0)),
                  pl.BlockSpec((1, d, bn), lambda l, j: (l, 0, j)),
                  pl.BlockSpec((1, 1, bn), lambda l, j: (l, 0, j))],
        out_specs=pl.BlockSpec((1, 8, bn), lambda l, j: (l, 0, j)),
        out_shape=jax.ShapeDtypeStruct((depth, 8, n), F32),
        compiler_params=_params("arbitrary", "arbitrary"),
        name="adaln_mod",
    )(c_rows, ada_w, ada_b.reshape(depth, 1, n))


def _norm_matmul_kernel(*refs, has_mod, n_extra, epilogue):
    if has_mod:
        x_ref, g_ref, sc_ref, sh_ref, w_ref = refs[:5]
        rest = refs[5:]
    else:
        x_ref, g_ref, w_ref = refs[:3]
        rest = refs[3:]
    extras, outs, h_sc = rest[:n_extra], rest[n_extra:-1], rest[-1]
    n = pl.program_id(1)

    @pl.when(n == 0)
    def _():
        h = _rms(x_ref[...], g_ref[...])
        if has_mod:
            h = h * (1.0 + sc_ref[0]) + sh_ref[0]
        h_sc[...] = h.astype(BF16)

    acc = jnp.dot(h_sc[...], w_ref[...], preferred_element_type=F32)
    epilogue(acc, n, extras, outs)


def _norm_matmul(x, xcol, kdim, g, mod, mod_chunks, w, bn, extras, extra_specs, out_shapes, out_specs,
                 epilogue, n_row_tiles, mod_row, name):
    bm = ROW_TILE
    n_col = w.shape[1] // bn
    has_mod = mod is not None
    in_specs = [pl.BlockSpec((bm, kdim), lambda m, n: (m, xcol)),
                pl.BlockSpec((1, kdim), lambda m, n: (0, 0))]
    args = [x, g.reshape(1, kdim)]
    if has_mod:
        sc_chunk, sh_chunk = mod_chunks
        in_specs += [pl.BlockSpec((1, 1, kdim), lambda m, n: (mod_row(m), 0, sc_chunk)),
                     pl.BlockSpec((1, 1, kdim), lambda m, n: (mod_row(m), 0, sh_chunk))]
        args += [mod, mod]
    in_specs.append(pl.BlockSpec((kdim, bn), lambda m, n: (0, n)))
    args.append(w)
    in_specs += extra_specs
    args += extras
    kern = functools.partial(_norm_matmul_kernel, has_mod=has_mod, n_extra=len(extras), epilogue=epilogue)
    return pl.pallas_call(
        kern,
        grid=(n_row_tiles, n_col),
        in_specs=in_specs,
        out_specs=out_specs,
        out_shape=out_shapes,
        scratch_shapes=[pltpu.VMEM((bm, kdim), BF16)],
        compiler_params=_params("arbitrary", "arbitrary"),
        name=name,
    )(*args)


def _epi_colscale(acc, n, extras, outs):
    (cs_ref,), (o_ref,) = extras, outs
    o_ref[...] = (acc * cs_ref[...]).astype(o_ref.dtype)


def _epi_plain(acc, n, extras, outs):
    (o_ref,) = outs
    o_ref[...] = acc.astype(o_ref.dtype)


def _epi_mla_a(acc, n, extras, outs):
    cos_ref, sa_ref, sb_ref = extras
    qc_ref, kpe_ref = outs
    wide = qc_ref.shape[1]
    qc_ref[...] = acc[:, :wide]
    kpe = _rope(acc[:, wide:], cos_ref[...], sa_ref[...], sb_ref[...], MLA_ROPE_DIM // 4)
    kpe_ref[...] = kpe.astype(kpe_ref.dtype)


def _epi_mla_q(acc, n, extras, outs, *, scale):
    cos_ref, sa_ref, sb_ref = extras
    (qt_ref,) = outs
    for c in range(acc.shape[1] // LANES):
        y = acc[:, c * LANES:(c + 1) * LANES]
        if c % 2 == 1:
            y = _rope(y, cos_ref[...], sa_ref[...], sb_ref[...], MLA_ROPE_DIM // 4)
        qt_ref[c // 2, (c % 2) * LANES:(c % 2 + 1) * LANES, :] = (y * scale).T.astype(qt_ref.dtype)


def _epi_mla_k(acc, n, extras, outs):
    (kpe_ref,), (o_ref,) = extras, outs
    kpe = kpe_ref[...]
    for c in range(acc.shape[1] // LANES):
        o_ref[:, (2 * c) * LANES:(2 * c + 1) * LANES] = acc[:, c * LANES:(c + 1) * LANES].astype(o_ref.dtype)
        o_ref[:, (2 * c + 1) * LANES:(2 * c + 2) * LANES] = kpe


def _epi_gqa(acc, n, extras, outs, *, n_q_tiles):
    gain_ref, cos_ref, sa_ref, sb_ref = extras
    qt_ref, kv_ref = outs

    def head(c):
        y = _rms(acc[:, c * LANES:(c + 1) * LANES], gain_ref[:, c * LANES:(c + 1) * LANES])
        return _rope(y, cos_ref[...], sa_ref[...], sb_ref[...], GQA_HEAD_DIM // 4)

    @pl.when(n < n_q_tiles)
    def _():
        for c in range(acc.shape[1] // LANES):
            qt_ref[c] = head(c).T.astype(qt_ref.dtype)

    @pl.when(n == n_q_tiles)
    def _():
        for c in range(acc.shape[1] // LANES):
            kv_ref[:, c * LANES:(c + 1) * LANES] = head(c).astype(kv_ref.dtype)

    @pl.when(n > n_q_tiles)
    def _():
        kv_ref[...] = acc.astype(kv_ref.dtype)


def _mm_res_kernel(a_ref, w_ref, x_ref, g_ref, o_ref):
    acc = jnp.dot(a_ref[...], w_ref[...], preferred_element_type=F32)
    o_ref[...] = x_ref[...] + g_ref[0] * acc


def _matmul_residual(a, w, x, mod, gate_chunk, bn, n_row_tiles, mod_row, name):
    bm = ROW_TILE
    kdim, n_out = w.shape
    assert n_out == x.shape[1] and n_out % bn == 0
    per_chunk = n_out // bn
    return pl.pallas_call(
        _mm_res_kernel,
        grid=(n_row_tiles, per_chunk),
        in_specs=[pl.BlockSpec((bm, kdim), lambda m, n: (m, 0)),
                  pl.BlockSpec((kdim, bn), lambda m, n: (0, n)),
                  pl.BlockSpec((bm, bn), lambda m, n: (m, n)),
                  pl.BlockSpec((1, 1, bn), lambda m, n: (mod_row(m), 0, gate_chunk * per_chunk + n))],
        out_specs=pl.BlockSpec((bm, bn), lambda m, n: (m, n)),
        out_shape=jax.ShapeDtypeStruct(x.shape, x.dtype),
        input_output_aliases={2: 0},
        compiler_params=_params("arbitrary", "arbitrary"),
        name=name,
    )(a, w, x, mod)


def _ffn_up_kernel(x_ref, xp_ref, xn_ref, g_ref, sc_ref, sh_ref, wg_ref, wv_ref, cw_ref, cb_ref, o_ref, h_sc,
                   *, lat_rows, lat_seq, ctx_seq):
    bm = x_ref.shape[0]
    m = pl.program_id(0)

    @pl.when(pl.program_id(1) == 0)
    def _():
        def nm(x):
            return (_rms(x, g_ref[...]) * (1.0 + sc_ref[0]) + sh_ref[0]).astype(BF16)
        h_sc[0:bm, :] = nm(x_ref[...])
        h_sc[bm:bm + 2 * HALO, :] = nm(jnp.concatenate([xn_ref[...], xp_ref[...]], axis=0))

    ext = bm + 2 * HALO
    row = m * bm + lax.broadcasted_iota(jnp.int32, (bm, 1), 0)
    in_ctx = row >= lat_rows
    pos = jnp.where(in_ctx, (row - lat_rows) & (ctx_seq - 1), row & (lat_seq - 1))
    seq = jnp.where(in_ctx, ctx_seq, lat_seq)
    gate = jnp.dot(h_sc[...], wg_ref[...], preferred_element_type=F32)
    val = jnp.dot(h_sc[0:bm, :], wv_ref[...], preferred_element_type=F32)
    g_prev = pltpu.roll(gate, 1, 0)[0:bm]
    g_next = pltpu.roll(gate, ext - 1, 0)[0:bm]
    g_cur = gate[0:bm]
    g_prev = jnp.where(pos == 0, 0.0, g_prev)
    g_next = jnp.where(pos == seq - 1, 0.0, g_next)
    cw = cw_ref[...]
    z = cb_ref[...] + g_prev * cw[0:1] + g_cur * cw[1:2] + g_next * cw[2:3]
    o_ref[...] = (jax.nn.silu(z) * val).astype(o_ref.dtype)


def _ffn_up(x, g, mod, w_up, conv_w, conv_b, bn, n_row_tiles, mod_row, lat_rows, lat_seq, ctx_seq, name):
    bm = ROW_TILE
    rows, d = x.shape
    f = w_up.shape[1] // 2
    n_col = f // bn
    last_halo = rows // HALO - 1
    per = bm // HALO
    kern = functools.partial(_ffn_up_kernel, lat_rows=lat_rows, lat_seq=lat_seq, ctx_seq=ctx_seq)
    return pl.pallas_call(
        kern,
        grid=(n_row_tiles, n_col),
        in_specs=[pl.BlockSpec((bm, d), lambda m, n: (m, 0)),
                  pl.BlockSpec((HALO, d), lambda m, n: (jnp.maximum(m * per - 1, 0), 0)),
                  pl.BlockSpec((HALO, d), lambda m, n: (jnp.minimum((m + 1) * per, last_halo), 0)),
                  pl.BlockSpec((1, d), lambda m, n: (0, 0)),
                  pl.BlockSpec((1, 1, d), lambda m, n: (mod_row(m), 0, 4)),
                  pl.BlockSpec((1, 1, d), lambda m, n: (mod_row(m), 0, 3)),
                  pl.BlockSpec((d, bn), lambda m, n: (0, n)),
                  pl.BlockSpec((d, bn), lambda m, n: (0, n_col + n)),
                  pl.BlockSpec((CONV_WIDTH, bn), lambda m, n: (0, n)),
                  pl.BlockSpec((1, bn), lambda m, n: (0, n))],
        out_specs=pl.BlockSpec((bm, bn), lambda m, n: (m, n)),
        out_shape=jax.ShapeDtypeStruct((rows, f), BF16),
        scratch_shapes=[pltpu.VMEM((bm + 2 * HALO, d), BF16)],
        compiler_params=_params("arbitrary", "arbitrary"),
        name=name,
    )(x, x, x, g.reshape(1, d), mod, mod, w_up, w_up, conv_w, conv_b.reshape(1, f))


def _flash_t_kernel(qt_ref, k_ref, vt_ref, kc_ref, vct_ref, o_ref, m_sc, acc_sc, s_sc, mx_sc, *, rep, bk, n_kv,
                    chunk, unroll, lead):
    bq = qt_ref.shape[2]
    dv = vt_ref.shape[3] - ONES_ROWS
    n_chunks = rep * bq // chunk

    def scores(k, c, half, n_keys):
        r, off = divmod(c * chunk, bq)
        s = jnp.dot(k, qt_ref[r, :, off:off + chunk], preferred_element_type=F32)
        s_sc[half, c, 0:n_keys, :] = s
        mx_sc[half, c] = jnp.max(s, axis=0, keepdims=True)

    def softmax_pv(half, n_keys, vt, c, is_first):
        cs = slice(c * chunk, (c + 1) * chunk)
        s = s_sc[half, c, 0:n_keys, :]
        mx = mx_sc[half, c]
        if is_first:
            m_new = mx
            p = jnp.exp2(s - m_new)
            acc_sc[:, cs] = jnp.dot(vt, p.astype(vt.dtype), preferred_element_type=F32)
        else:
            m_prev = m_sc[:, cs]
            m_new = jnp.maximum(m_prev, mx)
            alpha = jnp.exp2(m_prev - m_new)
            p = jnp.exp2(s - m_new)
            acc_sc[:, cs] = alpha * acc_sc[:, cs] + jnp.dot(vt, p.astype(vt.dtype), preferred_element_type=F32)
        m_sc[:, cs] = m_new

    kc = kc_ref[...]
    k0 = k_ref[0:bk, :]
    n_ctx = kc_ref.shape[0]
    assert n_ctx <= bk
    for c in range(n_chunks):
        scores(kc, c, 1, n_ctx)
    for c in range(n_chunks):
        scores(k0, c, 0, bk)
        softmax_pv(1, n_ctx, vct_ref[0, 0], c, True)

    def body(i, carry):
        def issue_scores(t):
            u, c = divmod(t, n_chunks)
            j_next = jnp.minimum(unroll * i + u + 1, n_kv - 1)
            off = pl.multiple_of(j_next * bk, bk)
            scores(k_ref[pl.ds(off, bk), :], c, (u + 1) % 2, bk)

        def consume(t):
            u, c = divmod(t, n_chunks)
            softmax_pv(u % 2, bk, vt_ref[0, 0, unroll * i + u], c, False)

        n_tasks = unroll * n_chunks
        for t in range(n_tasks + lead):
            if t < n_tasks:
                issue_scores(t)
            if t >= lead:
                consume(t - lead)
        return carry

    assert unroll % 2 == 0 and n_kv % unroll == 0 and lead < n_chunks
    lax.fori_loop(0, n_kv // unroll, body, 0)
    for r in range(rep):
        ot = acc_sc[0:dv, r * bq:(r + 1) * bq] / acc_sc[dv:dv + 1, r * bq:(r + 1) * bq]
        o_ref[:, r * dv:(r + 1) * dv] = ot.T.astype(o_ref.dtype)


def _attention_t(qt, k_arr, v_arr, *, batch, seq, ctx_len, groups, dv, kcol0, vcol0, bq, bk, chunk, out_cols, name):
    heads, dk, rows = qt.shape
    lat_rows = batch * seq
    rep = heads // groups
    n_q, n_kv = seq // bq, seq // bk
    ctx_blk0 = lat_rows // ctx_len
    v_lat = v_arr[:lat_rows, vcol0 * dv:(vcol0 + groups) * dv]
    vt = v_lat.reshape(batch, n_kv, bk, groups, dv).transpose(0, 3, 1, 4, 2)
    vt = jnp.concatenate([vt, jnp.ones(vt.shape[:3] + (ONES_ROWS, bk), vt.dtype)], axis=3)
    v_ctx = v_arr[lat_rows:lat_rows + batch * ctx_len, vcol0 * dv:(vcol0 + groups) * dv]
    vct = v_ctx.reshape(batch, ctx_len, groups, dv).transpose(0, 2, 3, 1)
    vct = jnp.concatenate([vct, jnp.ones(vct.shape[:2] + (ONES_ROWS, ctx_len), vct.dtype)], axis=2)
    dve = dv + ONES_ROWS
    kern = functools.partial(_flash_t_kernel, rep=rep, bk=bk, n_kv=n_kv, chunk=chunk,
                             unroll=min(FLASH_UNROLL, n_kv), lead=FLASH_LEAD)
    return pl.pallas_call(
        kern,
        grid=(batch, groups, n_q),
        in_specs=[pl.BlockSpec((rep, dk, bq), lambda b, g, i: (g, 0, b * n_q + i)),
                  pl.BlockSpec((seq, dk), lambda b, g, i: (b, kcol0 + g)),
                  pl.BlockSpec((1, 1, n_kv, dve, bk), lambda b, g, i: (b, g, 0, 0, 0)),
                  pl.BlockSpec((ctx_len, dk), lambda b, g, i: (ctx_blk0 + b, kcol0 + g)),
                  pl.BlockSpec((1, 1, dve, ctx_len), lambda b, g, i: (b, g, 0, 0))],
        out_specs=pl.BlockSpec((bq, rep * dv), lambda b, g, i: (b * n_q + i, g)),
        out_shape=jax.ShapeDtypeStruct((rows, out_cols), BF16),
        scratch_shapes=[pltpu.VMEM((1, rep * bq), F32), pltpu.VMEM((dve, rep * bq), F32),
                        pltpu.VMEM((2, rep * bq // chunk, bk, chunk), F32),
                        pltpu.VMEM((2, rep * bq // chunk, 1, chunk), F32)],
        compiler_params=_params("arbitrary", "arbitrary", "arbitrary"),
        name=name,
    )(qt, k_arr, vt, k_arr, vct)


def _ctx_attention_kernel(q_ref, k_ref, v_ref, o_prev_ref, o_ref, *, rep, dk, dv, q_transposed):
    del o_prev_ref
    if q_transposed:
        heads = [q_ref[r].astype(F32).T.astype(k_ref.dtype) for r in range(rep)]
    else:
        heads = [q_ref[:, r * dk:(r + 1) * dk] for r in range(rep)]
    n_q = heads[0].shape[0]
    q = jnp.concatenate(heads, axis=0) if rep > 1 else heads[0]
    s = lax.dot_general(q, k_ref[...], (((1,), (1,)), ((), ())), preferred_element_type=F32)
    p = jnp.exp2(s - jnp.max(s, axis=-1, keepdims=True))
    o = jnp.dot(p.astype(v_ref.dtype), v_ref[...], preferred_element_type=F32) / jnp.sum(p, axis=-1, keepdims=True)
    for r in range(rep):
        o_ref[:, r * dv:(r + 1) * dv] = o[r * n_q:(r + 1) * n_q].astype(o_ref.dtype)


def _ctx_attention(q_arr, q_transposed, k_arr, v_arr, o_prev, *, batch, groups, rep, dk, dv, qcol0, kcol0, vcol0,
                   ctx_blk0, ctx_len, name):
    if q_transposed:
        q_spec = pl.BlockSpec((rep, dk, ctx_len), lambda b, g: (g, 0, ctx_blk0 + b))
    else:
        q_spec = pl.BlockSpec((ctx_len, rep * dk), lambda b, g: (ctx_blk0 + b, qcol0 + g))
    kern = functools.partial(_ctx_attention_kernel, rep=rep, dk=dk, dv=dv, q_transposed=q_transposed)
    return pl.pallas_call(
        kern,
        grid=(batch, groups),
        in_specs=[q_spec,
                  pl.BlockSpec((ctx_len, dk), lambda b, g: (ctx_blk0 + b, kcol0 + g)),
                  pl.BlockSpec((ctx_len, dv), lambda b, g: (ctx_blk0 + b, vcol0 + g)),
                  pl.BlockSpec(memory_space=pl.ANY)],
        out_specs=pl.BlockSpec((ctx_len, rep * dv), lambda b, g: (ctx_blk0 + b, g)),
        out_shape=jax.ShapeDtypeStruct(o_prev.shape, o_prev.dtype),
        input_output_aliases={3: 0},
        compiler_params=_params("arbitrary", "arbitrary"),
        name=name,
    )(q_arr, k_arr, v_arr, o_prev)


def _rpb_table_kernel(rpb_ref, o_ref, *, n_tiles, grid_rows):
    var = pl.program_id(0)
    h = pl.program_id(1)
    t_rep = jnp.where(var == 0, 0, jnp.where(var == 1, 1, n_tiles - 1))
    qc = lax.broadcasted_iota(jnp.int32, (GRID_W, GRID_W), 0)
    kc = lax.broadcasted_iota(jnp.int32, (GRID_W, GRID_W), 1)
    dc = kc - qc
    cstart = jnp.clip(qc - NA_WIN_COLS // 2, 0, GRID_W - NA_WIN_COLS)
    valid_c = (kc >= cstart) & (kc < cstart + NA_WIN_COLS)
    n_dc = 2 * NA_WIN_COLS - 1
    masked = jnp.full((GRID_W, GRID_W), MASK_VALUE, F32)
    toeplitz = {}
    for dr in range(-(NA_WIN_ROWS - 1), NA_WIN_ROWS):
        acc = jnp.zeros((GRID_W, GRID_W), F32)
        for b in range(n_dc):
            bias = rpb_ref[h, (dr + NA_WIN_ROWS - 1) * n_dc + b] * LOG2_E
            acc = jnp.where(dc == b - (NA_WIN_COLS - 1), bias, acc)
        toeplitz[dr] = jnp.where(valid_c, acc, MASK_VALUE)
    for ql in range(NA_TILE_ROWS):
        qr = NA_TILE_ROWS * t_rep + ql
        rstart = jnp.clip(qr - NA_WIN_ROWS // 2, 0, grid_rows - NA_WIN_ROWS)
        for kl in range(3 * NA_TILE_ROWS):
            kr = NA_TILE_ROWS * (t_rep - 1) + kl
            dr = kl - NA_TILE_ROWS - ql
            if abs(dr) > NA_WIN_ROWS - 1:
                blk = masked
            else:
                ok = ((kr >= rstart) & (kr < rstart + NA_WIN_ROWS)).astype(F32)
                blk = toeplitz[dr] * ok + MASK_VALUE * (1.0 - ok)
            o_ref[0, 0, ql * GRID_W:(ql + 1) * GRID_W, kl * GRID_W:(kl + 1) * GRID_W] = blk


def _rpb_table(rpb, n_tiles, grid_rows):
    heads = rpb.shape[0]
    tq = NA_TILE_ROWS * GRID_W
    kern = functools.partial(_rpb_table_kernel, n_tiles=n_tiles, grid_rows=grid_rows)
    return pl.pallas_call(
        kern,
        grid=(3, heads),
        in_specs=[pl.BlockSpec(memory_space=pltpu.SMEM)],
        out_specs=pl.BlockSpec((1, 1, tq, 3 * tq), lambda v, h: (v, h, 0, 0)),
        out_shape=jax.ShapeDtypeStruct((3, heads, tq, 3 * tq), F32),
        compiler_params=_params("arbitrary", "arbitrary"),
        name="na_rpb_table",
    )(rpb.reshape(heads, -1))


def _na_kernel(q_ref, kp_ref, kc_ref, kn_ref, kx_ref, vp_ref, vc_ref, vn_ref, vx_ref, tab_ref, o_ref, *, heads, dh):
    tq = q_ref.shape[0]
    k_refs = (kp_ref, kc_ref, kn_ref, kx_ref)
    v_refs = (vp_ref, vc_ref, vn_ref, vx_ref)
    ones = jnp.ones((tq, dh), BF16)

    def scores(h):
        sl = slice(h * dh, (h + 1) * dh)
        q = q_ref[:, sl]
        return [lax.dot_general(q, k_ref[:, sl], (((1,), (1,)), ((), ())), preferred_element_type=F32)
                for k_ref in k_refs]

    s_next = scores(0)
    for h in range(heads):
        sl = slice(h * dh, (h + 1) * dh)
        s = s_next
        if h + 1 < heads:
            s_next = scores(h + 1)
        s = [s[j] + tab_ref[0, h, :, j * tq:(j + 1) * tq] for j in range(3)] + [s[3]]
        m = jnp.max(jnp.maximum(jnp.maximum(s[0], s[1]), jnp.maximum(s[2], s[3])), axis=-1, keepdims=True)
        o = None
        for sj, v_ref in zip(s, v_refs):
            p = jnp.exp2(sj - m).astype(BF16)
            part = jnp.dot(p, jnp.concatenate([v_ref[:, sl], ones], axis=1), preferred_element_type=F32)
            o = part if o is None else o + part
        o_ref[:, sl] = (o[:, 0:dh] / o[:, dh:2 * dh]).astype(o_ref.dtype)


def _neighborhood_attention(qkv, table, *, batch, seq, ctx_len, heads, dh, hb):
    rows = qkv.shape[0]
    tq = NA_TILE_ROWS * GRID_W
    assert ctx_len == tq and seq % tq == 0
    n_tiles = seq // tq
    n_hg = heads // hb
    ctx_blk0 = batch * seq // ctx_len
    wb = hb * dh

    def var(t):
        return jnp.where(t == 0, 0, jnp.where(t == n_tiles - 1, 2, 1))

    def spec(col0, shift):
        if shift is None:
            return pl.BlockSpec((tq, wb), lambda b, g, t: (ctx_blk0 + b, col0 + g))
        return pl.BlockSpec((tq, wb), lambda b, g, t: (b * n_tiles + jnp.clip(t + shift, 0, n_tiles - 1), col0 + g))

    kern = functools.partial(_na_kernel, heads=hb, dh=dh)
    return pl.pallas_call(
        kern,
        grid=(batch, n_hg, n_tiles),
        in_specs=[spec(0, 0),
                  spec(n_hg, -1), spec(n_hg, 0), spec(n_hg, 1), spec(n_hg, None),
                  spec(2 * n_hg, -1), spec(2 * n_hg, 0), spec(2 * n_hg, 1), spec(2 * n_hg, None),
                  pl.BlockSpec((1, hb, tq, 3 * tq), lambda b, g, t: (var(t), g, 0, 0))],
        out_specs=pl.BlockSpec((tq, wb), lambda b, g, t: (b * n_tiles + t, g)),
        out_shape=jax.ShapeDtypeStruct((rows, heads * dh), BF16),
        compiler_params=_params("arbitrary", "arbitrary", "arbitrary"),
        name="na_attention",
    )(qkv, qkv, qkv, qkv, qkv, qkv, qkv, qkv, qkv, table)


def _final_norm_kernel(x_ref, g_ref, o_ref):
    o_ref[...] = _rms(x_ref[...], g_ref[...])


def _final_norm(x, g, n_row_tiles):
    bm = ROW_TILE
    d = x.shape[1]
    return pl.pallas_call(
        _final_norm_kernel,
        grid=(n_row_tiles,),
        in_specs=[pl.BlockSpec((bm, d), lambda m: (m, 0)), pl.BlockSpec((1, d), lambda m: (0, 0))],
        out_specs=pl.BlockSpec((bm, d), lambda m: (m, 0)),
        out_shape=jax.ShapeDtypeStruct((n_row_tiles * bm, d), F32),
        compiler_params=_params("arbitrary"),
        name="final_norm",
    )(x, g.reshape(1, d))


def _rope_tables(n_tok, rot_dim, batch, n_ctx_rows):
    quarter = rot_dim // 4
    t = jnp.arange(n_tok)
    row = (t // GRID_W).astype(F32)
    col = (t % GRID_W).astype(F32)
    inv = ROPE_THETA ** (-jnp.arange(quarter, dtype=F32) / quarter)
    ar = row[:, None] * inv
    ac = col[:, None] * inv
    ang = jnp.concatenate([ar, ar, ac, ac], axis=-1)
    cos, sin = jnp.cos(ang), jnp.sin(ang)
    first = (jnp.arange(rot_dim) % (2 * quarter)) < quarter
    sin_a = jnp.where(first, -sin, 0.0)
    sin_b = jnp.where(first, 0.0, sin)
    pad = LANES - rot_dim

    def full(tab, fill):
        tab = jnp.pad(tab, ((0, 0), (0, pad)), constant_values=fill)
        tab = jnp.tile(tab, (batch, 1))
        return jnp.concatenate([tab, jnp.full((n_ctx_rows, LANES), fill, F32)], axis=0)

    return full(cos, 1.0), full(sin_a, 0.0), full(sin_b, 0.0)


def kernel(x, c, ctx, c_ctx, ada_w, ada_b, mix_norm, ffn_norm, ffn_up, ffn_conv_w, ffn_conv_b, ffn_down, na_wqkv, na_rpb, na_wo, mla_wq_a, mla_q_norm, mla_wq_b, mla_wkv_a, mla_kv_norm, mla_wkv_b, mla_wo, gqa_wqkv, gqa_q_norm, gqa_k_norm, gqa_wo, final_norm):
    batch, seq, d = x.shape
    ctx_len = ctx.shape[1]
    depth = ada_w.shape[0]
    bm = ROW_TILE
    lat_rows = batch * seq
    ctx_rows = batch * ctx_len
    pad_rows = -(lat_rows + ctx_rows) % bm
    rows = lat_rows + ctx_rows + pad_rows
    assert seq % bm == 0 and batch < 8
    assert seq & (seq - 1) == 0 and ctx_len & (ctx_len - 1) == 0
    tiles_per_batch = seq // bm
    lat_tiles = lat_rows // bm
    all_tiles = rows // bm
    grid_rows = seq // GRID_W

    def mod_row(m):
        return jnp.minimum(m // tiles_per_batch, batch)

    xs = jnp.concatenate([x.reshape(lat_rows, d), ctx.reshape(ctx_rows, d), jnp.zeros((pad_rows, d), F32)], axis=0)
    c_rows = jnp.zeros((8, d), F32).at[:batch].set(c).at[batch].set(c_ctx)
    mod_all = _modulation(c_rows, ada_w, ada_b)

    cos_b, sa_b, sb_b = _rope_tables(seq, MLA_ROPE_DIM, batch, ctx_rows + pad_rows)
    cos_c, sa_c, sb_c = _rope_tables(seq, GQA_HEAD_DIM, batch, ctx_rows + pad_rows)

    def row_spec(width):
        return pl.BlockSpec((bm, width), lambda m, n: (m, 0))

    rope_specs = [row_spec(LANES)] * 3

    for i in range(depth):
        last = i == depth - 1
        kind, j = i % N_MIXERS, i // N_MIXERS
        mod = mod_all[i].reshape(8, 1, 6 * d)
        n_tiles = lat_tiles if last else all_tiles

        if kind == 0:
            scale = (d // NA_HEADS) ** -0.5 * LOG2_E
            colscale = jnp.concatenate([jnp.full((1, d), scale, F32), jnp.ones((1, 2 * d), F32)], axis=1)
            bn = d // 2
            qkv = _norm_matmul(
                xs, 0, d, mix_norm[i], mod, (1, 0), na_wqkv[j].astype(BF16), bn,
                [colscale], [pl.BlockSpec((1, bn), lambda m, n: (0, n))],
                jax.ShapeDtypeStruct((rows, 3 * d), BF16), pl.BlockSpec((bm, bn), lambda m, n: (m, n)),
                _epi_colscale, all_tiles, mod_row, "na_qkv")
            table = _rpb_table(na_rpb[j], seq // (NA_TILE_ROWS * GRID_W), grid_rows)
            dh = d // NA_HEADS
            o = _neighborhood_attention(qkv, table, batch=batch, seq=seq, ctx_len=ctx_len,
                                        heads=NA_HEADS, dh=dh, hb=NA_HEADS_PER_STEP)
            if not last:
                o = _ctx_attention(qkv, False, qkv, qkv, o, batch=batch, groups=NA_HEADS, rep=1, dk=dh, dv=dh,
                                   qcol0=0, kcol0=NA_HEADS, vcol0=2 * NA_HEADS, ctx_blk0=lat_rows // ctx_len,
                                   ctx_len=ctx_len, name="na_ctx_attention")
            wo = na_wo[j]
        elif kind == 1:
            scale = (MLA_NOPE_DIM + MLA_ROPE_DIM) ** -0.5 * LOG2_E
            q_rank = mla_wq_a.shape[2]
            kv_rank = mla_kv_norm.shape[1]
            hq = 2 * LANES
            w1 = jnp.concatenate([mla_wq_a[j], mla_wkv_a[j], jnp.zeros((d, LANES - MLA_ROPE_DIM), F32)], axis=1)
            n1 = w1.shape[1]
            qc, kpe = _norm_matmul(
                xs, 0, d, mix_norm[i], mod, (1, 0), w1.astype(BF16), n1,
                [cos_b, sa_b, sb_b], rope_specs,
                (jax.ShapeDtypeStruct((rows, q_rank + kv_rank), F32), jax.ShapeDtypeStruct((rows, LANES), BF16)),
                (pl.BlockSpec((bm, q_rank + kv_rank), lambda m, n: (m, 0)), pl.BlockSpec((bm, LANES), lambda m, n: (m, 0))),
                _epi_mla_a, all_tiles, mod_row, "mla_a")
            wqb = mla_wq_b[j].reshape(q_rank, MLA_HEADS, MLA_NOPE_DIM + MLA_ROPE_DIM)
            wqb = jnp.pad(wqb, ((0, 0), (0, 0), (0, hq - MLA_NOPE_DIM - MLA_ROPE_DIM))).reshape(q_rank, MLA_HEADS * hq)
            bn = 4 * hq
            qt = _norm_matmul(
                qc, 0, q_rank, mla_q_norm[j], None, None, wqb.astype(BF16), bn,
                [cos_b, sa_b, sb_b], rope_specs,
                jax.ShapeDtypeStruct((MLA_HEADS, hq, rows), BF16),
                pl.BlockSpec((bn // hq, hq, bm), lambda m, n: (n, 0, m)),
                functools.partial(_epi_mla_q, scale=scale), all_tiles, mod_row, "mla_q")
            wkvb = mla_wkv_b[j].reshape(kv_rank, MLA_HEADS, MLA_NOPE_DIM + MLA_V_DIM)
            wk = wkvb[:, :, :MLA_NOPE_DIM].reshape(kv_rank, MLA_HEADS * MLA_NOPE_DIM)
            wv = wkvb[:, :, MLA_NOPE_DIM:].reshape(kv_rank, MLA_HEADS * MLA_V_DIM)
            bnk = 8 * MLA_NOPE_DIM
            k = _norm_matmul(
                qc, 1, kv_rank, mla_kv_norm[j], None, None, wk.astype(BF16), bnk,
                [kpe], [row_spec(LANES)],
                jax.ShapeDtypeStruct((rows, MLA_HEADS * hq), BF16), pl.BlockSpec((bm, 2 * bnk), lambda m, n: (m, n)),
                _epi_mla_k, all_tiles, mod_row, "mla_k")
            v = _norm_matmul(
                qc, 1, kv_rank, mla_kv_norm[j], None, None, wv.astype(BF16), bnk,
                [], [],
                jax.ShapeDtypeStruct((rows, MLA_HEADS * MLA_V_DIM), BF16), pl.BlockSpec((bm, bnk), lambda m, n: (m, n)),
                _epi_plain, all_tiles, mod_row, "mla_v")
            out_cols = MLA_HEADS * MLA_V_DIM
            o = _attention_t(qt, k, v, batch=batch, seq=seq, ctx_len=ctx_len, groups=MLA_HEADS, dv=MLA_V_DIM,
                             kcol0=0, vcol0=0, bq=min(FLASH_Q_TILE_MLA, seq), bk=FLASH_KEY_BLOCK, chunk=FLASH_CHUNK,
                             out_cols=out_cols, name="mla_attention")
            if not last:
                o = _ctx_attention(qt, True, k, v, o, batch=batch, groups=MLA_HEADS, rep=1, dk=hq, dv=MLA_V_DIM,
                                   qcol0=None, kcol0=0, vcol0=0, ctx_blk0=lat_rows // ctx_len, ctx_len=ctx_len,
                                   name="mla_ctx_attention")
            wo = mla_wo[j]
        else:
            scale = GQA_HEAD_DIM ** -0.5 * LOG2_E
            rep = GQA_HEADS // GQA_KV_HEADS
            nq_cols = GQA_HEADS * GQA_HEAD_DIM
            nk_cols = GQA_KV_HEADS * GQA_HEAD_DIM
            bn = nk_cols
            gain = jnp.concatenate([jnp.tile(gqa_q_norm[j], GQA_HEADS) * scale, jnp.tile(gqa_k_norm[j], GQA_KV_HEADS),
                                    jnp.ones((nk_cols,), F32)]).reshape(1, -1)
            dh = GQA_HEAD_DIM
            n_q_tiles = nq_cols // bn
            qt, kv = _norm_matmul(
                xs, 0, d, mix_norm[i], mod, (1, 0), gqa_wqkv[j].astype(BF16), bn,
                [gain, cos_c, sa_c, sb_c], [pl.BlockSpec((1, bn), lambda m, n: (0, n))] + rope_specs,
                (jax.ShapeDtypeStruct((GQA_HEADS, dh, rows), BF16), jax.ShapeDtypeStruct((rows, 2 * nk_cols), BF16)),
                (pl.BlockSpec((bn // dh, dh, bm), lambda m, n: (jnp.minimum(n, n_q_tiles - 1), 0, m)),
                 pl.BlockSpec((bm, bn), lambda m, n: (m, jnp.maximum(n - n_q_tiles, 0)))),
                functools.partial(_epi_gqa, n_q_tiles=n_q_tiles), all_tiles, mod_row, "gqa_qkv")
            o = _attention_t(qt, kv, kv, batch=batch, seq=seq, ctx_len=ctx_len, groups=GQA_KV_HEADS, dv=dh,
                             kcol0=0, vcol0=GQA_KV_HEADS, bq=min(FLASH_Q_TILE_GQA, seq), bk=FLASH_KEY_BLOCK,
                             chunk=FLASH_CHUNK,
                             out_cols=nq_cols, name="gqa_attention")
            if not last:
                o = _ctx_attention(qt, True, kv, kv, o, batch=batch, groups=GQA_KV_HEADS, rep=rep, dk=dh, dv=dh,
                                   qcol0=None, kcol0=0, vcol0=GQA_KV_HEADS, ctx_blk0=lat_rows // ctx_len,
                                   ctx_len=ctx_len, name="gqa_ctx_attention")
            wo = gqa_wo[j]

        xs = _matmul_residual(o, wo.astype(BF16), xs, mod, 2, d // 2, n_tiles, mod_row, "attn_out")
        a = _ffn_up(xs, ffn_norm[i], mod, ffn_up[i].astype(BF16), ffn_conv_w[i], ffn_conv_b[i], FFN_COL_TILE,
                    n_tiles, mod_row, lat_rows, seq, ctx_len, "ffn_up")
        xs = _matmul_residual(a, ffn_down[i].astype(BF16), xs, mod, 5, d // 4, n_tiles, mod_row, "ffn_down")

    out = _final_norm(xs, final_norm, lat_tiles)
    return out.reshape(batch, seq, d)
```

```python
import functools

import jax
import jax.numpy as jnp
from jax import lax
from jax.experimental import pallas as pl
from jax.experimental.pallas import tpu as pltpu

F32 = jnp.float32
BF16 = jnp.bfloat16

GRID_W = 64
NA_HEADS = 16
NA_WIN_ROWS = 8
NA_WIN_COLS = 16
MLA_HEADS = 16
MLA_NOPE_DIM = 128
MLA_ROPE_DIM = 64
MLA_V_DIM = 128
GQA_HEADS = 16
GQA_KV_HEADS = 4
GQA_HEAD_DIM = 128
CONV_WIDTH = 3
ROPE_THETA = 10000.0
NORM_EPS = 1e-6
N_MIXERS = 3
LOG2_E = 1.4426950408889634

LANES = 128
ROW_TILE = 1024
HALO = 8
ONES_ROWS = 16
FLASH_KEY_BLOCK = 512
FLASH_CHUNK = 256
FLASH_Q_TILE_MLA = 2048
FLASH_Q_TILE_GQA = 512
NA_HEADS_PER_STEP = 16
NA_LEAD = 1
FFN_COL_TILE = 512
FLASH_UNROLL = 8
FLASH_LEAD = 2
NA_TILE_ROWS = 4
MASK_VALUE = -1e30
VMEM_LIMIT = 56 * 1024 * 1024


def _params(*sem):
    return pltpu.CompilerParams(dimension_semantics=sem, vmem_limit_bytes=VMEM_LIMIT)


def _rms(x, g):
    y = x * lax.rsqrt(jnp.mean(x * x, axis=-1, keepdims=True) + NORM_EPS)
    return y * g


def _rope(y, cos, sin_a, sin_b, quarter):
    return (y * cos + pltpu.roll(y, LANES - quarter, 1) * sin_a
            + pltpu.roll(y, quarter, 1) * sin_b)


def _mod_kernel(c_ref, w_ref, b_ref, o_ref):
    s = jax.nn.silu(c_ref[...])
    o_ref[0] = jnp.dot(s.astype(BF16), w_ref[0].astype(BF16), preferred_element_type=F32) + b_ref[0]


def _modulation(c_rows, ada_w, ada_b):
    depth, d, n = ada_w.shape
    bn = n // 8
    return pl.pallas_call(
        _mod_kernel,
        grid=(depth, n // bn),
        in_specs=[pl.BlockSpec((8, d), lambda l, j: (0, 0)),
                  pl.BlockSpec((1, d, bn), lambda l, j: (l, 0, j)),
                  pl.BlockSpec((1, 1, bn), lambda l, j: (l, 0, j))],
        out_specs=pl.BlockSpec((1, 8, bn), lambda l, j: (l, 0, j)),
        out_shape=jax.ShapeDtypeStruct((depth, 8, n), F32),
        compiler_params=_params("arbitrary", "arbitrary"),
        name="adaln_mod",
    )(c_rows, ada_w, ada_b.reshape(depth, 1, n))


def _norm_matmul_kernel(*refs, has_mod, n_extra, epilogue):
    if has_mod:
        x_ref, g_ref, sc_ref, sh_ref, w_ref = refs[:5]
        rest = refs[5:]
    else:
        x_ref, g_ref, w_ref = refs[:3]
        rest = refs[3:]
    extras, outs, h_sc = rest[:n_extra], rest[n_extra:-1], rest[-1]
    n = pl.program_id(1)

    @pl.when(n == 0)
    def _():
        h = _rms(x_ref[...], g_ref[...])
        if has_mod:
            h = h * (1.0 + sc_ref[0]) + sh_ref[0]
        h_sc[...] = h.astype(BF16)

    acc = jnp.dot(h_sc[...], w_ref[...], preferred_element_type=F32)
    epilogue(acc, n, extras, outs)


def _norm_matmul(x, xcol, kdim, g, mod, mod_chunks, w, bn, extras, extra_specs, out_shapes, out_specs,
                 epilogue, n_row_tiles, mod_row, name):
    bm = ROW_TILE
    n_col = w.shape[1] // bn
    has_mod = mod is not None
    in_specs = [pl.BlockSpec((bm, kdim), lambda m, n: (m, xcol)),
                pl.BlockSpec((1, kdim), lambda m, n: (0, 0))]
    args = [x, g.reshape(1, kdim)]
    if has_mod:
        sc_chunk, sh_chunk = mod_chunks
        in_specs += [pl.BlockSpec((1, 1, kdim), lambda m, n: (mod_row(m), 0, sc_chunk)),
                     pl.BlockSpec((1, 1, kdim), lambda m, n: (mod_row(m), 0, sh_chunk))]
        args += [mod, mod]
    in_specs.append(pl.BlockSpec((kdim, bn), lambda m, n: (0, n)))
    args.append(w)
    in_specs += extra_specs
    args += extras
    kern = functools.partial(_norm_matmul_kernel, has_mod=has_mod, n_extra=len(extras), epilogue=epilogue)
    return pl.pallas_call(
        kern,
        grid=(n_row_tiles, n_col),
        in_specs=in_specs,
        out_specs=out_specs,
        out_shape=out_shapes,
        scratch_shapes=[pltpu.VMEM((bm, kdim), BF16)],
        compiler_params=_params("arbitrary", "arbitrary"),
        name=name,
    )(*args)


def _epi_colscale(acc, n, extras, outs):
    (cs_ref,), (o_ref,) = extras, outs
    o_ref[...] = (acc * cs_ref[...]).astype(o_ref.dtype)


def _epi_plain(acc, n, extras, outs):
    (o_ref,) = outs
    o_ref[...] = acc.astype(o_ref.dtype)


def _epi_mla_a(acc, n, extras, outs):
    cos_ref, sa_ref, sb_ref = extras
    qc_ref, kpe_ref = outs
    wide = qc_ref.shape[1]
    qc_ref[...] = acc[:, :wide]
    kpe = _rope(acc[:, wide:], cos_ref[...], sa_ref[...], sb_ref[...], MLA_ROPE_DIM // 4)
    kpe_ref[...] = kpe.astype(kpe_ref.dtype)


def _epi_mla_q(acc, n, extras, outs, *, scale):
    cos_ref, sa_ref, sb_ref = extras
    (qt_ref,) = outs
    for c in range(acc.shape[1] // LANES):
        y = acc[:, c * LANES:(c + 1) * LANES]
        if c % 2 == 1:
            y = _rope(y, cos_ref[...], sa_ref[...], sb_ref[...], MLA_ROPE_DIM // 4)
        qt_ref[c // 2, (c % 2) * LANES:(c % 2 + 1) * LANES, :] = (y * scale).T.astype(qt_ref.dtype)


def _epi_mla_k(acc, n, extras, outs):
    (kpe_ref,), (o_ref,) = extras, outs
    kpe = kpe_ref[...]
    for c in range(acc.shape[1] // LANES):
        o_ref[:, (2 * c) * LANES:(2 * c + 1) * LANES] = acc[:, c * LANES:(c + 1) * LANES].astype(o_ref.dtype)
        o_ref[:, (2 * c + 1) * LANES:(2 * c + 2) * LANES] = kpe


def _epi_gqa(acc, n, extras, outs, *, n_q_tiles):
    gain_ref, cos_ref, sa_ref, sb_ref = extras
    qt_ref, kv_ref = outs

    def head(c):
        y = _rms(acc[:, c * LANES:(c + 1) * LANES], gain_ref[:, c * LANES:(c + 1) * LANES])
        return _rope(y, cos_ref[...], sa_ref[...], sb_ref[...], GQA_HEAD_DIM // 4)

    @pl.when(n < n_q_tiles)
    def _():
        for c in range(acc.shape[1] // LANES):
            qt_ref[c] = head(c).T.astype(qt_ref.dtype)

    @pl.when(n == n_q_tiles)
    def _():
        for c in range(acc.shape[1] // LANES):
            kv_ref[:, c * LANES:(c + 1) * LANES] = head(c).astype(kv_ref.dtype)

    @pl.when(n > n_q_tiles)
    def _():
        kv_ref[...] = acc.astype(kv_ref.dtype)


def _mm_res_kernel(a_ref, w_ref, x_ref, g_ref, o_ref):
    acc = jnp.dot(a_ref[...], w_ref[...], preferred_element_type=F32)
    o_ref[...] = x_ref[...] + g_ref[0] * acc


def _matmul_residual(a, w, x, mod, gate_chunk, bn, n_row_tiles, mod_row, name):
    bm = ROW_TILE
    kdim, n_out = w.shape
    assert n_out == x.shape[1] and n_out % bn == 0
    per_chunk = n_out // bn
    return pl.pallas_call(
        _mm_res_kernel,
        grid=(n_row_tiles, per_chunk),
        in_specs=[pl.BlockSpec((bm, kdim), lambda m, n: (m, 0)),
                  pl.BlockSpec((kdim, bn), lambda m, n: (0, n)),
                  pl.BlockSpec((bm, bn), lambda m, n: (m, n)),
                  pl.BlockSpec((1, 1, bn), lambda m, n: (mod_row(m), 0, gate_chunk * per_chunk + n))],
        out_specs=pl.BlockSpec((bm, bn), lambda m, n: (m, n)),
        out_shape=jax.ShapeDtypeStruct(x.shape, x.dtype),
        input_output_aliases={2: 0},
        compiler_params=_params("arbitrary", "arbitrary"),
        name=name,
    )(a, w, x, mod)


def _ffn_up_kernel(x_ref, xp_ref, xn_ref, g_ref, sc_ref, sh_ref, wg_ref, wv_ref, cw_ref, cb_ref, o_ref, h_sc,
                   *, lat_rows, lat_seq, ctx_seq):
    bm = x_ref.shape[0]
    m = pl.program_id(0)

    @pl.when(pl.program_id(1) == 0)
    def _():
        def nm(x):
            return (_rms(x, g_ref[...]) * (1.0 + sc_ref[0]) + sh_ref[0]).astype(BF16)
        h_sc[0:bm, :] = nm(x_ref[...])
        h_sc[bm:bm + 2 * HALO, :] = nm(jnp.concatenate([xn_ref[...], xp_ref[...]], axis=0))

    ext = bm + 2 * HALO
    row = m * bm + lax.broadcasted_iota(jnp.int32, (bm, 1), 0)
    in_ctx = row >= lat_rows
    pos = jnp.where(in_ctx, (row - lat_rows) & (ctx_seq - 1), row & (lat_seq - 1))
    seq = jnp.where(in_ctx, ctx_seq, lat_seq)
    gate = jnp.dot(h_sc[...], wg_ref[...], preferred_element_type=F32)
    val = jnp.dot(h_sc[0:bm, :], wv_ref[...], preferred_element_type=F32)
    g_prev = pltpu.roll(gate, 1, 0)[0:bm]
    g_next = pltpu.roll(gate, ext - 1, 0)[0:bm]
    g_cur = gate[0:bm]
    g_prev = jnp.where(pos == 0, 0.0, g_prev)
    g_next = jnp.where(pos == seq - 1, 0.0, g_next)
    cw = cw_ref[...]
    z = cb_ref[...] + g_prev * cw[0:1] + g_cur * cw[1:2] + g_next * cw[2:3]
    o_ref[...] = (jax.nn.silu(z) * val).astype(o_ref.dtype)


def _ffn_up(x, g, mod, w_up, conv_w, conv_b, bn, n_row_tiles, mod_row, lat_rows, lat_seq, ctx_seq, name):
    bm = ROW_TILE
    rows, d = x.shape
    f = w_up.shape[1] // 2
    n_col = f // bn
    last_halo = rows // HALO - 1
    per = bm // HALO
    kern = functools.partial(_ffn_up_kernel, lat_rows=lat_rows, lat_seq=lat_seq, ctx_seq=ctx_seq)
    return pl.pallas_call(
        kern,
        grid=(n_row_tiles, n_col),
        in_specs=[pl.BlockSpec((bm, d), lambda m, n: (m, 0)),
                  pl.BlockSpec((HALO, d), lambda m, n: (jnp.maximum(m * per - 1, 0), 0)),
                  pl.BlockSpec((HALO, d), lambda m, n: (jnp.minimum((m + 1) * per, last_halo), 0)),
                  pl.BlockSpec((1, d), lambda m, n: (0, 0)),
                  pl.BlockSpec((1, 1, d), lambda m, n: (mod_row(m), 0, 4)),
                  pl.BlockSpec((1, 1, d), lambda m, n: (mod_row(m), 0, 3)),
                  pl.BlockSpec((d, bn), lambda m, n: (0, n)),
                  pl.BlockSpec((d, bn), lambda m, n: (0, n_col + n)),
                  pl.BlockSpec((CONV_WIDTH, bn), lambda m, n: (0, n)),
                  pl.BlockSpec((1, bn), lambda m, n: (0, n))],
        out_specs=pl.BlockSpec((bm, bn), lambda m, n: (m, n)),
        out_shape=jax.ShapeDtypeStruct((rows, f), BF16),
        scratch_shapes=[pltpu.VMEM((bm + 2 * HALO, d), BF16)],
        compiler_params=_params("arbitrary", "arbitrary"),
        name=name,
    )(x, x, x, g.reshape(1, d), mod, mod, w_up, w_up, conv_w, conv_b.reshape(1, f))


def _flash_t_kernel(qt_ref, k_ref, vt_ref, kc_ref, vct_ref, o_ref, m_sc, acc_sc, s_sc, mx_sc, *, rep, bk, n_kv,
                    chunk, unroll, lead):
    bq = qt_ref.shape[2]
    dv = vt_ref.shape[3] - ONES_ROWS
    n_chunks = rep * bq // chunk

    def scores(k, c, half, n_keys):
        r, off = divmod(c * chunk, bq)
        s = jnp.dot(k, qt_ref[r, :, off:off + chunk], preferred_element_type=F32)
        s_sc[half, c, 0:n_keys, :] = s
        mx_sc[half, c] = jnp.max(s, axis=0, keepdims=True)

    def softmax_pv(half, n_keys, vt, c, is_first):
        cs = slice(c * chunk, (c + 1) * chunk)
        s = s_sc[half, c, 0:n_keys, :]
        mx = mx_sc[half, c]
        if is_first:
            m_new = mx
            p = jnp.exp2(s - m_new)
            acc_sc[:, cs] = jnp.dot(vt, p.astype(vt.dtype), preferred_element_type=F32)
        else:
            m_prev = m_sc[:, cs]
            m_new = jnp.maximum(m_prev, mx)
            alpha = jnp.exp2(m_prev - m_new)
            p = jnp.exp2(s - m_new)
            acc_sc[:, cs] = alpha * acc_sc[:, cs] + jnp.dot(vt, p.astype(vt.dtype), preferred_element_type=F32)
        m_sc[:, cs] = m_new

    kc = kc_ref[...]
    k0 = k_ref[0:bk, :]
    n_ctx = kc_ref.shape[0]
    assert n_ctx <= bk
    for c in range(n_chunks):
        scores(kc, c, 1, n_ctx)
    for c in range(n_chunks):
        scores(k0, c, 0, bk)
        softmax_pv(1, n_ctx, vct_ref[0, 0], c, True)

    def body(i, carry):
        def issue_scores(t):
            u, c = divmod(t, n_chunks)
            j_next = jnp.minimum(unroll * i + u + 1, n_kv - 1)
            off = pl.multiple_of(j_next * bk, bk)
            scores(k_ref[pl.ds(off, bk), :], c, (u + 1) % 2, bk)

        def consume(t):
            u, c = divmod(t, n_chunks)
            softmax_pv(u % 2, bk, vt_ref[0, 0, unroll * i + u], c, False)

        n_tasks = unroll * n_chunks
        for t in range(n_tasks + lead):
            if t < n_tasks:
                issue_scores(t)
            if t >= lead:
                consume(t - lead)
        return carry

    assert unroll % 2 == 0 and n_kv % unroll == 0 and lead < n_chunks
    lax.fori_loop(0, n_kv // unroll, body, 0)
    for r in range(rep):
        ot = acc_sc[0:dv, r * bq:(r + 1) * bq] / acc_sc[dv:dv + 1, r * bq:(r + 1) * bq]
        o_ref[:, r * dv:(r + 1) * dv] = ot.T.astype(o_ref.dtype)


def _attention_t(qt, k_arr, v_arr, *, batch, seq, ctx_len, groups, dv, kcol0, vcol0, bq, bk, chunk, out_cols, name):
    heads, dk, rows = qt.shape
    lat_rows = batch * seq
    rep = heads // groups
    n_q, n_kv = seq // bq, seq // bk
    ctx_blk0 = lat_rows // ctx_len
    v_lat = v_arr[:lat_rows, vcol0 * dv:(vcol0 + groups) * dv]
    vt = v_lat.reshape(batch, n_kv, bk, groups, dv).transpose(0, 3, 1, 4, 2)
    vt = jnp.concatenate([vt, jnp.ones(vt.shape[:3] + (ONES_ROWS, bk), vt.dtype)], axis=3)
    v_ctx = v_arr[lat_rows:lat_rows + batch * ctx_len, vcol0 * dv:(vcol0 + groups) * dv]
    vct = v_ctx.reshape(batch, ctx_len, groups, dv).transpose(0, 2, 3, 1)
    vct = jnp.concatenate([vct, jnp.ones(vct.shape[:2] + (ONES_ROWS, ctx_len), vct.dtype)], axis=2)
    dve = dv + ONES_ROWS
    kern = functools.partial(_flash_t_kernel, rep=rep, bk=bk, n_kv=n_kv, chunk=chunk,
                             unroll=min(FLASH_UNROLL, n_kv), lead=FLASH_LEAD)
    return pl.pallas_call(
        kern,
        grid=(batch, groups, n_q),
        in_specs=[pl.BlockSpec((rep, dk, bq), lambda b, g, i: (g, 0, b * n_q + i)),
                  pl.BlockSpec((seq, dk), lambda b, g, i: (b, kcol0 + g)),
                  pl.BlockSpec((1, 1, n_kv, dve, bk), lambda b, g, i: (b, g, 0, 0, 0)),
                  pl.BlockSpec((ctx_len, dk), lambda b, g, i: (ctx_blk0 + b, kcol0 + g)),
                  pl.BlockSpec((1, 1, dve, ctx_len), lambda b, g, i: (b, g, 0, 0))],
        out_specs=pl.BlockSpec((bq, rep * dv), lambda b, g, i: (b * n_q + i, g)),
        out_shape=jax.ShapeDtypeStruct((rows, out_cols), BF16),
        scratch_shapes=[pltpu.VMEM((1, rep * bq), F32), pltpu.VMEM((dve, rep * bq), F32),
                        pltpu.VMEM((2, rep * bq // chunk, bk, chunk), F32),
                        pltpu.VMEM((2, rep * bq // chunk, 1, chunk), F32)],
        compiler_params=_params("arbitrary", "arbitrary", "arbitrary"),
        name=name,
    )(qt, k_arr, vt, k_arr, vct)


def _ctx_attention_kernel(q_ref, k_ref, v_ref, o_prev_ref, o_ref, *, rep, dk, dv, q_transposed):
    del o_prev_ref
    if q_transposed:
        heads = [q_ref[r].astype(F32).T.astype(k_ref.dtype) for r in range(rep)]
    else:
        heads = [q_ref[:, r * dk:(r + 1) * dk] for r in range(rep)]
    n_q = heads[0].shape[0]
    q = jnp.concatenate(heads, axis=0) if rep > 1 else heads[0]
    s = lax.dot_general(q, k_ref[...], (((1,), (1,)), ((), ())), preferred_element_type=F32)
    p = jnp.exp2(s - jnp.max(s, axis=-1, keepdims=True))
    o = jnp.dot(p.astype(v_ref.dtype), v_ref[...], preferred_element_type=F32) / jnp.sum(p, axis=-1, keepdims=True)
    for r in range(rep):
        o_ref[:, r * dv:(r + 1) * dv] = o[r * n_q:(r + 1) * n_q].astype(o_ref.dtype)


def _ctx_attention(q_arr, q_transposed, k_arr, v_arr, o_prev, *, batch, groups, rep, dk, dv, qcol0, kcol0, vcol0,
                   ctx_blk0, ctx_len, name):
    if q_transposed:
        q_spec = pl.BlockSpec((rep, dk, ctx_len), lambda b, g: (g, 0, ctx_blk0 + b))
    else:
        q_spec = pl.BlockSpec((ctx_len, rep * dk), lambda b, g: (ctx_blk0 + b, qcol0 + g))
    kern = functools.partial(_ctx_attention_kernel, rep=rep, dk=dk, dv=dv, q_transposed=q_transposed)
    return pl.pallas_call(
        kern,
        grid=(batch, groups),
        in_specs=[q_spec,
                  pl.BlockSpec((ctx_len, dk), lambda b, g: (ctx_blk0 + b, kcol0 + g)),
                  pl.BlockSpec((ctx_len, dv), lambda b, g: (ctx_blk0 + b, vcol0 + g)),
                  pl.BlockSpec(memory_space=pl.ANY)],
        out_specs=pl.BlockSpec((ctx_len, rep * dv), lambda b, g: (ctx_blk0 + b, g)),
        out_shape=jax.ShapeDtypeStruct(o_prev.shape, o_prev.dtype),
        input_output_aliases={3: 0},
        compiler_params=_params("arbitrary", "arbitrary"),
        name=name,
    )(q_arr, k_arr, v_arr, o_prev)


def _rpb_table_kernel(rpb_ref, o_ref, *, n_tiles, grid_rows):
    var = pl.program_id(0)
    h = pl.program_id(1)
    t_rep = jnp.where(var == 0, 0, jnp.where(var == 1, 1, n_tiles - 1))
    qc = lax.broadcasted_iota(jnp.int32, (GRID_W, GRID_W), 0)
    kc = lax.broadcasted_iota(jnp.int32, (GRID_W, GRID_W), 1)
    dc = kc - qc
    cstart = jnp.clip(qc - NA_WIN_COLS // 2, 0, GRID_W - NA_WIN_COLS)
    valid_c = (kc >= cstart) & (kc < cstart + NA_WIN_COLS)
    n_dc = 2 * NA_WIN_COLS - 1
    masked = jnp.full((GRID_W, GRID_W), MASK_VALUE, F32)
    toeplitz = {}
    for dr in range(-(NA_WIN_ROWS - 1), NA_WIN_ROWS):
        acc = jnp.zeros((GRID_W, GRID_W), F32)
        for b in range(n_dc):
            bias = rpb_ref[h, (dr + NA_WIN_ROWS - 1) * n_dc + b] * LOG2_E
            acc = jnp.where(dc == b - (NA_WIN_COLS - 1), bias, acc)
        toeplitz[dr] = jnp.where(valid_c, acc, MASK_VALUE)
    for ql in range(NA_TILE_ROWS):
        qr = NA_TILE_ROWS * t_rep + ql
        rstart = jnp.clip(qr - NA_WIN_ROWS // 2, 0, grid_rows - NA_WIN_ROWS)
        for kl in range(3 * NA_TILE_ROWS):
            kr = NA_TILE_ROWS * (t_rep - 1) + kl
            dr = kl - NA_TILE_ROWS - ql
            if abs(dr) > NA_WIN_ROWS - 1:
                blk = masked
            else:
                ok = ((kr >= rstart) & (kr < rstart + NA_WIN_ROWS)).astype(F32)
                blk = toeplitz[dr] * ok + MASK_VALUE * (1.0 - ok)
            o_ref[0, 0, ql * GRID_W:(ql + 1) * GRID_W, kl * GRID_W:(kl + 1) * GRID_W] = blk


def _rpb_table(rpb, n_tiles, grid_rows):
    heads = rpb.shape[0]
    tq = NA_TILE_ROWS * GRID_W
    kern = functools.partial(_rpb_table_kernel, n_tiles=n_tiles, grid_rows=grid_rows)
    return pl.pallas_call(
        kern,
        grid=(3, heads),
        in_specs=[pl.BlockSpec(memory_space=pltpu.SMEM)],
        out_specs=pl.BlockSpec((1, 1, tq, 3 * tq), lambda v, h: (v, h, 0, 0)),
        out_shape=jax.ShapeDtypeStruct((3, heads, tq, 3 * tq), F32),
        compiler_params=_params("arbitrary", "arbitrary"),
        name="na_rpb_table",
    )(rpb.reshape(heads, -1))


def _na_kernel(q_ref, kp_ref, kc_ref, kn_ref, kx_ref, vp_ref, vc_ref, vn_ref, vx_ref, tab_ref, o_ref, *, heads, dh):
    tq = q_ref.shape[0]
    k_refs = (kp_ref, kc_ref, kn_ref, kx_ref)
    v_refs = (vp_ref, vc_ref, vn_ref, vx_ref)
    ones = jnp.ones((tq, dh), BF16)

    def scores(h):
        sl = slice(h * dh, (h + 1) * dh)
        q = q_ref[:, sl]
        return [lax.dot_general(q, k_ref[:, sl], (((1,), (1,)), ((), ())), preferred_element_type=F32)
                for k_ref in k_refs]

    pending = [scores(h) for h in range(min(NA_LEAD, heads))]
    for h in range(heads):
        sl = slice(h * dh, (h + 1) * dh)
        s = pending.pop(0)
        if h + NA_LEAD < heads:
            pending.append(scores(h + NA_LEAD))
        s = [s[j] + tab_ref[0, h, :, j * tq:(j + 1) * tq] for j in range(3)] + [s[3]]
        m = jnp.max(jnp.maximum(jnp.maximum(s[0], s[1]), jnp.maximum(s[2], s[3])), axis=-1, keepdims=True)
        o = None
        for sj, v_ref in zip(s, v_refs):
            p = jnp.exp2(sj - m).astype(BF16)
            part = jnp.dot(p, jnp.concatenate([v_ref[:, sl], ones], axis=1), preferred_element_type=F32)
            o = part if o is None else o + part
        o_ref[:, sl] = (o[:, 0:dh] / o[:, dh:2 * dh]).astype(o_ref.dtype)


def _neighborhood_attention(qkv, table, *, batch, seq, ctx_len, heads, dh, hb):
    rows = qkv.shape[0]
    tq = NA_TILE_ROWS * GRID_W
    assert ctx_len == tq and seq % tq == 0
    n_tiles = seq // tq
    n_hg = heads // hb
    ctx_blk0 = batch * seq // ctx_len
    wb = hb * dh

    def var(t):
        return jnp.where(t == 0, 0, jnp.where(t == n_tiles - 1, 2, 1))

    def spec(col0, shift):
        if shift is None:
            return pl.BlockSpec((tq, wb), lambda b, g, t: (ctx_blk0 + b, col0 + g))
        return pl.BlockSpec((tq, wb), lambda b, g, t: (b * n_tiles + jnp.clip(t + shift, 0, n_tiles - 1), col0 + g))

    kern = functools.partial(_na_kernel, heads=hb, dh=dh)
    return pl.pallas_call(
        kern,
        grid=(batch, n_hg, n_tiles),
        in_specs=[spec(0, 0),
                  spec(n_hg, -1), spec(n_hg, 0), spec(n_hg, 1), spec(n_hg, None),
                  spec(2 * n_hg, -1), spec(2 * n_hg, 0), spec(2 * n_hg, 1), spec(2 * n_hg, None),
                  pl.BlockSpec((1, hb, tq, 3 * tq), lambda b, g, t: (var(t), g, 0, 0))],
        out_specs=pl.BlockSpec((tq, wb), lambda b, g, t: (b * n_tiles + t, g)),
        out_shape=jax.ShapeDtypeStruct((rows, heads * dh), BF16),
        compiler_params=_params("arbitrary", "arbitrary", "arbitrary"),
        name="na_attention",
    )(qkv, qkv, qkv, qkv, qkv, qkv, qkv, qkv, qkv, table)


def _final_norm_kernel(x_ref, g_ref, o_ref):
    o_ref[...] = _rms(x_ref[...], g_ref[...])


def _final_norm(x, g, n_row_tiles):
    bm = ROW_TILE
    d = x.shape[1]
    return pl.pallas_call(
        _final_norm_kernel,
        grid=(n_row_tiles,),
        in_specs=[pl.BlockSpec((bm, d), lambda m: (m, 0)), pl.BlockSpec((1, d), lambda m: (0, 0))],
        out_specs=pl.BlockSpec((bm, d), lambda m: (m, 0)),
        out_shape=jax.ShapeDtypeStruct((n_row_tiles * bm, d), F32),
        compiler_params=_params("arbitrary"),
        name="final_norm",
    )(x, g.reshape(1, d))


def _rope_tables(n_tok, rot_dim, batch, n_ctx_rows):
    quarter = rot_dim // 4
    t = jnp.arange(n_tok)
    row = (t // GRID_W).astype(F32)
    col = (t % GRID_W).astype(F32)
    inv = ROPE_THETA ** (-jnp.arange(quarter, dtype=F32) / quarter)
    ar = row[:, None] * inv
    ac = col[:, None] * inv
    ang = jnp.concatenate([ar, ar, ac, ac], axis=-1)
    cos, sin = jnp.cos(ang), jnp.sin(ang)
    first = (jnp.arange(rot_dim) % (2 * quarter)) < quarter
    sin_a = jnp.where(first, -sin, 0.0)
    sin_b = jnp.where(first, 0.0, sin)
    pad = LANES - rot_dim

    def full(tab, fill):
        tab = jnp.pad(tab, ((0, 0), (0, pad)), constant_values=fill)
        tab = jnp.tile(tab, (batch, 1))
        return jnp.concatenate([tab, jnp.full((n_ctx_rows, LANES), fill, F32)], axis=0)

    return full(cos, 1.0), full(sin_a, 0.0), full(sin_b, 0.0)


def kernel(x, c, ctx, c_ctx, ada_w, ada_b, mix_norm, ffn_norm, ffn_up, ffn_conv_w, ffn_conv_b, ffn_down, na_wqkv, na_rpb, na_wo, mla_wq_a, mla_q_norm, mla_wq_b, mla_wkv_a, mla_kv_norm, mla_wkv_b, mla_wo, gqa_wqkv, gqa_q_norm, gqa_k_norm, gqa_wo, final_norm):
    batch, seq, d = x.shape
    ctx_len = ctx.shape[1]
    depth = ada_w.shape[0]
    bm = ROW_TILE
    lat_rows = batch * seq
    ctx_rows = batch * ctx_len
    pad_rows = -(lat_rows + ctx_rows) % bm
    rows = lat_rows + ctx_rows + pad_rows
    assert seq % bm == 0 and batch < 8
    assert seq & (seq - 1) == 0 and ctx_len & (ctx_len - 1) == 0
    tiles_per_batch = seq // bm
    lat_tiles = lat_rows // bm
    all_tiles = rows // bm
    grid_rows = seq // GRID_W

    def mod_row(m):
        return jnp.minimum(m // tiles_per_batch, batch)

    xs = jnp.concatenate([x.reshape(lat_rows, d), ctx.reshape(ctx_rows, d), jnp.zeros((pad_rows, d), F32)], axis=0)
    c_rows = jnp.zeros((8, d), F32).at[:batch].set(c).at[batch].set(c_ctx)
    mod_all = _modulation(c_rows, ada_w, ada_b)

    cos_b, sa_b, sb_b = _rope_tables(seq, MLA_ROPE_DIM, batch, ctx_rows + pad_rows)
    cos_c, sa_c, sb_c = _rope_tables(seq, GQA_HEAD_DIM, batch, ctx_rows + pad_rows)

    def row_spec(width):
        return pl.BlockSpec((bm, width), lambda m, n: (m, 0))

    rope_specs = [row_spec(LANES)] * 3

    for i in range(depth):
        last = i == depth - 1
        kind, j = i % N_MIXERS, i // N_MIXERS
        mod = mod_all[i].reshape(8, 1, 6 * d)
        n_tiles = lat_tiles if last else all_tiles

        if kind == 0:
            scale = (d // NA_HEADS) ** -0.5 * LOG2_E
            colscale = jnp.concatenate([jnp.full((1, d), scale, F32), jnp.ones((1, 2 * d), F32)], axis=1)
            bn = d // 2
            qkv = _norm_matmul(
                xs, 0, d, mix_norm[i], mod, (1, 0), na_wqkv[j].astype(BF16), bn,
                [colscale], [pl.BlockSpec((1, bn), lambda m, n: (0, n))],
                jax.ShapeDtypeStruct((rows, 3 * d), BF16), pl.BlockSpec((bm, bn), lambda m, n: (m, n)),
                _epi_colscale, all_tiles, mod_row, "na_qkv")
            table = _rpb_table(na_rpb[j], seq // (NA_TILE_ROWS * GRID_W), grid_rows)
            dh = d // NA_HEADS
            o = _neighborhood_attention(qkv, table, batch=batch, seq=seq, ctx_len=ctx_len,
                                        heads=NA_HEADS, dh=dh, hb=NA_HEADS_PER_STEP)
            if not last:
                o = _ctx_attention(qkv, False, qkv, qkv, o, batch=batch, groups=NA_HEADS, rep=1, dk=dh, dv=dh,
                                   qcol0=0, kcol0=NA_HEADS, vcol0=2 * NA_HEADS, ctx_blk0=lat_rows // ctx_len,
                                   ctx_len=ctx_len, name="na_ctx_attention")
            wo = na_wo[j]
        elif kind == 1:
            scale = (MLA_NOPE_DIM + MLA_ROPE_DIM) ** -0.5 * LOG2_E
            q_rank = mla_wq_a.shape[2]
            kv_rank = mla_kv_norm.shape[1]
            hq = 2 * LANES
            w1 = jnp.concatenate([mla_wq_a[j], mla_wkv_a[j], jnp.zeros((d, LANES - MLA_ROPE_DIM), F32)], axis=1)
            n1 = w1.shape[1]
            qc, kpe = _norm_matmul(
                xs, 0, d, mix_norm[i], mod, (1, 0), w1.astype(BF16), n1,
                [cos_b, sa_b, sb_b], rope_specs,
                (jax.ShapeDtypeStruct((rows, q_rank + kv_rank), F32), jax.ShapeDtypeStruct((rows, LANES), BF16)),
                (pl.BlockSpec((bm, q_rank + kv_rank), lambda m, n: (m, 0)), pl.BlockSpec((bm, LANES), lambda m, n: (m, 0))),
                _epi_mla_a, all_tiles, mod_row, "mla_a")
            wqb = mla_wq_b[j].reshape(q_rank, MLA_HEADS, MLA_NOPE_DIM + MLA_ROPE_DIM)
            wqb = jnp.pad(wqb, ((0, 0), (0, 0), (0, hq - MLA_NOPE_DIM - MLA_ROPE_DIM))).reshape(q_rank, MLA_HEADS * hq)
            bn = 4 * hq
            qt = _norm_matmul(
                qc, 0, q_rank, mla_q_norm[j], None, None, wqb.astype(BF16), bn,
                [cos_b, sa_b, sb_b], rope_specs,
                jax.ShapeDtypeStruct((MLA_HEADS, hq, rows), BF16),
                pl.BlockSpec((bn // hq, hq, bm), lambda m, n: (n, 0, m)),
                functools.partial(_epi_mla_q, scale=scale), all_tiles, mod_row, "mla_q")
            wkvb = mla_wkv_b[j].reshape(kv_rank, MLA_HEADS, MLA_NOPE_DIM + MLA_V_DIM)
            wk = wkvb[:, :, :MLA_NOPE_DIM].reshape(kv_rank, MLA_HEADS * MLA_NOPE_DIM)
            wv = wkvb[:, :, MLA_NOPE_DIM:].reshape(kv_rank, MLA_HEADS * MLA_V_DIM)
            bnk = 8 * MLA_NOPE_DIM
            k = _norm_matmul(
                qc, 1, kv_rank, mla_kv_norm[j], None, None, wk.astype(BF16), bnk,
                [kpe], [row_spec(LANES)],
                jax.ShapeDtypeStruct((rows, MLA_HEADS * hq), BF16), pl.BlockSpec((bm, 2 * bnk), lambda m, n: (m, n)),
                _epi_mla_k, all_tiles, mod_row, "mla_k")
            v = _norm_matmul(
                qc, 1, kv_rank, mla_kv_norm[j], None, None, wv.astype(BF16), bnk,
                [], [],
                jax.ShapeDtypeStruct((rows, MLA_HEADS * MLA_V_DIM), BF16), pl.BlockSpec((bm, bnk), lambda m, n: (m, n)),
                _epi_plain, all_tiles, mod_row, "mla_v")
            out_cols = MLA_HEADS * MLA_V_DIM
            o = _attention_t(qt, k, v, batch=batch, seq=seq, ctx_len=ctx_len, groups=MLA_HEADS, dv=MLA_V_DIM,
                             kcol0=0, vcol0=0, bq=min(FLASH_Q_TILE_MLA, seq), bk=FLASH_KEY_BLOCK, chunk=FLASH_CHUNK,
                             out_cols=out_cols, name="mla_attention")
            if not last:
                o = _ctx_attention(qt, True, k, v, o, batch=batch, groups=MLA_HEADS, rep=1, dk=hq, dv=MLA_V_DIM,
                                   qcol0=None, kcol0=0, vcol0=0, ctx_blk0=lat_rows // ctx_len, ctx_len=ctx_len,
                                   name="mla_ctx_attention")
            wo = mla_wo[j]
        else:
            scale = GQA_HEAD_DIM ** -0.5 * LOG2_E
            rep = GQA_HEADS // GQA_KV_HEADS
            nq_cols = GQA_HEADS * GQA_HEAD_DIM
            nk_cols = GQA_KV_HEADS * GQA_HEAD_DIM
            bn = nk_cols
            gain = jnp.concatenate([jnp.tile(gqa_q_norm[j], GQA_HEADS) * scale, jnp.tile(gqa_k_norm[j], GQA_KV_HEADS),
                                    jnp.ones((nk_cols,), F32)]).reshape(1, -1)
            dh = GQA_HEAD_DIM
            n_q_tiles = nq_cols // bn
            qt, kv = _norm_matmul(
                xs, 0, d, mix_norm[i], mod, (1, 0), gqa_wqkv[j].astype(BF16), bn,
                [gain, cos_c, sa_c, sb_c], [pl.BlockSpec((1, bn), lambda m, n: (0, n))] + rope_specs,
                (jax.ShapeDtypeStruct((GQA_HEADS, dh, rows), BF16), jax.ShapeDtypeStruct((rows, 2 * nk_cols), BF16)),
                (pl.BlockSpec((bn // dh, dh, bm), lambda m, n: (jnp.minimum(n, n_q_tiles - 1), 0, m)),
                 pl.BlockSpec((bm, bn), lambda m, n: (m, jnp.maximum(n - n_q_tiles, 0)))),
                functools.partial(_epi_gqa, n_q_tiles=n_q_tiles), all_tiles, mod_row, "gqa_qkv")
            o = _attention_t(qt, kv, kv, batch=batch, seq=seq, ctx_len=ctx_len, groups=GQA_KV_HEADS, dv=dh,
                             kcol0=0, vcol0=GQA_KV_HEADS, bq=min(FLASH_Q_TILE_GQA, seq), bk=FLASH_KEY_BLOCK,
                             chunk=FLASH_CHUNK,
                             out_cols=nq_cols, name="gqa_attention")
            if not last:
                o = _ctx_attention(qt, True, kv, kv, o, batch=batch, groups=GQA_KV_HEADS, rep=rep, dk=dh, dv=dh,
                                   qcol0=None, kcol0=0, vcol0=GQA_KV_HEADS, ctx_blk0=lat_rows // ctx_len,
                                   ctx_len=ctx_len, name="gqa_ctx_attention")
            wo = gqa_wo[j]

        xs = _matmul_residual(o, wo.astype(BF16), xs, mod, 2, d // 2, n_tiles, mod_row, "attn_out")
        a = _ffn_up(xs, ffn_norm[i], mod, ffn_up[i].astype(BF16), ffn_conv_w[i], ffn_conv_b[i], FFN_COL_TILE,
                    n_tiles, mod_row, lat_rows, seq, ctx_len, "ffn_up")
        xs = _matmul_residual(a, ffn_down[i].astype(BF16), xs, mod, 5, d // 4, n_tiles, mod_row, "ffn_down")

    out = _final_norm(xs, final_norm, lat_tiles)
    return out.reshape(batch, seq, d)
```

```python
import functools

import jax
import jax.numpy as jnp
from jax import lax
from jax.experimental import pallas as pl
from jax.experimental.pallas import tpu as pltpu

F32 = jnp.float32
BF16 = jnp.bfloat16

GRID_W = 64
NA_HEADS = 16
NA_WIN_ROWS = 8
NA_WIN_COLS = 16
MLA_HEADS = 16
MLA_NOPE_DIM = 128
MLA_ROPE_DIM = 64
MLA_V_DIM = 128
GQA_HEADS = 16
GQA_KV_HEADS = 4
GQA_HEAD_DIM = 128
CONV_WIDTH = 3
ROPE_THETA = 10000.0
NORM_EPS = 1e-6
N_MIXERS = 3
LOG2_E = 1.4426950408889634

LANES = 128
ROW_TILE = 1024
HALO = 8
ONES_ROWS = 16
FLASH_KEY_BLOCK = 512
FLASH_CHUNK = 256
FLASH_Q_TILE_MLA = 2048
FLASH_Q_TILE_GQA = 512
NA_HEADS_PER_STEP = 16
NA_LEAD = 1
FFN_COL_TILE = 512
FLASH_UNROLL = 8
FLASH_LEAD = 2
NA_TILE_ROWS = 4
MASK_VALUE = -1e30
VMEM_LIMIT = 56 * 1024 * 1024


def _params(*sem):
    return pltpu.CompilerParams(dimension_semantics=sem, vmem_limit_bytes=VMEM_LIMIT)


def _rms(x, g):
    y = x * lax.rsqrt(jnp.mean(x * x, axis=-1, keepdims=True) + NORM_EPS)
    return y * g


def _rope(y, cos, sin_a, sin_b, quarter):
    return (y * cos + pltpu.roll(y, LANES - quarter, 1) * sin_a
            + pltpu.roll(y, quarter, 1) * sin_b)


def _mod_kernel(c_ref, w_ref, b_ref, o_ref):
    s = jax.nn.silu(c_ref[...])
    o_ref[0] = jnp.dot(s.astype(BF16), w_ref[0].astype(BF16), preferred_element_type=F32) + b_ref[0]


def _modulation(c_rows, ada_w, ada_b):
    depth, d, n = ada_w.shape
    bn = n // 8
    return pl.pallas_call(
        _mod_kernel,
        grid=(depth, n // bn),
        in_specs=[pl.BlockSpec((8, d), lambda l, j: (0, 0)),
                  pl.BlockSpec((1, d, bn), lambda l, j: (l, 0, j)),
                  pl.BlockSpec((1, 1, bn), lambda l, j: (l, 0, j))],
        out_specs=pl.BlockSpec((1, 8, bn), lambda l, j: (l, 0, j)),
        out_shape=jax.ShapeDtypeStruct((depth, 8, n), F32),
        compiler_params=_params("arbitrary", "arbitrary"),
        name="adaln_mod",
    )(c_rows, ada_w, ada_b.reshape(depth, 1, n))


def _norm_matmul_kernel(*refs, has_mod, n_extra, epilogue):
    if has_mod:
        x_ref, g_ref, sc_ref, sh_ref, w_ref = refs[:5]
        rest = refs[5:]
    else:
        x_ref, g_ref, w_ref = refs[:3]
        rest = refs[3:]
    extras, outs, h_sc = rest[:n_extra], rest[n_extra:-1], rest[-1]
    n = pl.program_id(1)

    @pl.when(n == 0)
    def _():
        h = _rms(x_ref[...], g_ref[...])
        if has_mod:
            h = h * (1.0 + sc_ref[0]) + sh_ref[0]
        h_sc[...] = h.astype(BF16)

    acc = jnp.dot(h_sc[...], w_ref[...], preferred_element_type=F32)
    epilogue(acc, n, extras, outs)


def _norm_matmul(x, xcol, kdim, g, mod, mod_chunks, w, bn, extras, extra_specs, out_shapes, out_specs,
                 epilogue, n_row_tiles, mod_row, name):
    bm = ROW_TILE
    n_col = w.shape[1] // bn
    has_mod = mod is not None
    in_specs = [pl.BlockSpec((bm, kdim), lambda m, n: (m, xcol)),
                pl.BlockSpec((1, kdim), lambda m, n: (0, 0))]
    args = [x, g.reshape(1, kdim)]
    if has_mod:
        sc_chunk, sh_chunk = mod_chunks
        in_specs += [pl.BlockSpec((1, 1, kdim), lambda m, n: (mod_row(m), 0, sc_chunk)),
                     pl.BlockSpec((1, 1, kdim), lambda m, n: (mod_row(m), 0, sh_chunk))]
        args += [mod, mod]
    in_specs.append(pl.BlockSpec((kdim, bn), lambda m, n: (0, n)))
    args.append(w)
    in_specs += extra_specs
    args += extras
    kern = functools.partial(_norm_matmul_kernel, has_mod=has_mod, n_extra=len(extras), epilogue=epilogue)
    return pl.pallas_call(
        kern,
        grid=(n_row_tiles, n_col),
        in_specs=in_specs,
        out_specs=out_specs,
        out_shape=out_shapes,
        scratch_shapes=[pltpu.VMEM((bm, kdim), BF16)],
        compiler_params=_params("arbitrary", "arbitrary"),
        name=name,
    )(*args)


def _epi_colscale(acc, n, extras, outs):
    (cs_ref,), (o_ref,) = extras, outs
    o_ref[...] = (acc * cs_ref[...]).astype(o_ref.dtype)


def _epi_plain(acc, n, extras, outs):
    (o_ref,) = outs
    o_ref[...] = acc.astype(o_ref.dtype)


def _epi_mla_a(acc, n, extras, outs):
    cos_ref, sa_ref, sb_ref = extras
    qc_ref, kpe_ref = outs
    wide = qc_ref.shape[1]
    qc_ref[...] = acc[:, :wide]
    kpe = _rope(acc[:, wide:], cos_ref[...], sa_ref[...], sb_ref[...], MLA_ROPE_DIM // 4)
    kpe_ref[...] = kpe.astype(kpe_ref.dtype)


def _epi_mla_q(acc, n, extras, outs, *, scale):
    cos_ref, sa_ref, sb_ref = extras
    (qt_ref,) = outs
    for c in range(acc.shape[1] // LANES):
        y = acc[:, c * LANES:(c + 1) * LANES]
        if c % 2 == 1:
            y = _rope(y, cos_ref[...], sa_ref[...], sb_ref[...], MLA_ROPE_DIM // 4)
        qt_ref[c // 2, (c % 2) * LANES:(c % 2 + 1) * LANES, :] = (y * scale).T.astype(qt_ref.dtype)


def _epi_mla_k(acc, n, extras, outs):
    (kpe_ref,), (o_ref,) = extras, outs
    kpe = kpe_ref[...]
    for c in range(acc.shape[1] // LANES):
        o_ref[:, (2 * c) * LANES:(2 * c + 1) * LANES] = acc[:, c * LANES:(c + 1) * LANES].astype(o_ref.dtype)
        o_ref[:, (2 * c + 1) * LANES:(2 * c + 2) * LANES] = kpe


def _epi_gqa(acc, n, extras, outs, *, n_q_tiles):
    gain_ref, cos_ref, sa_ref, sb_ref = extras
    qt_ref, kv_ref = outs

    def head(c):
        y = _rms(acc[:, c * LANES:(c + 1) * LANES], gain_ref[:, c * LANES:(c + 1) * LANES])
        return _rope(y, cos_ref[...], sa_ref[...], sb_ref[...], GQA_HEAD_DIM // 4)

    @pl.when(n < n_q_tiles)
    def _():
        for c in range(acc.shape[1] // LANES):
            qt_ref[c] = head(c).T.astype(qt_ref.dtype)

    @pl.when(n == n_q_tiles)
    def _():
        for c in range(acc.shape[1] // LANES):
            kv_ref[:, c * LANES:(c + 1) * LANES] = head(c).astype(kv_ref.dtype)

    @pl.when(n > n_q_tiles)
    def _():
        kv_ref[...] = acc.astype(kv_ref.dtype)


def _mm_res_kernel(a_ref, w_ref, x_ref, g_ref, o_ref):
    acc = jnp.dot(a_ref[...], w_ref[...], preferred_element_type=F32)
    o_ref[...] = x_ref[...] + g_ref[0] * acc


def _matmul_residual(a, w, x, mod, gate_chunk, bn, n_row_tiles, mod_row, name):
    bm = ROW_TILE
    kdim, n_out = w.shape
    assert n_out == x.shape[1] and n_out % bn == 0
    per_chunk = n_out // bn
    return pl.pallas_call(
        _mm_res_kernel,
        grid=(n_row_tiles, per_chunk),
        in_specs=[pl.BlockSpec((bm, kdim), lambda m, n: (m, 0)),
                  pl.BlockSpec((kdim, bn), lambda m, n: (0, n)),
                  pl.BlockSpec((bm, bn), lambda m, n: (m, n)),
                  pl.BlockSpec((1, 1, bn), lambda m, n: (mod_row(m), 0, gate_chunk * per_chunk + n))],
        out_specs=pl.BlockSpec((bm, bn), lambda m, n: (m, n)),
        out_shape=jax.ShapeDtypeStruct(x.shape, x.dtype),
        input_output_aliases={2: 0},
        compiler_params=_params("arbitrary", "arbitrary"),
        name=name,
    )(a, w, x, mod)


def _ffn_up_kernel(x_ref, xp_ref, xn_ref, g_ref, sc_ref, sh_ref, wg_ref, wv_ref, cw_ref, cb_ref, o_ref, h_sc,
                   *, lat_rows, lat_seq, ctx_seq):
    bm = x_ref.shape[0]
    m = pl.program_id(0)

    @pl.when(pl.program_id(1) == 0)
    def _():
        def nm(x):
            return (_rms(x, g_ref[...]) * (1.0 + sc_ref[0]) + sh_ref[0]).astype(BF16)
        h_sc[0:bm, :] = nm(x_ref[...])
        h_sc[bm:bm + 2 * HALO, :] = nm(jnp.concatenate([xn_ref[...], xp_ref[...]], axis=0))

    ext = bm + 2 * HALO
    row = m * bm + lax.broadcasted_iota(jnp.int32, (bm, 1), 0)
    in_ctx = row >= lat_rows
    pos = jnp.where(in_ctx, (row - lat_rows) & (ctx_seq - 1), row & (lat_seq - 1))
    seq = jnp.where(in_ctx, ctx_seq, lat_seq)
    gate = jnp.dot(h_sc[...], wg_ref[...], preferred_element_type=F32)
    val = jnp.dot(h_sc[0:bm, :], wv_ref[...], preferred_element_type=F32)
    is_first = pos == 0
    is_last = pos == seq - 1
    for c0 in range(0, o_ref.shape[1], LANES):
        cs = slice(c0, c0 + LANES)
        g = gate[:, cs]
        g_prev = pltpu.roll(g, 1, 0)[0:bm]
        g_next = pltpu.roll(g, ext - 1, 0)[0:bm]
        g_prev = jnp.where(is_first, 0.0, g_prev)
        g_next = jnp.where(is_last, 0.0, g_next)
        cw = cw_ref[:, cs]
        z = cb_ref[:, cs] + g_prev * cw[0:1] + g[0:bm] * cw[1:2] + g_next * cw[2:3]
        o_ref[:, cs] = (jax.nn.silu(z) * val[:, cs]).astype(o_ref.dtype)


def _ffn_up(x, g, mod, w_up, conv_w, conv_b, bn, n_row_tiles, mod_row, lat_rows, lat_seq, ctx_seq, name):
    bm = ROW_TILE
    rows, d = x.shape
    f = w_up.shape[1] // 2
    n_col = f // bn
    last_halo = rows // HALO - 1
    per = bm // HALO
    kern = functools.partial(_ffn_up_kernel, lat_rows=lat_rows, lat_seq=lat_seq, ctx_seq=ctx_seq)
    return pl.pallas_call(
        kern,
        grid=(n_row_tiles, n_col),
        in_specs=[pl.BlockSpec((bm, d), lambda m, n: (m, 0)),
                  pl.BlockSpec((HALO, d), lambda m, n: (jnp.maximum(m * per - 1, 0), 0)),
                  pl.BlockSpec((HALO, d), lambda m, n: (jnp.minimum((m + 1) * per, last_halo), 0)),
                  pl.BlockSpec((1, d), lambda m, n: (0, 0)),
                  pl.BlockSpec((1, 1, d), lambda m, n: (mod_row(m), 0, 4)),
                  pl.BlockSpec((1, 1, d), lambda m, n: (mod_row(m), 0, 3)),
                  pl.BlockSpec((d, bn), lambda m, n: (0, n)),
                  pl.BlockSpec((d, bn), lambda m, n: (0, n_col + n)),
                  pl.BlockSpec((CONV_WIDTH, bn), lambda m, n: (0, n)),
                  pl.BlockSpec((1, bn), lambda m, n: (0, n))],
        out_specs=pl.BlockSpec((bm, bn), lambda m, n: (m, n)),
        out_shape=jax.ShapeDtypeStruct((rows, f), BF16),
        scratch_shapes=[pltpu.VMEM((bm + 2 * HALO, d), BF16)],
        compiler_params=_params("arbitrary", "arbitrary"),
        name=name,
    )(x, x, x, g.reshape(1, d), mod, mod, w_up, w_up, conv_w, conv_b.reshape(1, f))


def _flash_t_kernel(qt_ref, k_ref, vt_ref, kc_ref, vct_ref, o_ref, m_sc, acc_sc, s_sc, mx_sc, *, rep, bk, n_kv,
                    chunk, unroll, lead):
    bq = qt_ref.shape[2]
    dv = vt_ref.shape[3] - ONES_ROWS
    n_chunks = rep * bq // chunk

    def scores(k, c, half, n_keys):
        r, off = divmod(c * chunk, bq)
        s = jnp.dot(k, qt_ref[r, :, off:off + chunk], preferred_element_type=F32)
        s_sc[half, c, 0:n_keys, :] = s
        mx_sc[half, c] = jnp.max(s, axis=0, keepdims=True)

    def softmax_pv(half, n_keys, vt, c, is_first):
        cs = slice(c * chunk, (c + 1) * chunk)
        s = s_sc[half, c, 0:n_keys, :]
        mx = mx_sc[half, c]
        if is_first:
            m_new = mx
            p = jnp.exp2(s - m_new)
            acc_sc[:, cs] = jnp.dot(vt, p.astype(vt.dtype), preferred_element_type=F32)
        else:
            m_prev = m_sc[:, cs]
            m_new = jnp.maximum(m_prev, mx)
            alpha = jnp.exp2(m_prev - m_new)
            p = jnp.exp2(s - m_new)
            acc_sc[:, cs] = alpha * acc_sc[:, cs] + jnp.dot(vt, p.astype(vt.dtype), preferred_element_type=F32)
        m_sc[:, cs] = m_new

    kc = kc_ref[...]
    k0 = k_ref[0:bk, :]
    n_ctx = kc_ref.shape[0]
    assert n_ctx <= bk
    for c in range(n_chunks):
        scores(kc, c, 1, n_ctx)
    for c in range(n_chunks):
        scores(k0, c, 0, bk)
        softmax_pv(1, n_ctx, vct_ref[0, 0], c, True)

    def body(i, carry):
        def issue_scores(t):
            u, c = divmod(t, n_chunks)
            j_next = jnp.minimum(unroll * i + u + 1, n_kv - 1)
            off = pl.multiple_of(j_next * bk, bk)
            scores(k_ref[pl.ds(off, bk), :], c, (u + 1) % 2, bk)

        def consume(t):
            u, c = divmod(t, n_chunks)
            softmax_pv(u % 2, bk, vt_ref[0, 0, unroll * i + u], c, False)

        n_tasks = unroll * n_chunks
        for t in range(n_tasks + lead):
            if t < n_tasks:
                issue_scores(t)
            if t >= lead:
                consume(t - lead)
        return carry

    assert unroll % 2 == 0 and n_kv % unroll == 0 and lead < n_chunks
    lax.fori_loop(0, n_kv // unroll, body, 0)
    for r in range(rep):
        ot = acc_sc[0:dv, r * bq:(r + 1) * bq] / acc_sc[dv:dv + 1, r * bq:(r + 1) * bq]
        o_ref[:, r * dv:(r + 1) * dv] = ot.T.astype(o_ref.dtype)


def _attention_t(qt, k_arr, v_arr, *, batch, seq, ctx_len, groups, dv, kcol0, vcol0, bq, bk, chunk, out_cols, name):
    heads, dk, rows = qt.shape
    lat_rows = batch * seq
    rep = heads // groups
    n_q, n_kv = seq // bq, seq // bk
    ctx_blk0 = lat_rows // ctx_len
    v_lat = v_arr[:lat_rows, vcol0 * dv:(vcol0 + groups) * dv]
    vt = v_lat.reshape(batch, n_kv, bk, groups, dv).transpose(0, 3, 1, 4, 2)
    vt = jnp.concatenate([vt, jnp.ones(vt.shape[:3] + (ONES_ROWS, bk), vt.dtype)], axis=3)
    v_ctx = v_arr[lat_rows:lat_rows + batch * ctx_len, vcol0 * dv:(vcol0 + groups) * dv]
    vct = v_ctx.reshape(batch, ctx_len, groups, dv).transpose(0, 2, 3, 1)
    vct = jnp.concatenate([vct, jnp.ones(vct.shape[:2] + (ONES_ROWS, ctx_len), vct.dtype)], axis=2)
    dve = dv + ONES_ROWS
    kern = functools.partial(_flash_t_kernel, rep=rep, bk=bk, n_kv=n_kv, chunk=chunk,
                             unroll=min(FLASH_UNROLL, n_kv), lead=FLASH_LEAD)
    return pl.pallas_call(
        kern,
        grid=(batch, groups, n_q),
        in_specs=[pl.BlockSpec((rep, dk, bq), lambda b, g, i: (g, 0, b * n_q + i)),
                  pl.BlockSpec((seq, dk), lambda b, g, i: (b, kcol0 + g)),
                  pl.BlockSpec((1, 1, n_kv, dve, bk), lambda b, g, i: (b, g, 0, 0, 0)),
                  pl.BlockSpec((ctx_len, dk), lambda b, g, i: (ctx_blk0 + b, kcol0 + g)),
                  pl.BlockSpec((1, 1, dve, ctx_len), lambda b, g, i: (b, g, 0, 0))],
        out_specs=pl.BlockSpec((bq, rep * dv), lambda b, g, i: (b * n_q + i, g)),
        out_shape=jax.ShapeDtypeStruct((rows, out_cols), BF16),
        scratch_shapes=[pltpu.VMEM((1, rep * bq), F32), pltpu.VMEM((dve, rep * bq), F32),
                        pltpu.VMEM((2, rep * bq // chunk, bk, chunk), F32),
                        pltpu.VMEM((2, rep * bq // chunk, 1, chunk), F32)],
        compiler_params=_params("arbitrary", "arbitrary", "arbitrary"),
        name=name,
    )(qt, k_arr, vt, k_arr, vct)


def _ctx_attention_kernel(q_ref, k_ref, v_ref, o_prev_ref, o_ref, *, rep, dk, dv, q_transposed):
    del o_prev_ref
    if q_transposed:
        heads = [q_ref[r].astype(F32).T.astype(k_ref.dtype) for r in range(rep)]
    else:
        heads = [q_ref[:, r * dk:(r + 1) * dk] for r in range(rep)]
    n_q = heads[0].shape[0]
    q = jnp.concatenate(heads, axis=0) if rep > 1 else heads[0]
    s = lax.dot_general(q, k_ref[...], (((1,), (1,)), ((), ())), preferred_element_type=F32)
    p = jnp.exp2(s - jnp.max(s, axis=-1, keepdims=True))
    o = jnp.dot(p.astype(v_ref.dtype), v_ref[...], preferred_element_type=F32) / jnp.sum(p, axis=-1, keepdims=True)
    for r in range(rep):
        o_ref[:, r * dv:(r + 1) * dv] = o[r * n_q:(r + 1) * n_q].astype(o_ref.dtype)


def _ctx_attention(q_arr, q_transposed, k_arr, v_arr, o_prev, *, batch, groups, rep, dk, dv, qcol0, kcol0, vcol0,
                   ctx_blk0, ctx_len, name):
    if q_transposed:
        q_spec = pl.BlockSpec((rep, dk, ctx_len), lambda b, g: (g, 0, ctx_blk0 + b))
    else:
        q_spec = pl.BlockSpec((ctx_len, rep * dk), lambda b, g: (ctx_blk0 + b, qcol0 + g))
    kern = functools.partial(_ctx_attention_kernel, rep=rep, dk=dk, dv=dv, q_transposed=q_transposed)
    return pl.pallas_call(
        kern,
        grid=(batch, groups),
        in_specs=[q_spec,
                  pl.BlockSpec((ctx_len, dk), lambda b, g: (ctx_blk0 + b, kcol0 + g)),
                  pl.BlockSpec((ctx_len, dv), lambda b, g: (ctx_blk0 + b, vcol0 + g)),
                  pl.BlockSpec(memory_space=pl.ANY)],
        out_specs=pl.BlockSpec((ctx_len, rep * dv), lambda b, g: (ctx_blk0 + b, g)),
        out_shape=jax.ShapeDtypeStruct(o_prev.shape, o_prev.dtype),
        input_output_aliases={3: 0},
        compiler_params=_params("arbitrary", "arbitrary"),
        name=name,
    )(q_arr, k_arr, v_arr, o_prev)


def _rpb_table_kernel(rpb_ref, o_ref, *, n_tiles, grid_rows):
    var = pl.program_id(0)
    h = pl.program_id(1)
    t_rep = jnp.where(var == 0, 0, jnp.where(var == 1, 1, n_tiles - 1))
    qc = lax.broadcasted_iota(jnp.int32, (GRID_W, GRID_W), 0)
    kc = lax.broadcasted_iota(jnp.int32, (GRID_W, GRID_W), 1)
    dc = kc - qc
    cstart = jnp.clip(qc - NA_WIN_COLS // 2, 0, GRID_W - NA_WIN_COLS)
    valid_c = (kc >= cstart) & (kc < cstart + NA_WIN_COLS)
    n_dc = 2 * NA_WIN_COLS - 1
    masked = jnp.full((GRID_W, GRID_W), MASK_VALUE, F32)
    toeplitz = {}
    for dr in range(-(NA_WIN_ROWS - 1), NA_WIN_ROWS):
        acc = jnp.zeros((GRID_W, GRID_W), F32)
        for b in range(n_dc):
            bias = rpb_ref[h, (dr + NA_WIN_ROWS - 1) * n_dc + b] * LOG2_E
            acc = jnp.where(dc == b - (NA_WIN_COLS - 1), bias, acc)
        toeplitz[dr] = jnp.where(valid_c, acc, MASK_VALUE)
    for ql in range(NA_TILE_ROWS):
        qr = NA_TILE_ROWS * t_rep + ql
        rstart = jnp.clip(qr - NA_WIN_ROWS // 2, 0, grid_rows - NA_WIN_ROWS)
        for kl in range(3 * NA_TILE_ROWS):
            kr = NA_TILE_ROWS * (t_rep - 1) + kl
            dr = kl - NA_TILE_ROWS - ql
            if abs(dr) > NA_WIN_ROWS - 1:
                blk = masked
            else:
                ok = ((kr >= rstart) & (kr < rstart + NA_WIN_ROWS)).astype(F32)
                blk = toeplitz[dr] * ok + MASK_VALUE * (1.0 - ok)
            o_ref[0, 0, ql * GRID_W:(ql + 1) * GRID_W, kl * GRID_W:(kl + 1) * GRID_W] = blk


def _rpb_table(rpb, n_tiles, grid_rows):
    heads = rpb.shape[0]
    tq = NA_TILE_ROWS * GRID_W
    kern = functools.partial(_rpb_table_kernel, n_tiles=n_tiles, grid_rows=grid_rows)
    return pl.pallas_call(
        kern,
        grid=(3, heads),
        in_specs=[pl.BlockSpec(memory_space=pltpu.SMEM)],
        out_specs=pl.BlockSpec((1, 1, tq, 3 * tq), lambda v, h: (v, h, 0, 0)),
        out_shape=jax.ShapeDtypeStruct((3, heads, tq, 3 * tq), F32),
        compiler_params=_params("arbitrary", "arbitrary"),
        name="na_rpb_table",
    )(rpb.reshape(heads, -1))


def _na_kernel(q_ref, kp_ref, kc_ref, kn_ref, kx_ref, vp_ref, vc_ref, vn_ref, vx_ref, tab_ref, o_ref, *, heads, dh):
    tq = q_ref.shape[0]
    k_refs = (kp_ref, kc_ref, kn_ref, kx_ref)
    v_refs = (vp_ref, vc_ref, vn_ref, vx_ref)
    ones = jnp.ones((tq, dh), BF16)

    def scores(h):
        sl = slice(h * dh, (h + 1) * dh)
        q = q_ref[:, sl]
        return [lax.dot_general(q, k_ref[:, sl], (((1,), (1,)), ((), ())), preferred_element_type=F32)
                for k_ref in k_refs]

    pending = [scores(h) for h in range(min(NA_LEAD, heads))]
    for h in range(heads):
        sl = slice(h * dh, (h + 1) * dh)
        s = pending.pop(0)
        if h + NA_LEAD < heads:
            pending.append(scores(h + NA_LEAD))
        s = [s[j] + tab_ref[0, h, :, j * tq:(j + 1) * tq] for j in range(3)] + [s[3]]
        m = jnp.max(jnp.maximum(jnp.maximum(s[0], s[1]), jnp.maximum(s[2], s[3])), axis=-1, keepdims=True)
        o = None
        for sj, v_ref in zip(s, v_refs):
            p = jnp.exp2(sj - m).astype(BF16)
            part = jnp.dot(p, jnp.concatenate([v_ref[:, sl], ones], axis=1), preferred_element_type=F32)
            o = part if o is None else o + part
        o_ref[:, sl] = (o[:, 0:dh] / o[:, dh:2 * dh]).astype(o_ref.dtype)


def _neighborhood_attention(qkv, table, *, batch, seq, ctx_len, heads, dh, hb):
    rows = qkv.shape[0]
    tq = NA_TILE_ROWS * GRID_W
    assert ctx_len == tq and seq % tq == 0
    n_tiles = seq // tq
    n_hg = heads // hb
    ctx_blk0 = batch * seq // ctx_len
    wb = hb * dh

    def var(t):
        return jnp.where(t == 0, 0, jnp.where(t == n_tiles - 1, 2, 1))

    def spec(col0, shift):
        if shift is None:
            return pl.BlockSpec((tq, wb), lambda b, g, t: (ctx_blk0 + b, col0 + g))
        return pl.BlockSpec((tq, wb), lambda b, g, t: (b * n_tiles + jnp.clip(t + shift, 0, n_tiles - 1), col0 + g))

    kern = functools.partial(_na_kernel, heads=hb, dh=dh)
    return pl.pallas_call(
        kern,
        grid=(batch, n_hg, n_tiles),
        in_specs=[spec(0, 0),
                  spec(n_hg, -1), spec(n_hg, 0), spec(n_hg, 1), spec(n_hg, None),
                  spec(2 * n_hg, -1), spec(2 * n_hg, 0), spec(2 * n_hg, 1), spec(2 * n_hg, None),
                  pl.BlockSpec((1, hb, tq, 3 * tq), lambda b, g, t: (var(t), g, 0, 0))],
        out_specs=pl.BlockSpec((tq, wb), lambda b, g, t: (b * n_tiles + t, g)),
        out_shape=jax.ShapeDtypeStruct((rows, heads * dh), BF16),
        compiler_params=_params("arbitrary", "arbitrary", "arbitrary"),
        name="na_attention",
    )(qkv, qkv, qkv, qkv, qkv, qkv, qkv, qkv, qkv, table)


def _final_norm_kernel(x_ref, g_ref, o_ref):
    o_ref[...] = _rms(x_ref[...], g_ref[...])


def _final_norm(x, g, n_row_tiles):
    bm = ROW_TILE
    d = x.shape[1]
    return pl.pallas_call(
        _final_norm_kernel,
        grid=(n_row_tiles,),
        in_specs=[pl.BlockSpec((bm, d), lambda m: (m, 0)), pl.BlockSpec((1, d), lambda m: (0, 0))],
        out_specs=pl.BlockSpec((bm, d), lambda m: (m, 0)),
        out_shape=jax.ShapeDtypeStruct((n_row_tiles * bm, d), F32),
        compiler_params=_params("arbitrary"),
        name="final_norm",
    )(x, g.reshape(1, d))


def _rope_tables(n_tok, rot_dim, batch, n_ctx_rows):
    quarter = rot_dim // 4
    t = jnp.arange(n_tok)
    row = (t // GRID_W).astype(F32)
    col = (t % GRID_W).astype(F32)
    inv = ROPE_THETA ** (-jnp.arange(quarter, dtype=F32) / quarter)
    ar = row[:, None] * inv
    ac = col[:, None] * inv
    ang = jnp.concatenate([ar, ar, ac, ac], axis=-1)
    cos, sin = jnp.cos(ang), jnp.sin(ang)
    first = (jnp.arange(rot_dim) % (2 * quarter)) < quarter
    sin_a = jnp.where(first, -sin, 0.0)
    sin_b = jnp.where(first, 0.0, sin)
    pad = LANES - rot_dim

    def full(tab, fill):
        tab = jnp.pad(tab, ((0, 0), (0, pad)), constant_values=fill)
        tab = jnp.tile(tab, (batch, 1))
        return jnp.concatenate([tab, jnp.full((n_ctx_rows, LANES), fill, F32)], axis=0)

    return full(cos, 1.0), full(sin_a, 0.0), full(sin_b, 0.0)


def kernel(x, c, ctx, c_ctx, ada_w, ada_b, mix_norm, ffn_norm, ffn_up, ffn_conv_w, ffn_conv_b, ffn_down, na_wqkv, na_rpb, na_wo, mla_wq_a, mla_q_norm, mla_wq_b, mla_wkv_a, mla_kv_norm, mla_wkv_b, mla_wo, gqa_wqkv, gqa_q_norm, gqa_k_norm, gqa_wo, final_norm):
    batch, seq, d = x.shape
    ctx_len = ctx.shape[1]
    depth = ada_w.shape[0]
    bm = ROW_TILE
    lat_rows = batch * seq
    ctx_rows = batch * ctx_len
    pad_rows = -(lat_rows + ctx_rows) % bm
    rows = lat_rows + ctx_rows + pad_rows
    assert seq % bm == 0 and batch < 8
    assert seq & (seq - 1) == 0 and ctx_len & (ctx_len - 1) == 0
    tiles_per_batch = seq // bm
    lat_tiles = lat_rows // bm
    all_tiles = rows // bm
    grid_rows = seq // GRID_W

    def mod_row(m):
        return jnp.minimum(m // tiles_per_batch, batch)

    xs = jnp.concatenate([x.reshape(lat_rows, d), ctx.reshape(ctx_rows, d), jnp.zeros((pad_rows, d), F32)], axis=0)
    c_rows = jnp.zeros((8, d), F32).at[:batch].set(c).at[batch].set(c_ctx)
    mod_all = _modulation(c_rows, ada_w, ada_b)

    cos_b, sa_b, sb_b = _rope_tables(seq, MLA_ROPE_DIM, batch, ctx_rows + pad_rows)
    cos_c, sa_c, sb_c = _rope_tables(seq, GQA_HEAD_DIM, batch, ctx_rows + pad_rows)

    def row_spec(width):
        return pl.BlockSpec((bm, width), lambda m, n: (m, 0))

    rope_specs = [row_spec(LANES)] * 3

    for i in range(depth):
        last = i == depth - 1
        kind, j = i % N_MIXERS, i // N_MIXERS
        mod = mod_all[i].reshape(8, 1, 6 * d)
        n_tiles = lat_tiles if last else all_tiles

        if kind == 0:
            scale = (d // NA_HEADS) ** -0.5 * LOG2_E
            colscale = jnp.concatenate([jnp.full((1, d), scale, F32), jnp.ones((1, 2 * d), F32)], axis=1)
            bn = d // 2
            qkv = _norm_matmul(
                xs, 0, d, mix_norm[i], mod, (1, 0), na_wqkv[j].astype(BF16), bn,
                [colscale], [pl.BlockSpec((1, bn), lambda m, n: (0, n))],
                jax.ShapeDtypeStruct((rows, 3 * d), BF16), pl.BlockSpec((bm, bn), lambda m, n: (m, n)),
                _epi_colscale, all_tiles, mod_row, "na_qkv")
            table = _rpb_table(na_rpb[j], seq // (NA_TILE_ROWS * GRID_W), grid_rows)
            dh = d // NA_HEADS
            o = _neighborhood_attention(qkv, table, batch=batch, seq=seq, ctx_len=ctx_len,
                                        heads=NA_HEADS, dh=dh, hb=NA_HEADS_PER_STEP)
            if not last:
                o = _ctx_attention(qkv, False, qkv, qkv, o, batch=batch, groups=NA_HEADS, rep=1, dk=dh, dv=dh,
                                   qcol0=0, kcol0=NA_HEADS, vcol0=2 * NA_HEADS, ctx_blk0=lat_rows // ctx_len,
                                   ctx_len=ctx_len, name="na_ctx_attention")
            wo = na_wo[j]
        elif kind == 1:
            scale = (MLA_NOPE_DIM + MLA_ROPE_DIM) ** -0.5 * LOG2_E
            q_rank = mla_wq_a.shape[2]
            kv_rank = mla_kv_norm.shape[1]
            hq = 2 * LANES
            w1 = jnp.concatenate([mla_wq_a[j], mla_wkv_a[j], jnp.zeros((d, LANES - MLA_ROPE_DIM), F32)], axis=1)
            n1 = w1.shape[1]
            qc, kpe = _norm_matmul(
                xs, 0, d, mix_norm[i], mod, (1, 0), w1.astype(BF16), n1,
                [cos_b, sa_b, sb_b], rope_specs,
                (jax.ShapeDtypeStruct((rows, q_rank + kv_rank), F32), jax.ShapeDtypeStruct((rows, LANES), BF16)),
                (pl.BlockSpec((bm, q_rank + kv_rank), lambda m, n: (m, 0)), pl.BlockSpec((bm, LANES), lambda m, n: (m, 0))),
                _epi_mla_a, all_tiles, mod_row, "mla_a")
            wqb = mla_wq_b[j].reshape(q_rank, MLA_HEADS, MLA_NOPE_DIM + MLA_ROPE_DIM)
            wqb = jnp.pad(wqb, ((0, 0), (0, 0), (0, hq - MLA_NOPE_DIM - MLA_ROPE_DIM))).reshape(q_rank, MLA_HEADS * hq)
            bn = 4 * hq
            qt = _norm_matmul(
                qc, 0, q_rank, mla_q_norm[j], None, None, wqb.astype(BF16), bn,
                [cos_b, sa_b, sb_b], rope_specs,
                jax.ShapeDtypeStruct((MLA_HEADS, hq, rows), BF16),
                pl.BlockSpec((bn // hq, hq, bm), lambda m, n: (n, 0, m)),
                functools.partial(_epi_mla_q, scale=scale), all_tiles, mod_row, "mla_q")
            wkvb = mla_wkv_b[j].reshape(kv_rank, MLA_HEADS, MLA_NOPE_DIM + MLA_V_DIM)
            wk = wkvb[:, :, :MLA_NOPE_DIM].reshape(kv_rank, MLA_HEADS * MLA_NOPE_DIM)
            wv = wkvb[:, :, MLA_NOPE_DIM:].reshape(kv_rank, MLA_HEADS * MLA_V_DIM)
            bnk = 8 * MLA_NOPE_DIM
            k = _norm_matmul(
                qc, 1, kv_rank, mla_kv_norm[j], None, None, wk.astype(BF16), bnk,
                [kpe], [row_spec(LANES)],
                jax.ShapeDtypeStruct((rows, MLA_HEADS * hq), BF16), pl.BlockSpec((bm, 2 * bnk), lambda m, n: (m, n)),
                _epi_mla_k, all_tiles, mod_row, "mla_k")
            v = _norm_matmul(
                qc, 1, kv_rank, mla_kv_norm[j], None, None, wv.astype(BF16), bnk,
                [], [],
                jax.ShapeDtypeStruct((rows, MLA_HEADS * MLA_V_DIM), BF16), pl.BlockSpec((bm, bnk), lambda m, n: (m, n)),
                _epi_plain, all_tiles, mod_row, "mla_v")
            out_cols = MLA_HEADS * MLA_V_DIM
            o = _attention_t(qt, k, v, batch=batch, seq=seq, ctx_len=ctx_len, groups=MLA_HEADS, dv=MLA_V_DIM,
                             kcol0=0, vcol0=0, bq=min(FLASH_Q_TILE_MLA, seq), bk=FLASH_KEY_BLOCK, chunk=FLASH_CHUNK,
                             out_cols=out_cols, name="mla_attention")
            if not last:
                o = _ctx_attention(qt, True, k, v, o, batch=batch, groups=MLA_HEADS, rep=1, dk=hq, dv=MLA_V_DIM,
                                   qcol0=None, kcol0=0, vcol0=0, ctx_blk0=lat_rows // ctx_len, ctx_len=ctx_len,
                                   name="mla_ctx_attention")
            wo = mla_wo[j]
        else:
            scale = GQA_HEAD_DIM ** -0.5 * LOG2_E
            rep = GQA_HEADS // GQA_KV_HEADS
            nq_cols = GQA_HEADS * GQA_HEAD_DIM
            nk_cols = GQA_KV_HEADS * GQA_HEAD_DIM
            bn = nk_cols
            gain = jnp.concatenate([jnp.tile(gqa_q_norm[j], GQA_HEADS) * scale, jnp.tile(gqa_k_norm[j], GQA_KV_HEADS),
                                    jnp.ones((nk_cols,), F32)]).reshape(1, -1)
            dh = GQA_HEAD_DIM
            n_q_tiles = nq_cols // bn
            qt, kv = _norm_matmul(
                xs, 0, d, mix_norm[i], mod, (1, 0), gqa_wqkv[j].astype(BF16), bn,
                [gain, cos_c, sa_c, sb_c], [pl.BlockSpec((1, bn), lambda m, n: (0, n))] + rope_specs,
                (jax.ShapeDtypeStruct((GQA_HEADS, dh, rows), BF16), jax.ShapeDtypeStruct((rows, 2 * nk_cols), BF16)),
                (pl.BlockSpec((bn // dh, dh, bm), lambda m, n: (jnp.minimum(n, n_q_tiles - 1), 0, m)),
                 pl.BlockSpec((bm, bn), lambda m, n: (m, jnp.maximum(n - n_q_tiles, 0)))),
                functools.partial(_epi_gqa, n_q_tiles=n_q_tiles), all_tiles, mod_row, "gqa_qkv")
            o = _attention_t(qt, kv, kv, batch=batch, seq=seq, ctx_len=ctx_len, groups=GQA_KV_HEADS, dv=dh,
                             kcol0=0, vcol0=GQA_KV_HEADS, bq=min(FLASH_Q_TILE_GQA, seq), bk=FLASH_KEY_BLOCK,
                             chunk=FLASH_CHUNK,
                             out_cols=nq_cols, name="gqa_attention")
            if not last:
                o = _ctx_attention(qt, True, kv, kv, o, batch=batch, groups=GQA_KV_HEADS, rep=rep, dk=dh, dv=dh,
                                   qcol0=None, kcol0=0, vcol0=GQA_KV_HEADS, ctx_blk0=lat_rows // ctx_len,
                                   ctx_len=ctx_len, name="gqa_ctx_attention")
            wo = gqa_wo[j]

        xs = _matmul_residual(o, wo.astype(BF16), xs, mod, 2, d // 2, n_tiles, mod_row, "attn_out")
        a = _ffn_up(xs, ffn_norm[i], mod, ffn_up[i].astype(BF16), ffn_conv_w[i], ffn_conv_b[i], FFN_COL_TILE,
                    n_tiles, mod_row, lat_rows, seq, ctx_len, "ffn_up")
        xs = _matmul_residual(a, ffn_down[i].astype(BF16), xs, mod, 5, d // 4, n_tiles, mod_row, "ffn_down")

    out = _final_norm(xs, final_norm, lat_tiles)
    return out.reshape(batch, seq, d)
```
